```python
import math
import jax, jax.numpy as jnp
from jax import lax
import numpy as np

D_MODEL = 1024
BATCH = 16
SEQ = 2048
DEPTH = 4

N_EVEN = (DEPTH + 1) // 2
N_ODD = DEPTH // 2

CONV_WIDTH = D_MODEL // 2
CONV_KERNEL = 31
SSM_WIDTH = D_MODEL // 2
SSM_GROUP = 16
SSM_GROUPS = SSM_WIDTH // SSM_GROUP
SSM_STATE = 64
EVEN_IN = 3 * CONV_WIDTH + 2 * SSM_WIDTH
EVEN_MIX = CONV_WIDTH + SSM_WIDTH
ATTN_HEAD_DIM = 128
ATTN_PATTERNS = ((128, 1), (512, 4), (2048, 16))
N_PATTERNS = len(ATTN_PATTERNS)
HEADS_PER_PATTERN = D_MODEL // ATTN_HEAD_DIM
ATTN_QKV = N_PATTERNS * HEADS_PER_PATTERN * ATTN_HEAD_DIM
ATTN_WIDTH = HEADS_PER_PATTERN * ATTN_HEAD_DIM
ODD_IN = 3 * ATTN_QKV + ATTN_WIDTH
ATTN_SCALE = ATTN_HEAD_DIM ** -0.5
ROPE_THETA = 10000.0
EPS = 1e-6
NEG_INF = -1e30

kernel_name = "hybrid_conv_s5_dilated_attn_trunk"


def rms_norm(x, w):
    xf = x.astype(jnp.float32)
    y = xf * lax.rsqrt(jnp.mean(xf * xf, axis=-1, keepdims=True) + EPS)
    return (y * w.astype(jnp.float32)).astype(x.dtype)


def layer_norm(x, w, b):
    xf = x.astype(jnp.float32)
    mu = jnp.mean(xf, axis=-1, keepdims=True)
    xc = xf - mu
    y = xc * lax.rsqrt(jnp.mean(xc * xc, axis=-1, keepdims=True) + EPS)
    return (y * w.astype(jnp.float32) + b.astype(jnp.float32)).astype(x.dtype)


def rope_tables(positions):
    inv = ROPE_THETA ** (-jnp.arange(0, ATTN_HEAD_DIM, 2, dtype=jnp.float32) / ATTN_HEAD_DIM)
    ang = positions.astype(jnp.float32)[..., None] * inv
    return jnp.cos(ang), jnp.sin(ang)


def apply_rope(t, cos, sin):
    tf = t.astype(jnp.float32)
    half = ATTN_HEAD_DIM // 2
    t1, t2 = tf[..., :half], tf[..., half:]
    c = cos[:, :, None, None, :]
    s = sin[:, :, None, None, :]
    return jnp.concatenate([t1 * c - t2 * s, t2 * c + t1 * s], axis=-1).astype(t.dtype)


def conformer_conv(a_in, dw_w, dw_b, ln_w, ln_b):
    a = a_in[..., :CONV_WIDTH] * jax.nn.sigmoid(a_in[..., CONV_WIDTH:])
    y = lax.conv_general_dilated(
        a, dw_w[:, None, :], window_strides=(1,), padding=[(CONV_KERNEL - 1, 0)],
        dimension_numbers=('NWC', 'WIO', 'NWC'), feature_group_count=CONV_WIDTH) + dw_b
    return jax.nn.silu(layer_norm(y, ln_w, ln_b))


def _complex_affine_combine(e1, e2):
    a1r, a1i, b1r, b1i = e1
    a2r, a2i, b2r, b2i = e2
    ar = a2r * a1r - a2i * a1i
    ai = a2r * a1i + a2i * a1r
    br = a2r * b1r - a2i * b1i + b2r
    bi = a2r * b1i + a2i * b1r + b2i
    return (ar, ai, br, bi)


def s5_ssm(u, lam_re, lam_im, log_dt, b_re, b_im, c_re, c_im, d_skip):
    f32 = jnp.float32
    bsz, seq_len, _ = u.shape
    uf = u.astype(f32).reshape(bsz, seq_len, SSM_GROUPS, SSM_GROUP)
    lr, li = lam_re.astype(f32), lam_im.astype(f32)
    dt = jnp.exp(log_dt.astype(f32))[:, None]
    mag = jnp.exp(lr * dt)
    ar = mag * jnp.cos(li * dt)
    ai = mag * jnp.sin(li * dt)
    den = lr * lr + li * li
    nr = ar - 1.0
    kr = (nr * lr + ai * li) / den
    ki = (ai * lr - nr * li) / den
    br, bi = b_re.astype(f32), b_im.astype(f32)
    bbr = kr[..., None] * br - ki[..., None] * bi
    bbi = kr[..., None] * bi + ki[..., None] * br
    bu_r = jnp.einsum('blgh,gph->lbgp', uf, bbr)
    bu_i = jnp.einsum('blgh,gph->lbgp', uf, bbi)
    a_r = jnp.broadcast_to(ar[None, None], (seq_len, 1, SSM_GROUPS, SSM_STATE))
    a_i = jnp.broadcast_to(ai[None, None], (seq_len, 1, SSM_GROUPS, SSM_STATE))
    _, _, xr, xi = lax.associative_scan(_complex_affine_combine, (a_r, a_i, bu_r, bu_i), axis=0)
    y = (jnp.einsum('lbgp,ghp->blgh', xr, c_re.astype(f32))
         - jnp.einsum('lbgp,ghp->blgh', xi, c_im.astype(f32)))
    y = y.reshape(bsz, seq_len, SSM_WIDTH) + d_skip.astype(f32) * uf.reshape(bsz, seq_len, SSM_WIDTH)
    return y.astype(u.dtype)


def dilated_window_attention(q, k, v, window, dilation):
    bsz, seq_len, n_heads, hd = q.shape
    n_keys = window // dilation
    sub_len = seq_len // dilation
    n_blk = -(-sub_len // n_keys)
    pad_len = n_blk * n_keys

    def to_sub(t):
        t = t.reshape(bsz, sub_len, dilation, n_heads, hd).transpose(0, 2, 3, 1, 4)
        t = jnp.pad(t, ((0, 0), (0, 0), (0, 0), (0, pad_len - sub_len), (0, 0)))
        return t.reshape(bsz, dilation, n_heads, n_blk, n_keys, hd)

    def with_prev(t):
        prev = jnp.pad(t, ((0, 0), (0, 0), (0, 0), (1, 0), (0, 0), (0, 0)))[:, :, :, :-1]
        return jnp.concatenate([prev, t], axis=4)

    qb = to_sub(q)
    kk = with_prev(to_sub(k))
    vv = with_prev(to_sub(v))
    s = jnp.einsum('bdhnqe,bdhnke->bdhnqk', qb, kk).astype(jnp.float32) * ATTN_SCALE
    blk = jnp.arange(n_blk)[:, None, None]
    qi = jnp.arange(n_keys)[None, :, None]
    kj = jnp.arange(2 * n_keys)[None, None, :]
    valid = (kj >= qi) & (kj <= qi + n_keys) & ((blk > 0) | (kj >= n_keys))
    s = jnp.where(valid, s, NEG_INF)
    m = jnp.max(s, axis=-1)
    p = jnp.exp(s - m[..., None])
    den = jnp.sum(p, axis=-1)
    o = jnp.einsum('bdhnqk,bdhnke->bdhnqe', p, vv.astype(jnp.float32))

    def back(t):
        t = t.reshape((bsz, dilation, n_heads, pad_len) + t.shape[5:])[:, :, :, :sub_len]
        perm = (0, 3, 1, 2) + tuple(range(4, t.ndim))
        return t.transpose(perm).reshape((bsz, seq_len, n_heads) + t.shape[4:])

    return back(m), back(den), back(o)


def even_mixer(h, w_in, dw_w, dw_b, ln_w, ln_b, lam_re, lam_im, log_dt,
               b_re, b_im, c_re, c_im, d_skip, glu_w, glu_b, w_out):
    proj = h @ w_in
    a_in = proj[..., :2 * CONV_WIDTH]
    a_z = proj[..., 2 * CONV_WIDTH:3 * CONV_WIDTH]
    u = proj[..., 3 * CONV_WIDTH:3 * CONV_WIDTH + SSM_WIDTH]
    b_z = proj[..., 3 * CONV_WIDTH + SSM_WIDTH:]
    ya = conformer_conv(a_in, dw_w, dw_b, ln_w, ln_b) * jax.nn.silu(a_z)
    ys = jax.nn.gelu(s5_ssm(u, lam_re, lam_im, log_dt, b_re, b_im, c_re, c_im, d_skip), approximate=False)
    ys = ys * jax.nn.sigmoid(ys @ glu_w + glu_b)
    ys = ys * jax.nn.silu(b_z)
    return jnp.concatenate([ya, ys], axis=-1) @ w_out


def odd_mixer(h, cos, sin, w_in, q_norm_w, k_norm_w, w_out):
    bsz, seq_len, _ = h.shape
    proj = h @ w_in
    shp = (bsz, seq_len, N_PATTERNS, HEADS_PER_PATTERN, ATTN_HEAD_DIM)
    q = proj[..., :ATTN_QKV].reshape(shp)
    k = proj[..., ATTN_QKV:2 * ATTN_QKV].reshape(shp)
    v = proj[..., 2 * ATTN_QKV:3 * ATTN_QKV].reshape(shp)
    z = proj[..., 3 * ATTN_QKV:]
    q = apply_rope(rms_norm(q, q_norm_w), cos, sin)
    k = apply_rope(rms_norm(k, k_norm_w), cos, sin)
    stats = [dilated_window_attention(q[:, :, g], k[:, :, g], v[:, :, g], win, dil)
             for g, (win, dil) in enumerate(ATTN_PATTERNS)]
    m_all = jnp.stack([st[0] for st in stats])
    s_all = jnp.stack([st[1] for st in stats])
    o_all = jnp.stack([st[2] for st in stats])
    wgt = jnp.exp(m_all - jnp.max(m_all, axis=0, keepdims=True))
    den = jnp.sum(wgt * s_all, axis=0)
    o = jnp.sum(wgt[..., None] * o_all, axis=0) / den[..., None]
    o = o.reshape(bsz, seq_len, ATTN_WIDTH).astype(h.dtype) * jax.nn.silu(z)
    return o @ w_out


def setup_inputs(seed: int = 0) -> dict:
    key = jax.random.key(seed)
    ks = jax.random.split(key, 32)
    f32 = jnp.float32
    nrm = lambda k, shp, sc: jax.random.normal(k, shp, f32) * sc
    x = nrm(ks[0], (BATCH, SEQ, D_MODEL), 1.0)
    c = nrm(ks[1], (BATCH, D_MODEL), 1.0)
    offset = jax.random.randint(ks[2], (BATCH, 1), 0, 4096)
    positions = (offset + jnp.arange(SEQ)[None, :]).astype(jnp.int32)
    mod_w = nrm(ks[3], (DEPTH, D_MODEL, 3 * D_MODEL), 0.1 * D_MODEL ** -0.5)
    mod_b = jnp.concatenate([nrm(ks[4], (DEPTH, 2 * D_MODEL), 0.02),
                             1.0 + nrm(ks[5], (DEPTH, D_MODEL), 0.02)], axis=-1)
    norm_w = 1.0 + nrm(ks[6], (DEPTH, D_MODEL), 0.02)
    even_w_in = nrm(ks[7], (N_EVEN, D_MODEL, EVEN_IN), D_MODEL ** -0.5)
    conv_dw_w = nrm(ks[8], (N_EVEN, CONV_KERNEL, CONV_WIDTH), CONV_KERNEL ** -0.5)
    conv_dw_b = nrm(ks[9], (N_EVEN, CONV_WIDTH), 0.02)
    conv_ln_w = 1.0 + nrm(ks[10], (N_EVEN, CONV_WIDTH), 0.02)
    conv_ln_b = nrm(ks[11], (N_EVEN, CONV_WIDTH), 0.02)
    ssm_lam_re = -0.5 + nrm(ks[12], (N_EVEN, SSM_GROUPS, SSM_STATE), 0.01)
    ssm_lam_im = (jnp.pi * jnp.arange(SSM_STATE, dtype=f32))[None, None, :] + nrm(ks[13], (N_EVEN, SSM_GROUPS, SSM_STATE), 0.01)
    ssm_log_dt = jax.random.uniform(ks[14], (N_EVEN, SSM_GROUPS), f32, math.log(1e-3), math.log(1e-1))
    ssm_b_re = nrm(ks[15], (N_EVEN, SSM_GROUPS, SSM_STATE, SSM_GROUP), (2 * SSM_GROUP) ** -0.5)
    ssm_b_im = nrm(ks[16], (N_EVEN, SSM_GROUPS, SSM_STATE, SSM_GROUP), (2 * SSM_GROUP) ** -0.5)
    ssm_c_re = nrm(ks[17], (N_EVEN, SSM_GROUPS, SSM_GROUP, SSM_STATE), (2 * SSM_STATE) ** -0.5)
    ssm_c_im = nrm(ks[18], (N_EVEN, SSM_GROUPS, SSM_GROUP, SSM_STATE), (2 * SSM_STATE) ** -0.5)
    ssm_d = nrm(ks[19], (N_EVEN, SSM_WIDTH), 1.0)
    ssm_glu_w = nrm(ks[20], (N_EVEN, SSM_WIDTH, SSM_WIDTH), SSM_WIDTH ** -0.5)
    ssm_glu_b = nrm(ks[21], (N_EVEN, SSM_WIDTH), 0.02)
    even_w_out = nrm(ks[22], (N_EVEN, EVEN_MIX, D_MODEL), EVEN_MIX ** -0.5)
    attn_w_in = nrm(ks[23], (N_ODD, D_MODEL, ODD_IN), D_MODEL ** -0.5)
    attn_q_norm_w = 1.0 + nrm(ks[24], (N_ODD, ATTN_HEAD_DIM), 0.02)
    attn_k_norm_w = 1.0 + nrm(ks[25], (N_ODD, ATTN_HEAD_DIM), 0.02)
    attn_w_out = nrm(ks[26], (N_ODD, ATTN_WIDTH, D_MODEL), ATTN_WIDTH ** -0.5)
    return {"x": x, "c": c, "positions": positions, "mod_w": mod_w, "mod_b": mod_b,
            "norm_w": norm_w, "even_w_in": even_w_in, "conv_dw_w": conv_dw_w,
            "conv_dw_b": conv_dw_b, "conv_ln_w": conv_ln_w, "conv_ln_b": conv_ln_b,
            "ssm_lam_re": ssm_lam_re, "ssm_lam_im": ssm_lam_im, "ssm_log_dt": ssm_log_dt,
            "ssm_b_re": ssm_b_re, "ssm_b_im": ssm_b_im, "ssm_c_re": ssm_c_re,
            "ssm_c_im": ssm_c_im, "ssm_d": ssm_d, "ssm_glu_w": ssm_glu_w,
            "ssm_glu_b": ssm_glu_b, "even_w_out": even_w_out, "attn_w_in": attn_w_in,
            "attn_q_norm_w": attn_q_norm_w, "attn_k_norm_w": attn_k_norm_w,
            "attn_w_out": attn_w_out}


def reference(x, c, positions, mod_w, mod_b, norm_w, even_w_in, conv_dw_w, conv_dw_b,
              conv_ln_w, conv_ln_b, ssm_lam_re, ssm_lam_im, ssm_log_dt, ssm_b_re, ssm_b_im,
              ssm_c_re, ssm_c_im, ssm_d, ssm_glu_w, ssm_glu_b, even_w_out, attn_w_in,
              attn_q_norm_w, attn_k_norm_w, attn_w_out):
    cos, sin = rope_tables(positions)
    for layer in range(DEPTH):
        mod = c @ mod_w[layer] + mod_b[layer]
        shift = mod[:, None, :D_MODEL]
        scale = mod[:, None, D_MODEL:2 * D_MODEL]
        gate = mod[:, None, 2 * D_MODEL:]
        h = rms_norm(x, norm_w[layer]) * (1.0 + scale) + shift
        i = layer // 2
        if layer % 2 == 0:
            out = even_mixer(h, even_w_in[i], conv_dw_w[i], conv_dw_b[i], conv_ln_w[i], conv_ln_b[i],
                             ssm_lam_re[i], ssm_lam_im[i], ssm_log_dt[i], ssm_b_re[i], ssm_b_im[i],
                             ssm_c_re[i], ssm_c_im[i], ssm_d[i], ssm_glu_w[i], ssm_glu_b[i],
                             even_w_out[i])
        else:
            out = odd_mixer(h, cos, sin, attn_w_in[i], attn_q_norm_w[i], attn_k_norm_w[i],
                            attn_w_out[i])
        x = x + gate * out
    return x
```

```python
import functools
import math

import jax
import jax.numpy as jnp
from jax import lax
from jax.experimental import pallas as pl
from jax.experimental.pallas import tpu as pltpu

F32 = jnp.float32
BF16 = jnp.bfloat16

D_MODEL = 1024
DEPTH = 4
CONV_WIDTH = 512
CONV_KERNEL = 31
SSM_WIDTH = 512
SSM_GROUP = 16
SSM_GROUPS = 32
SSM_STATE = 64
HEAD_DIM = 128
ATTN_PATTERNS = ((128, 1), (512, 4), (2048, 16))
N_PATTERNS = 3
HEADS = 8
ATTN_QKV = N_PATTERNS * HEADS * HEAD_DIM
ATTN_SCALE = HEAD_DIM ** -0.5
ROPE_THETA = 10000.0
EPS = 1e-6
NEG_INF = -1e30

LANES = 128
WINDOW_KEYS = 128
SSM_CHUNK = 16
SSM_LANE_GROUPS = LANES // SSM_GROUP
SSM_BLOCKS = SSM_WIDTH // LANES
SSM_STATE_COLS = SSM_LANE_GROUPS * SSM_STATE
SSM_BATCH = 4
ROW_TILE = 512
CONV_HALO = 32
CONV_ROWS = 32
PROJ_ROWS = 256
VMEM_LIMIT = 56 * 1024 * 1024


def _params(n_axes, vmem=VMEM_LIMIT):
    return pltpu.CompilerParams(dimension_semantics=("arbitrary",) * n_axes, vmem_limit_bytes=vmem)


def _adaln(x, nw, scale, shift):
    ms = jnp.mean(x * x, axis=-1, keepdims=True)
    return (x * lax.rsqrt(ms + EPS) * nw) * (1.0 + scale) + shift


def _dot(a, b):
    return jnp.dot(a, b, preferred_element_type=F32)


def _mod_kernel(c_ref, w_ref, b_ref, o_ref):
    o_ref[0] = jnp.dot(c_ref[...], w_ref[0], preferred_element_type=F32,
                       precision=lax.Precision.HIGHEST) + b_ref[0]


def _modulation(c, mod_w, mod_b):
    bsz = c.shape[0]
    nblk = 3 * D_MODEL // D_MODEL
    return pl.pallas_call(
        _mod_kernel,
        grid=(DEPTH, nblk),
        in_specs=[pl.BlockSpec((bsz, D_MODEL), lambda l, j: (0, 0)),
                  pl.BlockSpec((1, D_MODEL, D_MODEL), lambda l, j: (l, 0, j)),
                  pl.BlockSpec((1, 1, D_MODEL), lambda l, j: (l, 0, j))],
        out_specs=pl.BlockSpec((1, bsz, D_MODEL), lambda l, j: (l, 0, j)),
        out_shape=jax.ShapeDtypeStruct((DEPTH, bsz, 3 * D_MODEL), F32),
        compiler_params=_params(2),
        name="modulation",
    )(c, mod_w, mod_b.reshape(DEPTH, 1, 3 * D_MODEL))


def _rope_kernel(pos_ref, inv_ref, sign_ref, cos_ref, sin_ref):
    ang = pos_ref[0].astype(F32) * inv_ref[...]
    cos_ref[0] = jnp.cos(ang)
    sin_ref[0] = jnp.sin(ang) * sign_ref[...]


def _rope_tables(positions):
    bsz, seq = positions.shape
    inv = ROPE_THETA ** (-jnp.arange(0, HEAD_DIM, 2, dtype=F32) / HEAD_DIM)
    inv2 = jnp.concatenate([inv, inv]).reshape(1, HEAD_DIM)
    sign = jnp.concatenate([-jnp.ones((HEAD_DIM // 2,), F32), jnp.ones((HEAD_DIM // 2,), F32)]).reshape(1, HEAD_DIM)
    tab = jax.ShapeDtypeStruct((bsz, seq, HEAD_DIM), F32)
    return pl.pallas_call(
        _rope_kernel,
        grid=(bsz, seq // ROW_TILE),
        in_specs=[pl.BlockSpec((1, ROW_TILE, 1), lambda b, i: (b, i, 0)),
                  pl.BlockSpec((1, HEAD_DIM), lambda b, i: (0, 0)),
                  pl.BlockSpec((1, HEAD_DIM), lambda b, i: (0, 0))],
        out_specs=[pl.BlockSpec((1, ROW_TILE, HEAD_DIM), lambda b, i: (b, i, 0))] * 2,
        out_shape=[tab, tab],
        compiler_params=_params(2),
        name="rope_tables",
    )(positions.reshape(bsz, seq, 1), inv2, sign)


def _norm_kernel(x_ref, nw_ref, sc_ref, sh_ref, h_ref):
    h_ref[0] = _adaln(x_ref[0], nw_ref[...], sc_ref[0], sh_ref[0]).astype(BF16)


def _first_norm(x, nw, scale, shift):
    bsz, seq, _ = x.shape
    vec = pl.BlockSpec((1, 1, D_MODEL), lambda b, i: (b, 0, 0))
    row = pl.BlockSpec((1, ROW_TILE, D_MODEL), lambda b, i: (b, i, 0))
    return pl.pallas_call(
        _norm_kernel,
        grid=(bsz, seq // ROW_TILE),
        in_specs=[row, pl.BlockSpec((1, D_MODEL), lambda b, i: (0, 0)), vec, vec],
        out_specs=row,
        out_shape=jax.ShapeDtypeStruct(x.shape, BF16),
        compiler_params=_params(2),
        name="first_norm",
    )(x, nw, scale, shift)


def _even_in_kernel(h_ref, w_ref, dww_ref, dwb_ref, lnw_ref, lnb_ref,
                    ya_ref, u_ref, bz_ref, conv_scr, az_scr):
    i = pl.program_id(1)
    h = h_ref[0]
    cw = CONV_WIDTH

    @pl.when(i == 0)
    def _():
        conv_scr[0:CONV_HALO, :] = jnp.zeros((CONV_HALO, cw), F32)

    a1 = _dot(h, w_ref[:, 0:cw])
    a2 = _dot(h, w_ref[:, cw:2 * cw])
    conv_scr[CONV_HALO:CONV_HALO + ROW_TILE, :] = a1 * jax.nn.sigmoid(a2)
    az_scr[...] = jax.nn.silu(_dot(h, w_ref[:, 2 * cw:3 * cw]))
    u_ref[0] = _dot(h, w_ref[:, 3 * cw:4 * cw])
    bz_ref[0] = _dot(h, w_ref[:, 4 * cw:5 * cw])

    first_tap = CONV_HALO - (CONV_KERNEL - 1)

    def conv_block(r, carry):
        r0 = pl.multiple_of(r * CONV_ROWS, CONV_ROWS)
        acc = jnp.broadcast_to(dwb_ref[...], (CONV_ROWS, cw))
        win = conv_scr[pl.ds(r0, CONV_ROWS + CONV_HALO), :]
        for k in range(CONV_KERNEL):
            acc = acc + dww_ref[k:k + 1, :] * win[first_tap + k:first_tap + k + CONV_ROWS, :]
        mu = jnp.mean(acc, axis=-1, keepdims=True)
        xc = acc - mu
        y = xc * lax.rsqrt(jnp.mean(xc * xc, axis=-1, keepdims=True) + EPS)
        y = y * lnw_ref[...] + lnb_ref[...]
        ya_ref[0, pl.ds(r0, CONV_ROWS), :] = (jax.nn.silu(y) * az_scr[pl.ds(r0, CONV_ROWS), :]).astype(BF16)
        return carry

    lax.fori_loop(0, ROW_TILE // CONV_ROWS, conv_block, 0)
    conv_scr[0:CONV_HALO, :] = conv_scr[ROW_TILE:ROW_TILE + CONV_HALO, :]


def _even_in(h, w_in, dw_w, dw_b, ln_w, ln_b):
    bsz, seq, _ = h.shape
    cw = CONV_WIDTH
    const = lambda shape: pl.BlockSpec(shape, lambda b, i: (0,) * len(shape))
    row = lambda width: pl.BlockSpec((1, ROW_TILE, width), lambda b, i: (b, i, 0))
    return pl.pallas_call(
        _even_in_kernel,
        grid=(bsz, seq // ROW_TILE),
        in_specs=[row(D_MODEL), const(w_in.shape), const((CONV_KERNEL, cw)),
                  const((1, cw)), const((1, cw)), const((1, cw))],
        out_specs=[row(cw), row(cw), row(cw)],
        out_shape=[jax.ShapeDtypeStruct((bsz, seq, cw), BF16),
                   jax.ShapeDtypeStruct((bsz, seq, cw), F32),
                   jax.ShapeDtypeStruct((bsz, seq, cw), F32)],
        scratch_shapes=[pltpu.VMEM((CONV_HALO + ROW_TILE, cw), F32),
                        pltpu.VMEM((ROW_TILE, cw), F32)],
        compiler_params=_params(2),
        name="even_in",
    )(h, w_in, dw_w, dw_b.reshape(1, cw), ln_w.reshape(1, cw), ln_b.reshape(1, cw))


def _ssm_matrices(lam_re, lam_im, log_dt, b_re, b_im, c_re, c_im):
    hp = lax.Precision.HIGHEST
    t = SSM_CHUNK
    lr, li = lam_re.astype(F32), lam_im.astype(F32)
    dt = jnp.exp(log_dt.astype(F32))[:, None]

    def a_pow(k):
        kk = k.astype(F32)[:, None, None]
        mag = jnp.exp(kk * (lr * dt)[None])
        ang = kk * (li * dt)[None]
        return mag * jnp.cos(ang), mag * jnp.sin(ang)

    ar, ai = a_pow(jnp.ones((1,), F32))
    ar, ai = ar[0], ai[0]
    den = lr * lr + li * li
    nr = ar - 1.0
    kr = (nr * lr + ai * li) / den
    ki = (ai * lr - nr * li) / den
    br, bi = b_re.astype(F32), b_im.astype(F32)
    bbr = kr[..., None] * br - ki[..., None] * bi
    bbi = kr[..., None] * bi + ki[..., None] * br
    cr, ci = c_re.astype(F32), c_im.astype(F32)

    pr, pi = a_pow(jnp.arange(t + 1))
    wr = pr[:t, :, :, None] * bbr[None] - pi[:t, :, :, None] * bbi[None]
    wi = pr[:t, :, :, None] * bbi[None] + pi[:t, :, :, None] * bbr[None]
    kk = (jnp.einsum('gop,kgpi->kgoi', cr, wr, precision=hp)
          - jnp.einsum('gop,kgpi->kgoi', ci, wi, precision=hp))

    nb, gl, hh, pp = SSM_BLOCKS, SSM_LANE_GROUPS, SSM_GROUP, SSM_STATE
    eye = jnp.eye(gl, dtype=F32)
    lag = jnp.arange(t)[None, :] - jnp.arange(t)[:, None]
    toe = jnp.where((lag >= 0)[:, :, None, None, None], kk[jnp.clip(lag, 0, t - 1)], 0.0)
    toe = toe.reshape(t, t, nb, gl, hh, hh)
    m_intra = jnp.einsum('stcgoi,gh->csgitho', toe, eye).reshape(nb, t * LANES, t * LANES)

    sr = wr[::-1].reshape(t, nb, gl, pp, hh)
    si = wi[::-1].reshape(t, nb, gl, pp, hh)
    ms_r = jnp.einsum('scgpi,gh->csgihp', sr, eye).reshape(nb, t * LANES, gl * pp)
    ms_i = jnp.einsum('scgpi,gh->csgihp', si, eye).reshape(nb, t * LANES, gl * pp)
    m_all = jnp.concatenate([m_intra, ms_r, ms_i], axis=-1).astype(BF16)

    qr = pr[1:, :, None, :] * cr[None] - pi[1:, :, None, :] * ci[None]
    qi = pr[1:, :, None, :] * ci[None] + pi[1:, :, None, :] * cr[None]
    qr = qr.reshape(t, nb, gl, hh, pp)
    qi = qi.reshape(t, nb, gl, hh, pp)
    mi_r = jnp.einsum('tcgop,gh->cgptho', qr, eye).reshape(nb, gl * pp, t * LANES)
    mi_i = jnp.einsum('tcgop,gh->cgptho', -qi, eye).reshape(nb, gl * pp, t * LANES)
    m_in = jnp.concatenate([mi_r, mi_i], axis=1).astype(BF16)

    n_lvl = int(math.log2(2048 // t))
    sr2, si2 = a_pow(t * (2 ** jnp.arange(n_lvl)))
    ap_r = sr2.reshape(n_lvl, nb, gl * pp).transpose(1, 0, 2)
    ap_i = si2.reshape(n_lvl, nb, gl * pp).transpose(1, 0, 2)
    return m_all, m_in, ap_r, ap_i


def _ssm_kernel(u_ref, mall_ref, min_ref, apr_ref, api_ref, d_ref, y_ref,
                x_scr, carry_scr, zr_scr, zi_scr, *, n_chunks, n_levels):
    t = SSM_CHUNK
    sc = SSM_STATE_COLS
    tl = t * LANES
    for bb in range(SSM_BATCH):
        for t0 in range(t):
            x_scr[bb * n_chunks:(bb + 1) * n_chunks, t0 * LANES:(t0 + 1) * LANES] = (
                u_ref[bb, pl.ds(t0, n_chunks, stride=t), :].astype(BF16))
    x = x_scr[...]
    s_loc = _dot(x, mall_ref[0, :, tl:])

    zero = jnp.zeros((n_chunks, sc), F32)
    for p in range(2):
        zr_scr[p, 0:n_chunks, :] = zero
        zi_scr[p, 0:n_chunks, :] = zero
    for bb in range(SSM_BATCH):
        rows = slice(bb * n_chunks, (bb + 1) * n_chunks)
        zr_scr[0, n_chunks:, :] = s_loc[rows, :sc]
        zi_scr[0, n_chunks:, :] = s_loc[rows, sc:]
        for k in range(n_levels):
            src, dst = k % 2, 1 - (k % 2)
            sh = n_chunks - (1 << k)
            zr = zr_scr[src, n_chunks:, :]
            zi = zi_scr[src, n_chunks:, :]
            pr = zr_scr[src, sh:sh + n_chunks, :]
            pi = zi_scr[src, sh:sh + n_chunks, :]
            ar = apr_ref[0, k:k + 1, :]
            ai = api_ref[0, k:k + 1, :]
            zr_scr[dst, n_chunks:, :] = zr + ar * pr - ai * pi
            zi_scr[dst, n_chunks:, :] = zi + ar * pi + ai * pr
        fin = n_levels % 2
        carry_scr[rows, :sc] = zr_scr[fin, n_chunks - 1:2 * n_chunks - 1, :].astype(BF16)
        carry_scr[rows, sc:] = zi_scr[fin, n_chunks - 1:2 * n_chunks - 1, :].astype(BF16)

    carry = carry_scr[...]
    for tp in range(t // 2):
        cols = slice(tp * 2 * LANES, (tp + 1) * 2 * LANES)
        yc = _dot(x, mall_ref[0, :, cols]) + _dot(carry, min_ref[0, :, cols])
        for bb in range(SSM_BATCH):
            for j in range(2):
                y_ref[bb, pl.ds(2 * tp + j, n_chunks, stride=t), :] = (
                    yc[bb * n_chunks:(bb + 1) * n_chunks, j * LANES:(j + 1) * LANES])
    for bb in range(SSM_BATCH):
        y_ref[bb] = y_ref[bb] + d_ref[0] * u_ref[bb]


def _ssm(u, mats, d_skip):
    m_all, m_in, ap_r, ap_i = mats
    bsz, seq, _ = u.shape
    n_chunks = seq // SSM_CHUNK
    n_levels = ap_r.shape[1]
    nb = SSM_BATCH
    once = pl.Buffered(1)
    blk = pl.BlockSpec((nb, seq, LANES), lambda c, b: (b, 0, c))
    return pl.pallas_call(
        functools.partial(_ssm_kernel, n_chunks=n_chunks, n_levels=n_levels),
        grid=(SSM_BLOCKS, bsz // nb),
        in_specs=[blk,
                  pl.BlockSpec((1,) + m_all.shape[1:], lambda c, b: (c, 0, 0), pipeline_mode=once),
                  pl.BlockSpec((1,) + m_in.shape[1:], lambda c, b: (c, 0, 0), pipeline_mode=once),
                  pl.BlockSpec((1, n_levels, SSM_STATE_COLS), lambda c, b: (c, 0, 0)),
                  pl.BlockSpec((1, n_levels, SSM_STATE_COLS), lambda c, b: (c, 0, 0)),
                  pl.BlockSpec((1, 1, LANES), lambda c, b: (c, 0, 0))],
        out_specs=blk,
        out_shape=jax.ShapeDtypeStruct(u.shape, F32),
        scratch_shapes=[pltpu.VMEM((nb * n_chunks, SSM_CHUNK * LANES), BF16),
                        pltpu.VMEM((nb * n_chunks, 2 * SSM_STATE_COLS), BF16),
                        pltpu.VMEM((2, 2 * n_chunks, SSM_STATE_COLS), F32),
                        pltpu.VMEM((2, 2 * n_chunks, SSM_STATE_COLS), F32)],
        compiler_params=_params(2),
        name="ssm",
    )(u, m_all, m_in, ap_r, ap_i, d_skip.reshape(SSM_BLOCKS, 1, LANES))


def _finish(x_ref, gate_ref, out, nxt, xo_ref, ho_ref):
    xn = x_ref[0] + gate_ref[0] * out
    xo_ref[0] = xn
    if nxt is not None:
        nw_ref, sc_ref, sh_ref = nxt
        ho_ref[0] = _adaln(xn, nw_ref[...], sc_ref[0], sh_ref[0]).astype(BF16)


def _even_out_kernel(*refs, emit_h):
    if emit_h:
        (x_ref, ya_ref, ys_ref, bz_ref, gw_ref, gb_ref, wa_ref, ws_ref, gate_ref,
         nw_ref, sc_ref, sh_ref, xo_ref, ho_ref) = refs
        nxt = (nw_ref, sc_ref, sh_ref)
    else:
        (x_ref, ya_ref, ys_ref, bz_ref, gw_ref, gb_ref, wa_ref, ws_ref, gate_ref, xo_ref) = refs
        nxt, ho_ref = None, None
    y = ys_ref[0]
    y = 0.5 * y * (1.0 + lax.erf(y * (2.0 ** -0.5)))
    y = y * jax.nn.sigmoid(_dot(y.astype(BF16), gw_ref[...]) + gb_ref[...])
    y = y * jax.nn.silu(bz_ref[0])
    out = _dot(ya_ref[0], wa_ref[...]) + _dot(y.astype(BF16), ws_ref[...])
    _finish(x_ref, gate_ref, out, nxt, xo_ref, ho_ref)


def _odd_out_kernel(*refs, emit_h):
    if emit_h:
        (x_ref, h_ref, o_ref, wz_ref, wo_ref, gate_ref, nw_ref, sc_ref, sh_ref, xo_ref, ho_ref) = refs
        nxt = (nw_ref, sc_ref, sh_ref)
    else:
        (x_ref, h_ref, o_ref, wz_ref, wo_ref, gate_ref, xo_ref) = refs
        nxt, ho_ref = None, None
    z = _dot(h_ref[0], wz_ref[...])
    g = o_ref[0] * jax.nn.silu(z)
    out = _dot(g.astype(BF16), wo_ref[...])
    _finish(x_ref, gate_ref, out, nxt, xo_ref, ho_ref)


def _tail_call(body, name, x, rows, consts, gate, nxt):
    bsz, seq, _ = x.shape
    emit_h = nxt is not None
    row = lambda a: pl.BlockSpec((1, ROW_TILE, a.shape[-1]), lambda b, i: (b, i, 0))
    const = lambda a: pl.BlockSpec(a.shape, lambda b, i: (0,) * a.ndim)
    vec = pl.BlockSpec((1, 1, D_MODEL), lambda b, i: (b, 0, 0))
    args = [x, *rows, *consts, gate]
    in_specs = [row(x)] + [row(a) for a in rows] + [const(a) for a in consts] + [vec]
    out_specs = [row(x)]
    out_shape = [jax.ShapeDtypeStruct(x.shape, F32)]
    if emit_h:
        nw, sc, sh = nxt
        args += [nw, sc, sh]
        in_specs += [const(nw), vec, vec]
        out_specs.append(row(x))
        out_shape.append(jax.ShapeDtypeStruct(x.shape, BF16))
    res = pl.pallas_call(
        functools.partial(body, emit_h=emit_h),
        grid=(bsz, seq // ROW_TILE),
        in_specs=in_specs, out_specs=out_specs, out_shape=out_shape,
        compiler_params=_params(2),
        name=name,
    )(*args)
    return (res[0], res[1]) if emit_h else (res[0], None)


def _attn_kernel(h_ref, w_ref, cos_ref, sin_ref, qw_ref, kw_ref, o_ref,
                 q_scr, k_scr, v_scr, on_scr, m_scr, d_scr, bias_scr, *, seq):
    wk = WINDOW_KEYS
    qi = lax.broadcasted_iota(jnp.int32, (wk, 2 * wk), 0)
    kj = lax.broadcasted_iota(jnp.int32, (wk, 2 * wk), 1)
    band = (kj >= qi) & (kj <= qi + wk)
    bias_scr[0] = jnp.where(band, 0.0, NEG_INF)
    bias_scr[1] = jnp.where(band & (kj >= wk), 0.0, NEG_INF)

    for g, (window, dil) in enumerate(ATTN_PATTERNS):
        n_blk = (seq // dil) // wk
        w_off = g * 3 * HEAD_DIM

        def proj_body(c, carry):
            rows = pl.ds(pl.multiple_of(c * PROJ_ROWS, PROJ_ROWS), PROJ_ROWS)
            pr = _dot(h_ref[0, rows, :], w_ref[0, :, w_off:w_off + 3 * HEAD_DIM])
            cos = cos_ref[0, rows, :]
            sin = sin_ref[0, rows, :]

            def norm_rope(t, w):
                t = t * lax.rsqrt(jnp.mean(t * t, axis=-1, keepdims=True) + EPS) * w
                return t * cos + pltpu.roll(t, HEAD_DIM // 2, 1) * sin

            q_scr[rows, :] = norm_rope(pr[:, :HEAD_DIM], qw_ref[...]) * ATTN_SCALE
            k_scr[rows, :] = norm_rope(pr[:, HEAD_DIM:2 * HEAD_DIM], kw_ref[...])
            v_scr[rows, :] = pr[:, 2 * HEAD_DIM:]
            return carry

        lax.fori_loop(0, seq // PROJ_ROWS, proj_body, 0)

        def blk_body(j, carry):
            r = j // n_blk
            n = j % n_blk

            def rows_of(blk):
                if dil == 1:
                    return pl.ds(pl.multiple_of(blk * wk, wk), wk)
                return pl.ds(blk * (wk * dil) + r, wk, stride=dil)

            cur = rows_of(n)
            q = q_scr[cur, :].astype(BF16)
            kc = k_scr[cur, :].astype(BF16)
            vc = v_scr[cur, :].astype(BF16)
            if n_blk == 1:
                kk, vv = kc, vc
                bias = bias_scr[1, :, wk:]
            else:
                prev = rows_of(jnp.maximum(n - 1, 0))
                kk = jnp.concatenate([k_scr[prev, :].astype(BF16), kc], axis=0)
                vv = jnp.concatenate([v_scr[prev, :].astype(BF16), vc], axis=0)
                bias = bias_scr[jnp.where(n == 0, 1, 0)]
            s = lax.dot_general(q, kk, (((1,), (1,)), ((), ())), preferred_element_type=F32) + bias
            m = jnp.max(s, axis=-1, keepdims=True)
            p = jnp.exp(s - m)
            den = jnp.sum(p, axis=-1, keepdims=True)
            o = _dot(p.astype(BF16), vv)
            on_scr[g, cur, :] = o
            m_scr[g, cur, :] = jnp.broadcast_to(m, (wk, HEAD_DIM))
            d_scr[g, cur, :] = jnp.broadcast_to(den, (wk, HEAD_DIM))
            return carry

        lax.fori_loop(0, seq // wk, blk_body, 0)

    def merge_body(c, carry):
        rows = pl.ds(pl.multiple_of(c * PROJ_ROWS, PROJ_ROWS), PROJ_ROWS)
        ms = [m_scr[g, rows, :] for g in range(N_PATTERNS)]
        mx = jnp.maximum(jnp.maximum(ms[0], ms[1]), ms[2])
        num = jnp.zeros((PROJ_ROWS, HEAD_DIM), F32)
        den = jnp.zeros((PROJ_ROWS, HEAD_DIM), F32)
        for g in range(N_PATTERNS):
            wgt = jnp.exp(ms[g] - mx)
            num = num + wgt * on_scr[g, rows, :]
            den = den + wgt * d_scr[g, rows, :]
        o_ref[0, rows, :] = num / den
        return carry

    lax.fori_loop(0, seq // PROJ_ROWS, merge_body, 0)


def _attention(h, w_qkv, cos2, sin2, q_norm_w, k_norm_w):
    bsz, seq, _ = h.shape
    stat = pltpu.VMEM((N_PATTERNS, seq, HEAD_DIM), F32)
    head = pltpu.VMEM((seq, HEAD_DIM), F32)
    return pl.pallas_call(
        functools.partial(_attn_kernel, seq=seq),
        grid=(bsz, HEADS),
        in_specs=[pl.BlockSpec((1, seq, D_MODEL), lambda b, hd: (b, 0, 0)),
                  pl.BlockSpec((1, D_MODEL, 3 * N_PATTERNS * HEAD_DIM), lambda b, hd: (hd, 0, 0)),
                  pl.BlockSpec((1, seq, HEAD_DIM), lambda b, hd: (b, 0, 0)),
                  pl.BlockSpec((1, seq, HEAD_DIM), lambda b, hd: (b, 0, 0)),
                  pl.BlockSpec((1, HEAD_DIM), lambda b, hd: (0, 0)),
                  pl.BlockSpec((1, HEAD_DIM), lambda b, hd: (0, 0))],
        out_specs=pl.BlockSpec((1, seq, HEAD_DIM), lambda b, hd: (b, 0, hd)),
        out_shape=jax.ShapeDtypeStruct((bsz, seq, HEADS * HEAD_DIM), F32),
        scratch_shapes=[head, head, head, stat, stat, stat,
                        pltpu.VMEM((2, WINDOW_KEYS, 2 * WINDOW_KEYS), F32)],
        compiler_params=_params(2),
        name="attention",
    )(h, w_qkv, cos2, sin2, q_norm_w.reshape(1, HEAD_DIM), k_norm_w.reshape(1, HEAD_DIM))


def _head_major_qkv(w_in):
    w = w_in[:, :3 * ATTN_QKV].reshape(D_MODEL, 3, N_PATTERNS, HEADS, HEAD_DIM)
    return w.transpose(3, 0, 2, 1, 4).reshape(HEADS, D_MODEL, N_PATTERNS * 3 * HEAD_DIM).astype(BF16)


def kernel(x, c, positions, mod_w, mod_b, norm_w, even_w_in, conv_dw_w, conv_dw_b, conv_ln_w, conv_ln_b, ssm_lam_re, ssm_lam_im, ssm_log_dt, ssm_b_re, ssm_b_im, ssm_c_re, ssm_c_im, ssm_d, ssm_glu_w, ssm_glu_b, even_w_out, attn_w_in, attn_q_norm_w, attn_k_norm_w, attn_w_out):
    bsz, seq, _ = x.shape
    assert seq == 2048 and x.shape[-1] == D_MODEL and bsz % SSM_BATCH == 0
    mod = _modulation(c, mod_w, mod_b)
    shift = [mod[l, :, None, :D_MODEL] for l in range(DEPTH)]
    scale = [mod[l, :, None, D_MODEL:2 * D_MODEL] for l in range(DEPTH)]
    gate = [mod[l, :, None, 2 * D_MODEL:] for l in range(DEPTH)]
    nw = [norm_w[l].reshape(1, D_MODEL) for l in range(DEPTH)]
    cos2, sin2 = _rope_tables(positions)

    h = _first_norm(x, nw[0], scale[0], shift[0])
    for layer in range(DEPTH):
        i = layer // 2
        nxt = (nw[layer + 1], scale[layer + 1], shift[layer + 1]) if layer + 1 < DEPTH else None
        if layer % 2 == 0:
            ya, u, bz = _even_in(h, even_w_in[i].astype(BF16), conv_dw_w[i], conv_dw_b[i],
                                 conv_ln_w[i], conv_ln_b[i])
            mats = _ssm_matrices(ssm_lam_re[i], ssm_lam_im[i], ssm_log_dt[i], ssm_b_re[i], ssm_b_im[i],
                                 ssm_c_re[i], ssm_c_im[i])
            ys = _ssm(u, mats, ssm_d[i])
            w_out = even_w_out[i].astype(BF16)
            x, h = _tail_call(_even_out_kernel, "even_out", x, [ya, ys, bz],
                              [ssm_glu_w[i].astype(BF16), ssm_glu_b[i].reshape(1, SSM_WIDTH),
                               w_out[:CONV_WIDTH], w_out[CONV_WIDTH:]], gate[layer], nxt)
        else:
            o = _attention(h, _head_major_qkv(attn_w_in[i]), cos2, sin2, attn_q_norm_w[i], attn_k_norm_w[i])
            x, h = _tail_call(_odd_out_kernel, "odd_out", x, [h, o],
                              [attn_w_in[i][:, 3 * ATTN_QKV:].astype(BF16), attn_w_out[i].astype(BF16)],
                              gate[layer], nxt)
    return x
```

```python
import functools
import math

import jax
import jax.numpy as jnp
from jax import lax
from jax.experimental import pallas as pl
from jax.experimental.pallas import tpu as pltpu

F32 = jnp.float32
BF16 = jnp.bfloat16

D_MODEL = 1024
DEPTH = 4
CONV_WIDTH = 512
CONV_KERNEL = 31
SSM_WIDTH = 512
SSM_GROUP = 16
SSM_GROUPS = 32
SSM_STATE = 64
HEAD_DIM = 128
ATTN_PATTERNS = ((128, 1), (512, 4), (2048, 16))
N_PATTERNS = 3
HEADS = 8
ATTN_QKV = N_PATTERNS * HEADS * HEAD_DIM
ATTN_SCALE = HEAD_DIM ** -0.5
ROPE_THETA = 10000.0
EPS = 1e-6
NEG_INF = -1e30

LANES = 128
WINDOW_KEYS = 128
SSM_CHUNK = 16
SSM_LANE_GROUPS = LANES // SSM_GROUP
SSM_BLOCKS = SSM_WIDTH // LANES
SSM_STATE_COLS = SSM_LANE_GROUPS * SSM_STATE
SSM_BATCH = 4
ROW_TILE = 512
CONV_HALO = 32
CONV_ROWS = 32
PROJ_ROWS = 256
VMEM_LIMIT = 56 * 1024 * 1024


def _params(n_axes, vmem=VMEM_LIMIT):
    return pltpu.CompilerParams(dimension_semantics=("arbitrary",) * n_axes, vmem_limit_bytes=vmem)


def _adaln(x, nw, scale, shift):
    ms = jnp.mean(x * x, axis=-1, keepdims=True)
    return (x * lax.rsqrt(ms + EPS) * nw) * (1.0 + scale) + shift


def _dot(a, b):
    return jnp.dot(a, b, preferred_element_type=F32)


def _mod_kernel(c_ref, w_ref, b_ref, o_ref):
    o_ref[0] = jnp.dot(c_ref[...], w_ref[0], preferred_element_type=F32,
                       precision=lax.Precision.HIGHEST) + b_ref[0]


def _modulation(c, mod_w, mod_b):
    bsz = c.shape[0]
    nblk = 3 * D_MODEL // D_MODEL
    return pl.pallas_call(
        _mod_kernel,
        grid=(DEPTH, nblk),
        in_specs=[pl.BlockSpec((bsz, D_MODEL), lambda l, j: (0, 0)),
                  pl.BlockSpec((1, D_MODEL, D_MODEL), lambda l, j: (l, 0, j)),
                  pl.BlockSpec((1, 1, D_MODEL), lambda l, j: (l, 0, j))],
        out_specs=pl.BlockSpec((1, bsz, D_MODEL), lambda l, j: (l, 0, j)),
        out_shape=jax.ShapeDtypeStruct((DEPTH, bsz, 3 * D_MODEL), F32),
        compiler_params=_params(2),
        name="modulation",
    )(c, mod_w, mod_b.reshape(DEPTH, 1, 3 * D_MODEL))


def _rope_kernel(pos_ref, inv_ref, sign_ref, cos_ref, sin_ref):
    ang = pos_ref[0].astype(F32) * inv_ref[...]
    cos_ref[0] = jnp.cos(ang)
    sin_ref[0] = jnp.sin(ang) * sign_ref[...]


def _rope_tables(positions):
    bsz, seq = positions.shape
    inv = ROPE_THETA ** (-jnp.arange(0, HEAD_DIM, 2, dtype=F32) / HEAD_DIM)
    inv2 = jnp.concatenate([inv, inv]).reshape(1, HEAD_DIM)
    sign = jnp.concatenate([-jnp.ones((HEAD_DIM // 2,), F32), jnp.ones((HEAD_DIM // 2,), F32)]).reshape(1, HEAD_DIM)
    tab = jax.ShapeDtypeStruct((bsz, seq, HEAD_DIM), F32)
    return pl.pallas_call(
        _rope_kernel,
        grid=(bsz, seq // ROW_TILE),
        in_specs=[pl.BlockSpec((1, ROW_TILE, 1), lambda b, i: (b, i, 0)),
                  pl.BlockSpec((1, HEAD_DIM), lambda b, i: (0, 0)),
                  pl.BlockSpec((1, HEAD_DIM), lambda b, i: (0, 0))],
        out_specs=[pl.BlockSpec((1, ROW_TILE, HEAD_DIM), lambda b, i: (b, i, 0))] * 2,
        out_shape=[tab, tab],
        compiler_params=_params(2),
        name="rope_tables",
    )(positions.reshape(bsz, seq, 1), inv2, sign)


def _norm_kernel(x_ref, nw_ref, sc_ref, sh_ref, h_ref):
    h_ref[0] = _adaln(x_ref[0], nw_ref[...], sc_ref[0], sh_ref[0]).astype(BF16)


def _first_norm(x, nw, scale, shift):
    bsz, seq, _ = x.shape
    vec = pl.BlockSpec((1, 1, D_MODEL), lambda b, i: (b, 0, 0))
    row = pl.BlockSpec((1, ROW_TILE, D_MODEL), lambda b, i: (b, i, 0))
    return pl.pallas_call(
        _norm_kernel,
        grid=(bsz, seq // ROW_TILE),
        in_specs=[row, pl.BlockSpec((1, D_MODEL), lambda b, i: (0, 0)), vec, vec],
        out_specs=row,
        out_shape=jax.ShapeDtypeStruct(x.shape, BF16),
        compiler_params=_params(2),
        name="first_norm",
    )(x, nw, scale, shift)


def _even_in_kernel(h_ref, w_ref, dww_ref, dwb_ref, lnw_ref, lnb_ref,
                    ya_ref, u_ref, bz_ref, conv_scr, az_scr):
    i = pl.program_id(1)
    h = h_ref[0]
    cw = CONV_WIDTH

    @pl.when(i == 0)
    def _():
        conv_scr[0:CONV_HALO, :] = jnp.zeros((CONV_HALO, cw), F32)

    a1 = _dot(h, w_ref[:, 0:cw])
    a2 = _dot(h, w_ref[:, cw:2 * cw])
    conv_scr[CONV_HALO:CONV_HALO + ROW_TILE, :] = a1 * jax.nn.sigmoid(a2)
    az_scr[...] = jax.nn.silu(_dot(h, w_ref[:, 2 * cw:3 * cw]))
    u_ref[0] = _dot(h, w_ref[:, 3 * cw:4 * cw])
    bz_ref[0] = _dot(h, w_ref[:, 4 * cw:5 * cw])

    first_tap = CONV_HALO - (CONV_KERNEL - 1)

    def conv_block(r, carry):
        r0 = pl.multiple_of(r * CONV_ROWS, CONV_ROWS)
        acc = jnp.broadcast_to(dwb_ref[...], (CONV_ROWS, cw))
        win = conv_scr[pl.ds(r0, CONV_ROWS + CONV_HALO), :]
        for k in range(CONV_KERNEL):
            acc = acc + dww_ref[k:k + 1, :] * win[first_tap + k:first_tap + k + CONV_ROWS, :]
        mu = jnp.mean(acc, axis=-1, keepdims=True)
        xc = acc - mu
        y = xc * lax.rsqrt(jnp.mean(xc * xc, axis=-1, keepdims=True) + EPS)
        y = y * lnw_ref[...] + lnb_ref[...]
        ya_ref[0, pl.ds(r0, CONV_ROWS), :] = (jax.nn.silu(y) * az_scr[pl.ds(r0, CONV_ROWS), :]).astype(BF16)
        return carry

    lax.fori_loop(0, ROW_TILE // CONV_ROWS, conv_block, 0)
    conv_scr[0:CONV_HALO, :] = conv_scr[ROW_TILE:ROW_TILE + CONV_HALO, :]


def _even_in(h, w_in, dw_w, dw_b, ln_w, ln_b):
    bsz, seq, _ = h.shape
    cw = CONV_WIDTH
    const = lambda shape: pl.BlockSpec(shape, lambda b, i: (0,) * len(shape))
    row = lambda width: pl.BlockSpec((1, ROW_TILE, width), lambda b, i: (b, i, 0))
    return pl.pallas_call(
        _even_in_kernel,
        grid=(bsz, seq // ROW_TILE),
        in_specs=[row(D_MODEL), const(w_in.shape), const((CONV_KERNEL, cw)),
                  const((1, cw)), const((1, cw)), const((1, cw))],
        out_specs=[row(cw), row(cw), row(cw)],
        out_shape=[jax.ShapeDtypeStruct((bsz, seq, cw), BF16),
                   jax.ShapeDtypeStruct((bsz, seq, cw), F32),
                   jax.ShapeDtypeStruct((bsz, seq, cw), F32)],
        scratch_shapes=[pltpu.VMEM((CONV_HALO + ROW_TILE, cw), F32),
                        pltpu.VMEM((ROW_TILE, cw), F32)],
        compiler_params=_params(2),
        name="even_in",
    )(h, w_in, dw_w, dw_b.reshape(1, cw), ln_w.reshape(1, cw), ln_b.reshape(1, cw))


def _ssm_matrices(lam_re, lam_im, log_dt, b_re, b_im, c_re, c_im):
    hp = lax.Precision.HIGHEST
    t = SSM_CHUNK
    lr, li = lam_re.astype(F32), lam_im.astype(F32)
    dt = jnp.exp(log_dt.astype(F32))[:, None]

    def a_pow(k):
        kk = k.astype(F32)[:, None, None]
        mag = jnp.exp(kk * (lr * dt)[None])
        ang = kk * (li * dt)[None]
        return mag * jnp.cos(ang), mag * jnp.sin(ang)

    ar, ai = a_pow(jnp.ones((1,), F32))
    ar, ai = ar[0], ai[0]
    den = lr * lr + li * li
    nr = ar - 1.0
    kr = (nr * lr + ai * li) / den
    ki = (ai * lr - nr * li) / den
    br, bi = b_re.astype(F32), b_im.astype(F32)
    bbr = kr[..., None] * br - ki[..., None] * bi
    bbi = kr[..., None] * bi + ki[..., None] * br
    cr, ci = c_re.astype(F32), c_im.astype(F32)

    pr, pi = a_pow(jnp.arange(t + 1))
    wr = pr[:t, :, :, None] * bbr[None] - pi[:t, :, :, None] * bbi[None]
    wi = pr[:t, :, :, None] * bbi[None] + pi[:t, :, :, None] * bbr[None]
    kk = (jnp.einsum('gop,kgpi->kgoi', cr, wr, precision=hp)
          - jnp.einsum('gop,kgpi->kgoi', ci, wi, precision=hp))

    nb, gl, hh, pp = SSM_BLOCKS, SSM_LANE_GROUPS, SSM_GROUP, SSM_STATE
    eye = jnp.eye(gl, dtype=F32)
    lag = jnp.arange(t)[None, :] - jnp.arange(t)[:, None]
    toe = jnp.where((lag >= 0)[:, :, None, None, None], kk[jnp.clip(lag, 0, t - 1)], 0.0)
    toe = toe.reshape(t, t, nb, gl, hh, hh)
    m_intra = jnp.einsum('stcgoi,gh->csgitho', toe, eye).reshape(nb, t * LANES, t * LANES)

    sr = wr[::-1].reshape(t, nb, gl, pp, hh)
    si = wi[::-1].reshape(t, nb, gl, pp, hh)
    ms_r = jnp.einsum('scgpi,gh->csgihp', sr, eye).reshape(nb, t * LANES, gl * pp)
    ms_i = jnp.einsum('scgpi,gh->csgihp', si, eye).reshape(nb, t * LANES, gl * pp)
    m_all = jnp.concatenate([m_intra, ms_r, ms_i], axis=-1).astype(BF16)

    qr = pr[1:, :, None, :] * cr[None] - pi[1:, :, None, :] * ci[None]
    qi = pr[1:, :, None, :] * ci[None] + pi[1:, :, None, :] * cr[None]
    qr = qr.reshape(t, nb, gl, hh, pp)
    qi = qi.reshape(t, nb, gl, hh, pp)
    mi_r = jnp.einsum('tcgop,gh->cgptho', qr, eye).reshape(nb, gl * pp, t * LANES)
    mi_i = jnp.einsum('tcgop,gh->cgptho', -qi, eye).reshape(nb, gl * pp, t * LANES)
    m_in = jnp.concatenate([mi_r, mi_i], axis=1).astype(BF16)

    n_lvl = int(math.log2(2048 // t))
    sr2, si2 = a_pow(t * (2 ** jnp.arange(n_lvl)))
    ap_r = sr2.reshape(n_lvl, nb, gl * pp).transpose(1, 0, 2)
    ap_i = si2.reshape(n_lvl, nb, gl * pp).transpose(1, 0, 2)
    return m_all, m_in, ap_r, ap_i


def _ssm_kernel(u_ref, mall_ref, min_ref, apr_ref, api_ref, d_ref, y_ref,
                x_scr, carry_scr, zr_scr, zi_scr, *, n_chunks, n_levels):
    t = SSM_CHUNK
    sc = SSM_STATE_COLS
    tl = t * LANES
    for bb in range(SSM_BATCH):
        for t0 in range(t):
            x_scr[bb * n_chunks:(bb + 1) * n_chunks, t0 * LANES:(t0 + 1) * LANES] = (
                u_ref[bb, pl.ds(t0, n_chunks, stride=t), :].astype(BF16))
    x = x_scr[...]
    s_loc = _dot(x, mall_ref[0, :, tl:])

    zero = jnp.zeros((n_chunks, sc), F32)
    for p in range(2):
        zr_scr[p, 0:n_chunks, :] = zero
        zi_scr[p, 0:n_chunks, :] = zero
    for bb in range(SSM_BATCH):
        rows = slice(bb * n_chunks, (bb + 1) * n_chunks)
        zr_scr[0, n_chunks:, :] = s_loc[rows, :sc]
        zi_scr[0, n_chunks:, :] = s_loc[rows, sc:]
        for k in range(n_levels):
            src, dst = k % 2, 1 - (k % 2)
            sh = n_chunks - (1 << k)
            zr = zr_scr[src, n_chunks:, :]
            zi = zi_scr[src, n_chunks:, :]
            pr = zr_scr[src, sh:sh + n_chunks, :]
            pi = zi_scr[src, sh:sh + n_chunks, :]
            ar = apr_ref[0, k:k + 1, :]
            ai = api_ref[0, k:k + 1, :]
            zr_scr[dst, n_chunks:, :] = zr + ar * pr - ai * pi
            zi_scr[dst, n_chunks:, :] = zi + ar * pi + ai * pr
        fin = n_levels % 2
        carry_scr[rows, :sc] = zr_scr[fin, n_chunks - 1:2 * n_chunks - 1, :].astype(BF16)
        carry_scr[rows, sc:] = zi_scr[fin, n_chunks - 1:2 * n_chunks - 1, :].astype(BF16)

    carry = carry_scr[...]
    for tp in range(t // 2):
        cols = slice(tp * 2 * LANES, (tp + 1) * 2 * LANES)
        yc = _dot(x, mall_ref[0, :, cols]) + _dot(carry, min_ref[0, :, cols])
        for bb in range(SSM_BATCH):
            for j in range(2):
                y_ref[bb, pl.ds(2 * tp + j, n_chunks, stride=t), :] = (
                    yc[bb * n_chunks:(bb + 1) * n_chunks, j * LANES:(j + 1) * LANES])
    for bb in range(SSM_BATCH):
        y_ref[bb] = y_ref[bb] + d_ref[0] * u_ref[bb]


def _ssm(u, mats, d_skip):
    m_all, m_in, ap_r, ap_i = mats
    bsz, seq, _ = u.shape
    n_chunks = seq // SSM_CHUNK
    n_levels = ap_r.shape[1]
    nb = SSM_BATCH
    once = pl.Buffered(1)
    blk = pl.BlockSpec((nb, seq, LANES), lambda c, b: (b, 0, c))
    return pl.pallas_call(
        functools.partial(_ssm_kernel, n_chunks=n_chunks, n_levels=n_levels),
        grid=(SSM_BLOCKS, bsz // nb),
        in_specs=[blk,
                  pl.BlockSpec((1,) + m_all.shape[1:], lambda c, b: (c, 0, 0), pipeline_mode=once),
                  pl.BlockSpec((1,) + m_in.shape[1:], lambda c, b: (c, 0, 0), pipeline_mode=once),
                  pl.BlockSpec((1, n_levels, SSM_STATE_COLS), lambda c, b: (c, 0, 0)),
                  pl.BlockSpec((1, n_levels, SSM_STATE_COLS), lambda c, b: (c, 0, 0)),
                  pl.BlockSpec((1, 1, LANES), lambda c, b: (c, 0, 0))],
        out_specs=blk,
        out_shape=jax.ShapeDtypeStruct(u.shape, F32),
        scratch_shapes=[pltpu.VMEM((nb * n_chunks, SSM_CHUNK * LANES), BF16),
                        pltpu.VMEM((nb * n_chunks, 2 * SSM_STATE_COLS), BF16),
                        pltpu.VMEM((2, 2 * n_chunks, SSM_STATE_COLS), F32),
                        pltpu.VMEM((2, 2 * n_chunks, SSM_STATE_COLS), F32)],
        compiler_params=_params(2),
        name="ssm",
    )(u, m_all, m_in, ap_r, ap_i, d_skip.reshape(SSM_BLOCKS, 1, LANES))


def _finish(x_ref, gate_ref, out, nxt, xo_ref, ho_ref):
    xn = x_ref[0] + gate_ref[0] * out
    xo_ref[0] = xn
    if nxt is not None:
        nw_ref, sc_ref, sh_ref = nxt
        ho_ref[0] = _adaln(xn, nw_ref[...], sc_ref[0], sh_ref[0]).astype(BF16)


def _even_out_kernel(*refs, emit_h):
    if emit_h:
        (x_ref, ya_ref, ys_ref, bz_ref, gw_ref, gb_ref, wa_ref, ws_ref, gate_ref,
         nw_ref, sc_ref, sh_ref, xo_ref, ho_ref) = refs
        nxt = (nw_ref, sc_ref, sh_ref)
    else:
        (x_ref, ya_ref, ys_ref, bz_ref, gw_ref, gb_ref, wa_ref, ws_ref, gate_ref, xo_ref) = refs
        nxt, ho_ref = None, None
    y = ys_ref[0]
    y = 0.5 * y * (1.0 + lax.erf(y * (2.0 ** -0.5)))
    y = y * jax.nn.sigmoid(_dot(y.astype(BF16), gw_ref[...]) + gb_ref[...])
    y = y * jax.nn.silu(bz_ref[0])
    out = _dot(ya_ref[0], wa_ref[...]) + _dot(y.astype(BF16), ws_ref[...])
    _finish(x_ref, gate_ref, out, nxt, xo_ref, ho_ref)


def _odd_out_kernel(*refs, emit_h):
    if emit_h:
        (x_ref, h_ref, o_ref, wz_ref, wo_ref, gate_ref, nw_ref, sc_ref, sh_ref, xo_ref, ho_ref) = refs
        nxt = (nw_ref, sc_ref, sh_ref)
    else:
        (x_ref, h_ref, o_ref, wz_ref, wo_ref, gate_ref, xo_ref) = refs
        nxt, ho_ref = None, None
    z = _dot(h_ref[0], wz_ref[...])
    g = o_ref[0] * jax.nn.silu(z)
    out = _dot(g.astype(BF16), wo_ref[...])
    _finish(x_ref, gate_ref, out, nxt, xo_ref, ho_ref)


def _tail_call(body, name, x, rows, consts, gate, nxt):
    bsz, seq, _ = x.shape
    emit_h = nxt is not None
    row = lambda a: pl.BlockSpec((1, ROW_TILE, a.shape[-1]), lambda b, i: (b, i, 0))
    const = lambda a: pl.BlockSpec(a.shape, lambda b, i: (0,) * a.ndim)
    vec = pl.BlockSpec((1, 1, D_MODEL), lambda b, i: (b, 0, 0))
    args = [x, *rows, *consts, gate]
    in_specs = [row(x)] + [row(a) for a in rows] + [const(a) for a in consts] + [vec]
    out_specs = [row(x)]
    out_shape = [jax.ShapeDtypeStruct(x.shape, F32)]
    if emit_h:
        nw, sc, sh = nxt
        args += [nw, sc, sh]
        in_specs += [const(nw), vec, vec]
        out_specs.append(row(x))
        out_shape.append(jax.ShapeDtypeStruct(x.shape, BF16))
    res = pl.pallas_call(
        functools.partial(body, emit_h=emit_h),
        grid=(bsz, seq // ROW_TILE),
        in_specs=in_specs, out_specs=out_specs, out_shape=out_shape,
        compiler_params=_params(2),
        name=name,
    )(*args)
    return (res[0], res[1]) if emit_h else (res[0], None)


def _attn_kernel(h_ref, w_ref, cos_ref, sin_ref, qw_ref, kw_ref, o_ref,
                 q_scr, k_scr, v_scr, on_scr, m_scr, d_scr, bias_scr, *, seq):
    wk = WINDOW_KEYS
    n_chunks = seq // PROJ_ROWS
    chunk_blocks = PROJ_ROWS // wk
    dil1 = ATTN_PATTERNS[1][1]
    dil2 = ATTN_PATTERNS[2][1]
    assert chunk_blocks == 2 and wk * dil1 == 2 * PROJ_ROWS and seq == wk * dil2

    qi = lax.broadcasted_iota(jnp.int32, (wk, 2 * wk), 0)
    kj = lax.broadcasted_iota(jnp.int32, (wk, 2 * wk), 1)
    band = (kj >= qi) & (kj <= qi + wk)
    bias_scr[0] = jnp.where(band, 0.0, NEG_INF)
    bias_scr[1] = jnp.where(band & (kj >= wk), 0.0, NEG_INF)

    def proj_chunk(c):
        start = c * PROJ_ROWS
        if not isinstance(c, int):
            start = pl.multiple_of(start, PROJ_ROWS)
        rows = pl.ds(start, PROJ_ROWS)
        pr = _dot(h_ref[0, rows, :], w_ref[0])
        cos = cos_ref[0, rows, :]
        sin = sin_ref[0, rows, :]

        def norm_rope(t, w):
            t = t * lax.rsqrt(jnp.mean(t * t, axis=-1, keepdims=True) + EPS) * w
            return t * cos + pltpu.roll(t, HEAD_DIM // 2, 1) * sin

        for g in range(N_PATTERNS):
            off = g * 3 * HEAD_DIM
            q_scr[g, rows, :] = norm_rope(pr[:, off:off + HEAD_DIM], qw_ref[...]) * ATTN_SCALE
            k_scr[g, rows, :] = norm_rope(pr[:, off + HEAD_DIM:off + 2 * HEAD_DIM], kw_ref[...])
            v_scr[g, rows, :] = pr[:, off + 2 * HEAD_DIM:off + 3 * HEAD_DIM]

    def attn_block(g, r, n, first):
        dil = ATTN_PATTERNS[g][1]

        def rows_of(blk):
            if dil == 1:
                start = blk * wk
                if not isinstance(start, int):
                    start = pl.multiple_of(start, wk)
                return pl.ds(start, wk)
            return pl.ds(blk * (wk * dil) + r, wk, stride=dil)

        cur = rows_of(n)
        q = q_scr[g, cur, :].astype(BF16)
        kc = k_scr[g, cur, :].astype(BF16)
        vc = v_scr[g, cur, :].astype(BF16)
        if first is True:
            kk, vv = kc, vc
            bias = bias_scr[1, :, wk:]
        else:
            if first is False:
                prev = rows_of(n - 1)
                bias = bias_scr[0]
            else:
                prev = rows_of(jnp.maximum(n - 1, 0))
                bias = bias_scr[jnp.where(n == 0, 1, 0)]
            kk = jnp.concatenate([k_scr[g, prev, :].astype(BF16), kc], axis=0)
            vv = jnp.concatenate([v_scr[g, prev, :].astype(BF16), vc], axis=0)
        s = lax.dot_general(q, kk, (((1,), (1,)), ((), ())), preferred_element_type=F32) + bias
        m = jnp.max(s, axis=-1, keepdims=True)
        p = jnp.exp(s - m)
        den = jnp.sum(p, axis=-1, keepdims=True)
        o = _dot(p.astype(BF16), vv)
        on_scr[g, cur, :] = o
        m_scr[g, cur, :] = jnp.broadcast_to(m, (wk, HEAD_DIM))
        d_scr[g, cur, :] = jnp.broadcast_to(den, (wk, HEAD_DIM))

    proj_chunk(0)
    attn_block(0, 0, 0, True)
    attn_block(0, 0, 1, False)
    proj_chunk(1)

    def body(c, carry):
        j = chunk_blocks * (c - 1)
        attn_block(0, 0, j, False)
        attn_block(0, 0, j + 1, False)
        n1 = (c - 2) // 2
        r1 = 2 * (c % 2)
        attn_block(1, r1, n1, None)
        attn_block(1, r1 + 1, n1, None)
        proj_chunk(c)
        return carry

    lax.fori_loop(2, n_chunks, body, 0)

    last = seq // wk - chunk_blocks
    attn_block(0, 0, last, False)
    attn_block(0, 0, last + 1, False)
    for r in range(dil1):
        attn_block(1, r, seq // dil1 // wk - 1, False)

    def tail_body(i, carry):
        for u in range(4):
            attn_block(2, i * 4 + u, 0, True)
        return carry

    lax.fori_loop(0, dil2 // 4, tail_body, 0)

    def merge_body(c, carry):
        rows = pl.ds(pl.multiple_of(c * PROJ_ROWS, PROJ_ROWS), PROJ_ROWS)
        ms = [m_scr[g, rows, :] for g in range(N_PATTERNS)]
        mx = jnp.maximum(jnp.maximum(ms[0], ms[1]), ms[2])
        num = jnp.zeros((PROJ_ROWS, HEAD_DIM), F32)
        den = jnp.zeros((PROJ_ROWS, HEAD_DIM), F32)
        for g in range(N_PATTERNS):
            wgt = jnp.exp(ms[g] - mx)
            num = num + wgt * on_scr[g, rows, :]
            den = den + wgt * d_scr[g, rows, :]
        o_ref[0, rows, :] = num / den
        return carry

    lax.fori_loop(0, seq // PROJ_ROWS, merge_body, 0)


def _attention(h, w_qkv, cos2, sin2, q_norm_w, k_norm_w):
    bsz, seq, _ = h.shape
    stat = pltpu.VMEM((N_PATTERNS, seq, HEAD_DIM), F32)
    return pl.pallas_call(
        functools.partial(_attn_kernel, seq=seq),
        grid=(bsz, HEADS),
        in_specs=[pl.BlockSpec((1, seq, D_MODEL), lambda b, hd: (b, 0, 0)),
                  pl.BlockSpec((1, D_MODEL, 3 * N_PATTERNS * HEAD_DIM), lambda b, hd: (hd, 0, 0)),
                  pl.BlockSpec((1, seq, HEAD_DIM), lambda b, hd: (b, 0, 0)),
                  pl.BlockSpec((1, seq, HEAD_DIM), lambda b, hd: (b, 0, 0)),
                  pl.BlockSpec((1, HEAD_DIM), lambda b, hd: (0, 0)),
                  pl.BlockSpec((1, HEAD_DIM), lambda b, hd: (0, 0))],
        out_specs=pl.BlockSpec((1, seq, HEAD_DIM), lambda b, hd: (b, 0, hd)),
        out_shape=jax.ShapeDtypeStruct((bsz, seq, HEADS * HEAD_DIM), F32),
        scratch_shapes=[stat, stat, stat, stat, stat, stat,
                        pltpu.VMEM((2, WINDOW_KEYS, 2 * WINDOW_KEYS), F32)],
        compiler_params=_params(2),
        name="attention",
    )(h, w_qkv, cos2, sin2, q_norm_w.reshape(1, HEAD_DIM), k_norm_w.reshape(1, HEAD_DIM))


def _head_major_qkv(w_in):
    w = w_in[:, :3 * ATTN_QKV].reshape(D_MODEL, 3, N_PATTERNS, HEADS, HEAD_DIM)
    return w.transpose(3, 0, 2, 1, 4).reshape(HEADS, D_MODEL, N_PATTERNS * 3 * HEAD_DIM).astype(BF16)


def kernel(x, c, positions, mod_w, mod_b, norm_w, even_w_in, conv_dw_w, conv_dw_b, conv_ln_w, conv_ln_b, ssm_lam_re, ssm_lam_im, ssm_log_dt, ssm_b_re, ssm_b_im, ssm_c_re, ssm_c_im, ssm_d, ssm_glu_w, ssm_glu_b, even_w_out, attn_w_in, attn_q_norm_w, attn_k_norm_w, attn_w_out):
    bsz, seq, _ = x.shape
    assert seq == 2048 and x.shape[-1] == D_MODEL and bsz % SSM_BATCH == 0
    mod = _modulation(c, mod_w, mod_b)
    shift = [mod[l, :, None, :D_MODEL] for l in range(DEPTH)]
    scale = [mod[l, :, None, D_MODEL:2 * D_MODEL] for l in range(DEPTH)]
    gate = [mod[l, :, None, 2 * D_MODEL:] for l in range(DEPTH)]
    nw = [norm_w[l].reshape(1, D_MODEL) for l in range(DEPTH)]
    cos2, sin2 = _rope_tables(positions)

    h = _first_norm(x, nw[0], scale[0], shift[0])
    for layer in range(DEPTH):
        i = layer // 2
        nxt = (nw[layer + 1], scale[layer + 1], shift[layer + 1]) if layer + 1 < DEPTH else None
        if layer % 2 == 0:
            ya, u, bz = _even_in(h, even_w_in[i].astype(BF16), conv_dw_w[i], conv_dw_b[i],
                                 conv_ln_w[i], conv_ln_b[i])
            mats = _ssm_matrices(ssm_lam_re[i], ssm_lam_im[i], ssm_log_dt[i], ssm_b_re[i], ssm_b_im[i],
                                 ssm_c_re[i], ssm_c_im[i])
            ys = _ssm(u, mats, ssm_d[i])
            w_out = even_w_out[i].astype(BF16)
            x, h = _tail_call(_even_out_kernel, "even_out", x, [ya, ys, bz],
                              [ssm_glu_w[i].astype(BF16), ssm_glu_b[i].reshape(1, SSM_WIDTH),
                               w_out[:CONV_WIDTH], w_out[CONV_WIDTH:]], gate[layer], nxt)
        else:
            o = _attention(h, _head_major_qkv(attn_w_in[i]), cos2, sin2, attn_q_norm_w[i], attn_k_norm_w[i])
            x, h = _tail_call(_odd_out_kernel, "odd_out", x, [h, o],
                              [attn_w_in[i][:, 3 * ATTN_QKV:].astype(BF16), attn_w_out[i].astype(BF16)],
                              gate[layer], nxt)
    return x
```

```python
import functools
import math

import jax
import jax.numpy as jnp
from jax import lax
from jax.experimental import pallas as pl
from jax.experimental.pallas import tpu as pltpu

F32 = jnp.float32
BF16 = jnp.bfloat16

D_MODEL = 1024
DEPTH = 4
CONV_WIDTH = 512
CONV_KERNEL = 31
SSM_WIDTH = 512
SSM_GROUP = 16
SSM_GROUPS = 32
SSM_STATE = 64
HEAD_DIM = 128
ATTN_PATTERNS = ((128, 1), (512, 4), (2048, 16))
N_PATTERNS = 3
HEADS = 8
ATTN_QKV = N_PATTERNS * HEADS * HEAD_DIM
ATTN_SCALE = HEAD_DIM ** -0.5
ROPE_THETA = 10000.0
EPS = 1e-6
NEG_INF = -1e30

LANES = 128
WINDOW_KEYS = 128
SSM_CHUNK = 16
SSM_LANE_GROUPS = LANES // SSM_GROUP
SSM_BLOCKS = SSM_WIDTH // LANES
SSM_STATE_COLS = SSM_LANE_GROUPS * SSM_STATE
SSM_BATCH = 4
ROW_TILE = 512
CONV_HALO = 32
CONV_ROWS = 64
PROJ_ROWS = 256
VMEM_LIMIT = 56 * 1024 * 1024


def _params(n_axes, vmem=VMEM_LIMIT):
    return pltpu.CompilerParams(dimension_semantics=("arbitrary",) * n_axes, vmem_limit_bytes=vmem)


def _adaln(x, nw, scale, shift):
    ms = jnp.mean(x * x, axis=-1, keepdims=True)
    return (x * lax.rsqrt(ms + EPS) * nw) * (1.0 + scale) + shift


def _dot(a, b):
    return jnp.dot(a, b, preferred_element_type=F32)


def _mod_kernel(c_ref, w_ref, b_ref, o_ref):
    o_ref[0] = jnp.dot(c_ref[...], w_ref[0], preferred_element_type=F32,
                       precision=lax.Precision.HIGHEST) + b_ref[0]


def _modulation(c, mod_w, mod_b):
    bsz = c.shape[0]
    nblk = 3 * D_MODEL // D_MODEL
    return pl.pallas_call(
        _mod_kernel,
        grid=(DEPTH, nblk),
        in_specs=[pl.BlockSpec((bsz, D_MODEL), lambda l, j: (0, 0)),
                  pl.BlockSpec((1, D_MODEL, D_MODEL), lambda l, j: (l, 0, j)),
                  pl.BlockSpec((1, 1, D_MODEL), lambda l, j: (l, 0, j))],
        out_specs=pl.BlockSpec((1, bsz, D_MODEL), lambda l, j: (l, 0, j)),
        out_shape=jax.ShapeDtypeStruct((DEPTH, bsz, 3 * D_MODEL), F32),
        compiler_params=_params(2),
        name="modulation",
    )(c, mod_w, mod_b.reshape(DEPTH, 1, 3 * D_MODEL))


def _rope_kernel(pos_ref, inv_ref, sign_ref, cos_ref, sin_ref):
    ang = pos_ref[0].astype(F32) * inv_ref[...]
    cos_ref[0] = jnp.cos(ang)
    sin_ref[0] = jnp.sin(ang) * sign_ref[...]


def _rope_tables(positions):
    bsz, seq = positions.shape
    inv = ROPE_THETA ** (-jnp.arange(0, HEAD_DIM, 2, dtype=F32) / HEAD_DIM)
    inv2 = jnp.concatenate([inv, inv]).reshape(1, HEAD_DIM)
    sign = jnp.concatenate([-jnp.ones((HEAD_DIM // 2,), F32), jnp.ones((HEAD_DIM // 2,), F32)]).reshape(1, HEAD_DIM)
    tab = jax.ShapeDtypeStruct((bsz, seq, HEAD_DIM), F32)
    return pl.pallas_call(
        _rope_kernel,
        grid=(bsz, seq // ROW_TILE),
        in_specs=[pl.BlockSpec((1, ROW_TILE, 1), lambda b, i: (b, i, 0)),
                  pl.BlockSpec((1, HEAD_DIM), lambda b, i: (0, 0)),
                  pl.BlockSpec((1, HEAD_DIM), lambda b, i: (0, 0))],
        out_specs=[pl.BlockSpec((1, ROW_TILE, HEAD_DIM), lambda b, i: (b, i, 0))] * 2,
        out_shape=[tab, tab],
        compiler_params=_params(2),
        name="rope_tables",
    )(positions.reshape(bsz, seq, 1), inv2, sign)


def _norm_kernel(x_ref, nw_ref, sc_ref, sh_ref, h_ref):
    h_ref[0] = _adaln(x_ref[0], nw_ref[...], sc_ref[0], sh_ref[0]).astype(BF16)


def _first_norm(x, nw, scale, shift):
    bsz, seq, _ = x.shape
    vec = pl.BlockSpec((1, 1, D_MODEL), lambda b, i: (b, 0, 0))
    row = pl.BlockSpec((1, ROW_TILE, D_MODEL), lambda b, i: (b, i, 0))
    return pl.pallas_call(
        _norm_kernel,
        grid=(bsz, seq // ROW_TILE),
        in_specs=[row, pl.BlockSpec((1, D_MODEL), lambda b, i: (0, 0)), vec, vec],
        out_specs=row,
        out_shape=jax.ShapeDtypeStruct(x.shape, BF16),
        compiler_params=_params(2),
        name="first_norm",
    )(x, nw, scale, shift)


def _even_in_kernel(h_ref, w_ref, dww_ref, dwb_ref, lnw_ref, lnb_ref,
                    ya_ref, u_ref, bz_ref, conv_scr, y_scr, az_scr):
    i = pl.program_id(1)
    h = h_ref[0]
    cw = CONV_WIDTH
    slabs = cw // LANES

    @pl.when(i == 0)
    def _():
        conv_scr[:, 0:CONV_HALO, :] = jnp.zeros((slabs, CONV_HALO, LANES), F32)

    a1 = _dot(h, w_ref[:, 0:cw])
    a2 = _dot(h, w_ref[:, cw:2 * cw])
    a = a1 * jax.nn.sigmoid(a2)
    for l in range(slabs):
        conv_scr[l, CONV_HALO:CONV_HALO + ROW_TILE, :] = a[:, l * LANES:(l + 1) * LANES]
    az_scr[...] = jax.nn.silu(_dot(h, w_ref[:, 2 * cw:3 * cw]))
    u_ref[0] = _dot(h, w_ref[:, 3 * cw:4 * cw])
    bz_ref[0] = _dot(h, w_ref[:, 4 * cw:5 * cw])

    first_tap = CONV_HALO - (CONV_KERNEL - 1)
    half = CONV_ROWS // 2

    def conv_block(r, carry):
        r0 = r * CONV_ROWS
        for l in range(slabs):
            lanes = slice(l * LANES, (l + 1) * LANES)
            for par in range(2):
                acc = jnp.broadcast_to(dwb_ref[:, lanes], (half, LANES))
                for k in range(CONV_KERNEL):
                    tap = conv_scr[l, pl.ds(r0 + first_tap + par + k, half, stride=2), :]
                    acc = acc + dww_ref[k:k + 1, lanes] * tap
                y_scr[l, pl.ds(r0 + par, half, stride=2), :] = acc
        return carry

    lax.fori_loop(0, ROW_TILE // CONV_ROWS, conv_block, 0)
    for l in range(slabs):
        conv_scr[l, 0:CONV_HALO, :] = conv_scr[l, ROW_TILE:ROW_TILE + CONV_HALO, :]

    def norm_block(r, carry):
        rows = pl.ds(pl.multiple_of(r * CONV_ROWS, CONV_ROWS), CONV_ROWS)
        acc = jnp.concatenate([y_scr[l, rows, :] for l in range(slabs)], axis=-1)
        mu = jnp.mean(acc, axis=-1, keepdims=True)
        xc = acc - mu
        y = xc * lax.rsqrt(jnp.mean(xc * xc, axis=-1, keepdims=True) + EPS)
        y = y * lnw_ref[...] + lnb_ref[...]
        ya_ref[0, rows, :] = (jax.nn.silu(y) * az_scr[rows, :]).astype(BF16)
        return carry

    lax.fori_loop(0, ROW_TILE // CONV_ROWS, norm_block, 0, unroll=4)


def _even_in(h, w_in, dw_w, dw_b, ln_w, ln_b):
    bsz, seq, _ = h.shape
    cw = CONV_WIDTH
    const = lambda shape: pl.BlockSpec(shape, lambda b, i: (0,) * len(shape))
    row = lambda width: pl.BlockSpec((1, ROW_TILE, width), lambda b, i: (b, i, 0))
    return pl.pallas_call(
        _even_in_kernel,
        grid=(bsz, seq // ROW_TILE),
        in_specs=[row(D_MODEL), const(w_in.shape), const((CONV_KERNEL, cw)),
                  const((1, cw)), const((1, cw)), const((1, cw))],
        out_specs=[row(cw), row(cw), row(cw)],
        out_shape=[jax.ShapeDtypeStruct((bsz, seq, cw), BF16),
                   jax.ShapeDtypeStruct((bsz, seq, cw), F32),
                   jax.ShapeDtypeStruct((bsz, seq, cw), F32)],
        scratch_shapes=[pltpu.VMEM((cw // LANES, CONV_HALO + ROW_TILE, LANES), F32),
                        pltpu.VMEM((cw // LANES, ROW_TILE, LANES), F32),
                        pltpu.VMEM((ROW_TILE, cw), F32)],
        compiler_params=_params(2),
        name="even_in",
    )(h, w_in, dw_w, dw_b.reshape(1, cw), ln_w.reshape(1, cw), ln_b.reshape(1, cw))


def _ssm_matrices(lam_re, lam_im, log_dt, b_re, b_im, c_re, c_im):
    hp = lax.Precision.HIGHEST
    t = SSM_CHUNK
    lr, li = lam_re.astype(F32), lam_im.astype(F32)
    dt = jnp.exp(log_dt.astype(F32))[:, None]

    def a_pow(k):
        kk = k.astype(F32)[:, None, None]
        mag = jnp.exp(kk * (lr * dt)[None])
        ang = kk * (li * dt)[None]
        return mag * jnp.cos(ang), mag * jnp.sin(ang)

    ar, ai = a_pow(jnp.ones((1,), F32))
    ar, ai = ar[0], ai[0]
    den = lr * lr + li * li
    nr = ar - 1.0
    kr = (nr * lr + ai * li) / den
    ki = (ai * lr - nr * li) / den
    br, bi = b_re.astype(F32), b_im.astype(F32)
    bbr = kr[..., None] * br - ki[..., None] * bi
    bbi = kr[..., None] * bi + ki[..., None] * br
    cr, ci = c_re.astype(F32), c_im.astype(F32)

    pr, pi = a_pow(jnp.arange(t + 1))
    wr = pr[:t, :, :, None] * bbr[None] - pi[:t, :, :, None] * bbi[None]
    wi = pr[:t, :, :, None] * bbi[None] + pi[:t, :, :, None] * bbr[None]
    kk = (jnp.einsum('gop,kgpi->kgoi', cr, wr, precision=hp)
          - jnp.einsum('gop,kgpi->kgoi', ci, wi, precision=hp))

    nb, gl, hh, pp = SSM_BLOCKS, SSM_LANE_GROUPS, SSM_GROUP, SSM_STATE
    grp = jnp.arange(gl)[:, None, None]
    row = jnp.arange(t * hh)[None, :, None]
    col = jnp.arange(t * LANES)[None, None, :]
    place = ((row // hh == col // LANES) & (row % hh == col % hh) & ((col % LANES) // hh == grp)).astype(BF16)
    prow = jnp.arange(pp)[None, :, None]
    pcol = jnp.arange(gl * pp)[None, None, :]
    spread = ((pcol % pp == prow) & (pcol // pp == grp)).astype(BF16)

    def rows_sgi(m):
        n = m.shape[-1]
        return m.reshape(nb, gl, t, hh, n).transpose(0, 2, 1, 3, 4).reshape(nb, t * LANES, n)

    lag = jnp.arange(t)[None, :] - jnp.arange(t)[:, None]
    toe = jnp.where((lag >= 0)[:, :, None, None, None], kk[jnp.clip(lag, 0, t - 1)], 0.0)
    toe = toe.reshape(t, t, nb, gl, hh, hh).transpose(2, 3, 0, 5, 1, 4).reshape(nb, gl, t * hh, t * hh)
    m_intra = rows_sgi(jnp.einsum('cgab,gbn->cgan', toe.astype(BF16), place, preferred_element_type=BF16))

    def state_cols(w):
        w = w[::-1].reshape(t, nb, gl, pp, hh).transpose(1, 2, 0, 4, 3).reshape(nb, gl, t * hh, pp)
        return rows_sgi(jnp.einsum('cgap,gpn->cgan', w.astype(BF16), spread, preferred_element_type=BF16))

    m_state = jnp.concatenate([state_cols(wr), state_cols(wi)], axis=-1)

    def in_rows(q):
        q = q.reshape(t, nb, gl, hh, pp).transpose(1, 2, 4, 0, 3).reshape(nb, gl, pp, t * hh)
        return jnp.einsum('cgpb,gbn->cgpn', q.astype(BF16), place,
                          preferred_element_type=BF16).reshape(nb, gl * pp, t * LANES)

    m_in_r = in_rows(pr[1:, :, None, :] * cr[None] - pi[1:, :, None, :] * ci[None])
    m_in_i = in_rows(-(pr[1:, :, None, :] * ci[None] + pi[1:, :, None, :] * cr[None]))

    n_lvl = int(math.log2(2048 // t))
    sr2, si2 = a_pow(t * (2 ** jnp.arange(n_lvl)))
    ap_r = sr2.reshape(n_lvl, nb, gl * pp).transpose(1, 0, 2)
    ap_i = si2.reshape(n_lvl, nb, gl * pp).transpose(1, 0, 2)
    return m_intra, m_state, m_in_r, m_in_i, ap_r, ap_i


def _ssm_kernel(u_ref, mintra_ref, mstate_ref, minr_ref, mini_ref, apr_ref, api_ref, d_ref, y_ref,
                x_scr, cr_scr, ci_scr, zr_scr, zi_scr, *, n_chunks, n_levels):
    t = SSM_CHUNK
    sc = SSM_STATE_COLS
    for bb in range(SSM_BATCH):
        for t0 in range(t):
            x_scr[bb * n_chunks:(bb + 1) * n_chunks, t0 * LANES:(t0 + 1) * LANES] = (
                u_ref[bb, pl.ds(t0, n_chunks, stride=t), :].astype(BF16))
    x = x_scr[...]
    s_loc = _dot(x, mstate_ref[0])

    zero = jnp.zeros((n_chunks, sc), F32)
    for p in range(2):
        zr_scr[p, 0:n_chunks, :] = zero
        zi_scr[p, 0:n_chunks, :] = zero
    for bb in range(SSM_BATCH):
        rows = slice(bb * n_chunks, (bb + 1) * n_chunks)
        zr_scr[0, n_chunks:, :] = s_loc[rows, :sc]
        zi_scr[0, n_chunks:, :] = s_loc[rows, sc:]
        for k in range(n_levels):
            src, dst = k % 2, 1 - (k % 2)
            sh = n_chunks - (1 << k)
            zr = zr_scr[src, n_chunks:, :]
            zi = zi_scr[src, n_chunks:, :]
            pr = zr_scr[src, sh:sh + n_chunks, :]
            pi = zi_scr[src, sh:sh + n_chunks, :]
            ar = apr_ref[0, k:k + 1, :]
            ai = api_ref[0, k:k + 1, :]
            zr_scr[dst, n_chunks:, :] = zr + ar * pr - ai * pi
            zi_scr[dst, n_chunks:, :] = zi + ar * pi + ai * pr
        fin = n_levels % 2
        cr_scr[rows, :] = zr_scr[fin, n_chunks - 1:2 * n_chunks - 1, :].astype(BF16)
        ci_scr[rows, :] = zi_scr[fin, n_chunks - 1:2 * n_chunks - 1, :].astype(BF16)

    carry_r = cr_scr[...]
    carry_i = ci_scr[...]
    for tp in range(t // 2):
        cols = slice(tp * 2 * LANES, (tp + 1) * 2 * LANES)
        yc = (_dot(x, mintra_ref[0, :, cols]) + _dot(carry_r, minr_ref[0, :, cols])
              + _dot(carry_i, mini_ref[0, :, cols]))
        for bb in range(SSM_BATCH):
            for j in range(2):
                y_ref[bb, pl.ds(2 * tp + j, n_chunks, stride=t), :] = (
                    yc[bb * n_chunks:(bb + 1) * n_chunks, j * LANES:(j + 1) * LANES])
    for bb in range(SSM_BATCH):
        y_ref[bb] = y_ref[bb] + d_ref[0] * u_ref[bb]


def _ssm(u, mats, d_skip):
    m_intra, m_state, m_in_r, m_in_i, ap_r, ap_i = mats
    bsz, seq, _ = u.shape
    n_chunks = seq // SSM_CHUNK
    n_levels = ap_r.shape[1]
    nb = SSM_BATCH
    once = pl.Buffered(1)
    blk = pl.BlockSpec((nb, seq, LANES), lambda c, b: (b, 0, c))
    mat = lambda m: pl.BlockSpec((1,) + m.shape[1:], lambda c, b: (c, 0, 0), pipeline_mode=once)
    lvl = pl.BlockSpec((1, n_levels, SSM_STATE_COLS), lambda c, b: (c, 0, 0))
    return pl.pallas_call(
        functools.partial(_ssm_kernel, n_chunks=n_chunks, n_levels=n_levels),
        grid=(SSM_BLOCKS, bsz // nb),
        in_specs=[blk, mat(m_intra), mat(m_state), mat(m_in_r), mat(m_in_i), lvl, lvl,
                  pl.BlockSpec((1, 1, LANES), lambda c, b: (c, 0, 0))],
        out_specs=blk,
        out_shape=jax.ShapeDtypeStruct(u.shape, F32),
        scratch_shapes=[pltpu.VMEM((nb * n_chunks, SSM_CHUNK * LANES), BF16),
                        pltpu.VMEM((nb * n_chunks, SSM_STATE_COLS), BF16),
                        pltpu.VMEM((nb * n_chunks, SSM_STATE_COLS), BF16),
                        pltpu.VMEM((2, 2 * n_chunks, SSM_STATE_COLS), F32),
                        pltpu.VMEM((2, 2 * n_chunks, SSM_STATE_COLS), F32)],
        compiler_params=_params(2),
        name="ssm",
    )(u, m_intra, m_state, m_in_r, m_in_i, ap_r, ap_i, d_skip.reshape(SSM_BLOCKS, 1, LANES))


def _finish(x_ref, gate_ref, out, nxt, xo_ref, ho_ref):
    xn = x_ref[0] + gate_ref[0] * out
    xo_ref[0] = xn
    if nxt is not None:
        nw_ref, sc_ref, sh_ref = nxt
        ho_ref[0] = _adaln(xn, nw_ref[...], sc_ref[0], sh_ref[0]).astype(BF16)


def _even_out_kernel(*refs, emit_h):
    if emit_h:
        (x_ref, ya_ref, ys_ref, bz_ref, gw_ref, gb_ref, wa_ref, ws_ref, gate_ref,
         nw_ref, sc_ref, sh_ref, xo_ref, ho_ref) = refs
        nxt = (nw_ref, sc_ref, sh_ref)
    else:
        (x_ref, ya_ref, ys_ref, bz_ref, gw_ref, gb_ref, wa_ref, ws_ref, gate_ref, xo_ref) = refs
        nxt, ho_ref = None, None
    y = ys_ref[0]
    y = 0.5 * y * (1.0 + lax.erf(y * (2.0 ** -0.5)))
    y = y * jax.nn.sigmoid(_dot(y.astype(BF16), gw_ref[...]) + gb_ref[...])
    y = y * jax.nn.silu(bz_ref[0])
    out = _dot(ya_ref[0], wa_ref[...]) + _dot(y.astype(BF16), ws_ref[...])
    _finish(x_ref, gate_ref, out, nxt, xo_ref, ho_ref)


def _odd_out_kernel(*refs, emit_h):
    if emit_h:
        (x_ref, h_ref, o_ref, wz_ref, wo_ref, gate_ref, nw_ref, sc_ref, sh_ref, xo_ref, ho_ref) = refs
        nxt = (nw_ref, sc_ref, sh_ref)
    else:
        (x_ref, h_ref, o_ref, wz_ref, wo_ref, gate_ref, xo_ref) = refs
        nxt, ho_ref = None, None
    z = _dot(h_ref[0], wz_ref[...])
    g = o_ref[0] * jax.nn.silu(z)
    out = _dot(g.astype(BF16), wo_ref[...])
    _finish(x_ref, gate_ref, out, nxt, xo_ref, ho_ref)


def _tail_call(body, name, x, rows, consts, gate, nxt):
    bsz, seq, _ = x.shape
    emit_h = nxt is not None
    row = lambda a: pl.BlockSpec((1, ROW_TILE, a.shape[-1]), lambda b, i: (b, i, 0))
    const = lambda a: pl.BlockSpec(a.shape, lambda b, i: (0,) * a.ndim)
    vec = pl.BlockSpec((1, 1, D_MODEL), lambda b, i: (b, 0, 0))
    args = [x, *rows, *consts, gate]
    in_specs = [row(x)] + [row(a) for a in rows] + [const(a) for a in consts] + [vec]
    out_specs = [row(x)]
    out_shape = [jax.ShapeDtypeStruct(x.shape, F32)]
    if emit_h:
        nw, sc, sh = nxt
        args += [nw, sc, sh]
        in_specs += [const(nw), vec, vec]
        out_specs.append(row(x))
        out_shape.append(jax.ShapeDtypeStruct(x.shape, BF16))
    res = pl.pallas_call(
        functools.partial(body, emit_h=emit_h),
        grid=(bsz, seq // ROW_TILE),
        in_specs=in_specs, out_specs=out_specs, out_shape=out_shape,
        compiler_params=_params(2),
        name=name,
    )(*args)
    return (res[0], res[1]) if emit_h else (res[0], None)


def _attn_kernel(h0_ref, h1_ref, h2_ref, w_ref, c0_ref, s0_ref, c1_ref, s1_ref, c2_ref, s2_ref,
                 qw_ref, kw_ref, o_ref, q_scr, k_scr, v_scr, on_scr, lse_scr, bias_scr, *, seq):
    wk = WINDOW_KEYS
    n_chunks = seq // PROJ_ROWS
    chunk_blocks = PROJ_ROWS // wk
    h_refs = (h0_ref, h1_ref, h2_ref)
    tabs = ((c0_ref, s0_ref), (c1_ref, s1_ref), (c2_ref, s2_ref))

    qi = lax.broadcasted_iota(jnp.int32, (wk, 2 * wk), 0)
    kj = lax.broadcasted_iota(jnp.int32, (wk, 2 * wk), 1)
    band = (kj >= qi) & (kj <= qi + wk)
    bias_scr[0] = jnp.where(band, 0.0, NEG_INF)
    bias_scr[1] = jnp.where(band & (kj >= wk), 0.0, NEG_INF)

    def chunk_rows(c):
        start = c * PROJ_ROWS
        if not isinstance(c, int):
            start = pl.multiple_of(start, PROJ_ROWS)
        return pl.ds(start, PROJ_ROWS)

    def proj_dots(c):
        rows = chunk_rows(c)
        return [_dot(h_refs[g][0, rows, :], w_ref[0, :, g * 3 * HEAD_DIM:(g + 1) * 3 * HEAD_DIM])
                for g in range(N_PATTERNS)]

    def proj_store(c, prs):
        rows = chunk_rows(c)
        for g, pr in enumerate(prs):
            cos = tabs[g][0][0, rows, :]
            sin = tabs[g][1][0, rows, :]

            def norm_rope(t, w):
                t = t * lax.rsqrt(jnp.mean(t * t, axis=-1, keepdims=True) + EPS) * w
                return t * cos + pltpu.roll(t, HEAD_DIM // 2, 1) * sin

            q_scr[g, rows, :] = (norm_rope(pr[:, :HEAD_DIM], qw_ref[...]) * ATTN_SCALE).astype(BF16)
            k_scr[g, rows, :] = norm_rope(pr[:, HEAD_DIM:2 * HEAD_DIM], kw_ref[...]).astype(BF16)
            v_scr[g, rows, :] = pr[:, 2 * HEAD_DIM:].astype(BF16)

    def block_ids(c):
        return [(g, chunk_blocks * c + u, u) for g in range(N_PATTERNS) for u in range(chunk_blocks)]

    def is_first(g, j, u):
        n_blk = seq // ATTN_PATTERNS[g][1] // wk
        if n_blk == 1:
            return True
        if u % 2 == 1:
            return False
        if isinstance(j, int):
            return j % n_blk == 0
        return (j % n_blk) == 0

    def block_start(j):
        start = j * wk
        return start if isinstance(start, int) else pl.multiple_of(start, wk)

    def scores(g, j, u):
        first = is_first(g, j, u)
        q = q_scr[g, pl.ds(block_start(j), wk), :]
        if first is True:
            kk = k_scr[g, pl.ds(block_start(j), wk), :]
            bias = bias_scr[1, :, wk:]
        else:
            kk = k_scr[g, pl.ds(block_start(j - 1), 2 * wk), :]
            bias = bias_scr[0] if first is False else bias_scr[jnp.where(first, 1, 0)]
        return lax.dot_general(q, kk, (((1,), (1,)), ((), ())), preferred_element_type=F32) + bias

    def softmax(s):
        m = jnp.max(s, axis=-1, keepdims=True)
        p = jnp.exp(s - m)
        den = jnp.sum(p, axis=-1, keepdims=True)
        return p.astype(BF16), m, den

    def weighted(g, j, u, p):
        if is_first(g, j, u) is True:
            vv = v_scr[g, pl.ds(block_start(j), wk), :]
        else:
            vv = v_scr[g, pl.ds(block_start(j - 1), 2 * wk), :]
        return _dot(p, vv)

    def attn_store(g, j, o, m, den):
        dil = ATTN_PATTERNS[g][1]
        n_blk = seq // dil // wk
        if dil == 1:
            nat = pl.ds(block_start(j), wk)
        else:
            nat = pl.ds((j % n_blk) * (wk * dil) + j // n_blk, wk, stride=dil)
        on_scr[g, nat, :] = o * (1.0 / den)
        lse_scr[g, nat, :] = jnp.broadcast_to(m + jnp.log(den), (wk, HEAD_DIM))

    def step(c_attn, c_proj):
        ids = block_ids(c_attn) if c_attn is not None else []
        ss = [scores(g, j, u) for g, j, u in ids]
        prs = proj_dots(c_proj) if c_proj is not None else None
        sm = [softmax(s) for s in ss]
        os_ = [weighted(g, j, u, p) for (g, j, u), (p, _, _) in zip(ids, sm)]
        if prs is not None:
            proj_store(c_proj, prs)
        for (g, j, u), o, (_, m, den) in zip(ids, os_, sm):
            attn_store(g, j, o, m, den)

    step(None, 0)
    step(0, 1)

    def body(c, carry):
        step(c - 1, c)
        return carry

    lax.fori_loop(2, n_chunks, body, 0)
    step(n_chunks - 1, None)

    def merge_body(c, carry):
        rows = pl.ds(pl.multiple_of(c * PROJ_ROWS, PROJ_ROWS), PROJ_ROWS)
        ls = [lse_scr[g, rows, :] for g in range(N_PATTERNS)]
        mx = jnp.maximum(jnp.maximum(ls[0], ls[1]), ls[2])
        num = jnp.zeros((PROJ_ROWS, HEAD_DIM), F32)
        den = jnp.zeros((PROJ_ROWS, HEAD_DIM), F32)
        for g in range(N_PATTERNS):
            wgt = jnp.exp(ls[g] - mx)
            num = num + wgt * on_scr[g, rows, :]
            den = den + wgt
        o_ref[0, rows, :] = num / den
        return carry

    lax.fori_loop(0, seq // PROJ_ROWS, merge_body, 0)


def _attention(hs, w_qkv, tables, q_norm_w, k_norm_w):
    bsz, seq, _ = hs[0].shape
    stat = pltpu.VMEM((N_PATTERNS, seq, HEAD_DIM), F32)
    qkv = pltpu.VMEM((N_PATTERNS, seq, HEAD_DIM), BF16)
    once = pl.Buffered(1)
    hspec = pl.BlockSpec((1, seq, D_MODEL), lambda b, hd: (b, 0, 0))
    tspec = pl.BlockSpec((1, seq, HEAD_DIM), lambda b, hd: (b, 0, 0), pipeline_mode=once)
    vec = pl.BlockSpec((1, HEAD_DIM), lambda b, hd: (0, 0))
    return pl.pallas_call(
        functools.partial(_attn_kernel, seq=seq),
        grid=(bsz, HEADS),
        in_specs=[hspec, hspec, hspec,
                  pl.BlockSpec((1, D_MODEL, 3 * N_PATTERNS * HEAD_DIM), lambda b, hd: (hd, 0, 0)),
                  tspec, tspec, tspec, tspec, tspec, tspec, vec, vec],
        out_specs=pl.BlockSpec((1, seq, HEAD_DIM), lambda b, hd: (b, 0, hd)),
        out_shape=jax.ShapeDtypeStruct((bsz, seq, HEADS * HEAD_DIM), F32),
        scratch_shapes=[qkv, qkv, qkv, stat, stat,
                        pltpu.VMEM((2, WINDOW_KEYS, 2 * WINDOW_KEYS), F32)],
        compiler_params=_params(2),
        name="attention",
    )(*hs, w_qkv, *[t for pair in tables for t in pair], q_norm_w.reshape(1, HEAD_DIM), k_norm_w.reshape(1, HEAD_DIM))


def _head_major_qkv(w_in):
    w = w_in[:, :3 * ATTN_QKV].reshape(D_MODEL, 3, N_PATTERNS, HEADS, HEAD_DIM)
    return w.transpose(3, 0, 2, 1, 4).reshape(HEADS, D_MODEL, N_PATTERNS * 3 * HEAD_DIM).astype(BF16)


def kernel(x, c, positions, mod_w, mod_b, norm_w, even_w_in, conv_dw_w, conv_dw_b, conv_ln_w, conv_ln_b, ssm_lam_re, ssm_lam_im, ssm_log_dt, ssm_b_re, ssm_b_im, ssm_c_re, ssm_c_im, ssm_d, ssm_glu_w, ssm_glu_b, even_w_out, attn_w_in, attn_q_norm_w, attn_k_norm_w, attn_w_out):
    bsz, seq, _ = x.shape
    assert seq == 2048 and x.shape[-1] == D_MODEL and bsz % SSM_BATCH == 0
    mod = _modulation(c, mod_w, mod_b)
    shift = [mod[l, :, None, :D_MODEL] for l in range(DEPTH)]
    scale = [mod[l, :, None, D_MODEL:2 * D_MODEL] for l in range(DEPTH)]
    gate = [mod[l, :, None, 2 * D_MODEL:] for l in range(DEPTH)]
    nw = [norm_w[l].reshape(1, D_MODEL) for l in range(DEPTH)]
    residue_major = lambda a, d: a.reshape(bsz, seq // d, d, *a.shape[2:]).swapaxes(1, 2).reshape(a.shape)
    dils = [d for _, d in ATTN_PATTERNS]
    cos2, sin2 = _rope_tables(positions)
    tables = [(cos2, sin2) if d == 1 else (residue_major(cos2, d), residue_major(sin2, d)) for d in dils]

    h = _first_norm(x, nw[0], scale[0], shift[0])
    for layer in range(DEPTH):
        i = layer // 2
        nxt = (nw[layer + 1], scale[layer + 1], shift[layer + 1]) if layer + 1 < DEPTH else None
        if layer % 2 == 0:
            ya, u, bz = _even_in(h, even_w_in[i].astype(BF16), conv_dw_w[i], conv_dw_b[i],
                                 conv_ln_w[i], conv_ln_b[i])
            mats = _ssm_matrices(ssm_lam_re[i], ssm_lam_im[i], ssm_log_dt[i], ssm_b_re[i], ssm_b_im[i],
                                 ssm_c_re[i], ssm_c_im[i])
            ys = _ssm(u, mats, ssm_d[i])
            w_out = even_w_out[i].astype(BF16)
            x, h = _tail_call(_even_out_kernel, "even_out", x, [ya, ys, bz],
                              [ssm_glu_w[i].astype(BF16), ssm_glu_b[i].reshape(1, SSM_WIDTH),
                               w_out[:CONV_WIDTH], w_out[CONV_WIDTH:]], gate[layer], nxt)
        else:
            hs = [h if d == 1 else residue_major(h, d) for d in dils]
            o = _attention(hs, _head_major_qkv(attn_w_in[i]), tables, attn_q_norm_w[i], attn_k_norm_w[i])
            x, h = _tail_call(_odd_out_kernel, "odd_out", x, [h, o],
                              [attn_w_in[i][:, 3 * ATTN_QKV:].astype(BF16), attn_w_out[i].astype(BF16)],
                              gate[layer], nxt)
    return x
```

```python
import functools
import math

import jax
import jax.numpy as jnp
from jax import lax
from jax.experimental import pallas as pl
from jax.experimental.pallas import tpu as pltpu

F32 = jnp.float32
BF16 = jnp.bfloat16

D_MODEL = 1024
DEPTH = 4
CONV_WIDTH = 512
CONV_KERNEL = 31
SSM_WIDTH = 512
SSM_GROUP = 16
SSM_GROUPS = 32
SSM_STATE = 64
HEAD_DIM = 128
ATTN_PATTERNS = ((128, 1), (512, 4), (2048, 16))
N_PATTERNS = 3
HEADS = 8
ATTN_QKV = N_PATTERNS * HEADS * HEAD_DIM
ATTN_SCALE = HEAD_DIM ** -0.5
ROPE_THETA = 10000.0
EPS = 1e-6
NEG_INF = -1e30

LANES = 128
WINDOW_KEYS = 128
SSM_CHUNK = 16
SSM_LANE_GROUPS = LANES // SSM_GROUP
SSM_BLOCKS = SSM_WIDTH // LANES
SSM_STATE_COLS = SSM_LANE_GROUPS * SSM_STATE
SSM_BATCH = 4
ROW_TILE = 512
CONV_HALO = 32
CONV_ROWS = 64
PROJ_ROWS = 256
VMEM_LIMIT = 56 * 1024 * 1024


def _params(n_axes, vmem=VMEM_LIMIT):
    return pltpu.CompilerParams(dimension_semantics=("arbitrary",) * n_axes, vmem_limit_bytes=vmem)


def _adaln(x, nw, scale, shift):
    ms = jnp.mean(x * x, axis=-1, keepdims=True)
    return (x * lax.rsqrt(ms + EPS) * nw) * (1.0 + scale) + shift


def _dot(a, b):
    return jnp.dot(a, b, preferred_element_type=F32)


def _mod_kernel(c_ref, w_ref, b_ref, o_ref):
    o_ref[0] = jnp.dot(c_ref[...], w_ref[0], preferred_element_type=F32,
                       precision=lax.Precision.HIGHEST) + b_ref[0]


def _modulation(c, mod_w, mod_b):
    bsz = c.shape[0]
    nblk = 3 * D_MODEL // D_MODEL
    return pl.pallas_call(
        _mod_kernel,
        grid=(DEPTH, nblk),
        in_specs=[pl.BlockSpec((bsz, D_MODEL), lambda l, j: (0, 0)),
                  pl.BlockSpec((1, D_MODEL, D_MODEL), lambda l, j: (l, 0, j)),
                  pl.BlockSpec((1, 1, D_MODEL), lambda l, j: (l, 0, j))],
        out_specs=pl.BlockSpec((1, bsz, D_MODEL), lambda l, j: (l, 0, j)),
        out_shape=jax.ShapeDtypeStruct((DEPTH, bsz, 3 * D_MODEL), F32),
        compiler_params=_params(2),
        name="modulation",
    )(c, mod_w, mod_b.reshape(DEPTH, 1, 3 * D_MODEL))


def _rope_kernel(pos_ref, inv_ref, tab_ref):
    ang = pos_ref[0].astype(F32) * inv_ref[...]
    lane = lax.broadcasted_iota(jnp.int32, ang.shape, 1)
    tab_ref[0] = jnp.where(lane < HEAD_DIM // 2, jnp.cos(ang), jnp.sin(ang))


def _rope_table(positions):
    bsz, seq = positions.shape
    inv = ROPE_THETA ** (-jnp.arange(0, HEAD_DIM, 2, dtype=F32) / HEAD_DIM)
    inv2 = jnp.concatenate([inv, inv]).reshape(1, HEAD_DIM)
    return pl.pallas_call(
        _rope_kernel,
        grid=(bsz, seq // ROW_TILE),
        in_specs=[pl.BlockSpec((1, ROW_TILE, 1), lambda b, i: (b, i, 0)),
                  pl.BlockSpec((1, HEAD_DIM), lambda b, i: (0, 0))],
        out_specs=pl.BlockSpec((1, ROW_TILE, HEAD_DIM), lambda b, i: (b, i, 0)),
        out_shape=jax.ShapeDtypeStruct((bsz, seq, HEAD_DIM), F32),
        compiler_params=_params(2),
        name="rope_table",
    )(positions.reshape(bsz, seq, 1), inv2)


def _norm_kernel(x_ref, nw_ref, sc_ref, sh_ref, h_ref):
    h_ref[0] = _adaln(x_ref[0], nw_ref[...], sc_ref[0], sh_ref[0]).astype(BF16)


def _first_norm(x, nw, scale, shift):
    bsz, seq, _ = x.shape
    vec = pl.BlockSpec((1, 1, D_MODEL), lambda b, i: (b, 0, 0))
    row = pl.BlockSpec((1, ROW_TILE, D_MODEL), lambda b, i: (b, i, 0))
    return pl.pallas_call(
        _norm_kernel,
        grid=(bsz, seq // ROW_TILE),
        in_specs=[row, pl.BlockSpec((1, D_MODEL), lambda b, i: (0, 0)), vec, vec],
        out_specs=row,
        out_shape=jax.ShapeDtypeStruct(x.shape, BF16),
        compiler_params=_params(2),
        name="first_norm",
    )(x, nw, scale, shift)


def _even_in_kernel(h_ref, w_ref, dww_ref, dwb_ref, lnw_ref, lnb_ref,
                    ya_ref, u_ref, bz_ref, conv_scr, y_scr, az_scr):
    i = pl.program_id(1)
    h = h_ref[0]
    cw = CONV_WIDTH
    slabs = cw // LANES

    @pl.when(i == 0)
    def _():
        conv_scr[:, 0:CONV_HALO, :] = jnp.zeros((slabs, CONV_HALO, LANES), F32)

    a1 = _dot(h, w_ref[:, 0:cw])
    a2 = _dot(h, w_ref[:, cw:2 * cw])
    a = a1 * jax.nn.sigmoid(a2)
    for l in range(slabs):
        conv_scr[l, CONV_HALO:CONV_HALO + ROW_TILE, :] = a[:, l * LANES:(l + 1) * LANES]
    az_scr[...] = jax.nn.silu(_dot(h, w_ref[:, 2 * cw:3 * cw]))
    u_ref[0] = _dot(h, w_ref[:, 3 * cw:4 * cw])
    bz_ref[0] = _dot(h, w_ref[:, 4 * cw:5 * cw])

    first_tap = CONV_HALO - (CONV_KERNEL - 1)
    half = CONV_ROWS // 2

    def conv_block(r, carry):
        r0 = r * CONV_ROWS
        for l in range(slabs):
            lanes = slice(l * LANES, (l + 1) * LANES)
            for par in range(2):
                acc = jnp.broadcast_to(dwb_ref[:, lanes], (half, LANES))
                for k in range(CONV_KERNEL):
                    tap = conv_scr[l, pl.ds(r0 + first_tap + par + k, half, stride=2), :]
                    acc = acc + dww_ref[k:k + 1, lanes] * tap
                y_scr[l, pl.ds(r0 + par, half, stride=2), :] = acc
        return carry

    lax.fori_loop(0, ROW_TILE // CONV_ROWS, conv_block, 0)
    for l in range(slabs):
        conv_scr[l, 0:CONV_HALO, :] = conv_scr[l, ROW_TILE:ROW_TILE + CONV_HALO, :]

    def norm_block(r, carry):
        rows = pl.ds(pl.multiple_of(r * CONV_ROWS, CONV_ROWS), CONV_ROWS)
        acc = jnp.concatenate([y_scr[l, rows, :] for l in range(slabs)], axis=-1)
        mu = jnp.mean(acc, axis=-1, keepdims=True)
        xc = acc - mu
        y = xc * lax.rsqrt(jnp.mean(xc * xc, axis=-1, keepdims=True) + EPS)
        y = y * lnw_ref[...] + lnb_ref[...]
        ya_ref[0, rows, :] = (jax.nn.silu(y) * az_scr[rows, :]).astype(BF16)
        return carry

    lax.fori_loop(0, ROW_TILE // CONV_ROWS, norm_block, 0, unroll=4)


def _even_in(h, w_in, dw_w, dw_b, ln_w, ln_b):
    bsz, seq, _ = h.shape
    cw = CONV_WIDTH
    const = lambda shape: pl.BlockSpec(shape, lambda b, i: (0,) * len(shape))
    row = lambda width: pl.BlockSpec((1, ROW_TILE, width), lambda b, i: (b, i, 0))
    return pl.pallas_call(
        _even_in_kernel,
        grid=(bsz, seq // ROW_TILE),
        in_specs=[row(D_MODEL), const(w_in.shape), const((CONV_KERNEL, cw)),
                  const((1, cw)), const((1, cw)), const((1, cw))],
        out_specs=[row(cw), row(cw), row(cw)],
        out_shape=[jax.ShapeDtypeStruct((bsz, seq, cw), BF16),
                   jax.ShapeDtypeStruct((bsz, seq, cw), F32),
                   jax.ShapeDtypeStruct((bsz, seq, cw), F32)],
        scratch_shapes=[pltpu.VMEM((cw // LANES, CONV_HALO + ROW_TILE, LANES), F32),
                        pltpu.VMEM((cw // LANES, ROW_TILE, LANES), F32),
                        pltpu.VMEM((ROW_TILE, cw), F32)],
        compiler_params=_params(2),
        name="even_in",
    )(h, w_in, dw_w, dw_b.reshape(1, cw), ln_w.reshape(1, cw), ln_b.reshape(1, cw))


def _ssm_matrices(lam_re, lam_im, log_dt, b_re, b_im, c_re, c_im):
    hp = lax.Precision.HIGHEST
    t = SSM_CHUNK
    lr, li = lam_re.astype(F32), lam_im.astype(F32)
    dt = jnp.exp(log_dt.astype(F32))[:, None]

    def a_pow(k):
        kk = k.astype(F32)[:, None, None]
        mag = jnp.exp(kk * (lr * dt)[None])
        ang = kk * (li * dt)[None]
        return mag * jnp.cos(ang), mag * jnp.sin(ang)

    ar, ai = a_pow(jnp.ones((1,), F32))
    ar, ai = ar[0], ai[0]
    den = lr * lr + li * li
    nr = ar - 1.0
    kr = (nr * lr + ai * li) / den
    ki = (ai * lr - nr * li) / den
    br, bi = b_re.astype(F32), b_im.astype(F32)
    bbr = kr[..., None] * br - ki[..., None] * bi
    bbi = kr[..., None] * bi + ki[..., None] * br
    cr, ci = c_re.astype(F32), c_im.astype(F32)

    pr, pi = a_pow(jnp.arange(t + 1))
    wr = pr[:t, :, :, None] * bbr[None] - pi[:t, :, :, None] * bbi[None]
    wi = pr[:t, :, :, None] * bbi[None] + pi[:t, :, :, None] * bbr[None]
    kk = (jnp.einsum('gop,kgpi->kgoi', cr, wr, precision=hp)
          - jnp.einsum('gop,kgpi->kgoi', ci, wi, precision=hp))

    nb, gl, hh, pp = SSM_BLOCKS, SSM_LANE_GROUPS, SSM_GROUP, SSM_STATE
    grp = jnp.arange(gl)[:, None, None]
    row = jnp.arange(t * hh)[None, :, None]
    col = jnp.arange(t * LANES)[None, None, :]
    place = ((row // hh == col // LANES) & (row % hh == col % hh) & ((col % LANES) // hh == grp)).astype(BF16)
    prow = jnp.arange(pp)[None, :, None]
    pcol = jnp.arange(gl * pp)[None, None, :]
    spread = ((pcol % pp == prow) & (pcol // pp == grp)).astype(BF16)

    def rows_sgi(m):
        n = m.shape[-1]
        return m.reshape(nb, gl, t, hh, n).transpose(0, 2, 1, 3, 4).reshape(nb, t * LANES, n)

    lrow = jnp.arange(t * hh)[None, :, None]
    lcol = jnp.arange(t * hh)[None, None, :]
    s_idx = jnp.arange(t)[:, None, None]
    shift = ((lcol // hh == lrow // hh + s_idx) & (lcol % hh == lrow % hh)).astype(BF16)
    kk_i = kk.reshape(t, nb, gl, hh, hh).transpose(1, 2, 4, 0, 3).reshape(nb, gl, hh, t * hh)
    toe = jnp.einsum('cgil,sln->cgsin', kk_i.astype(BF16), shift,
                     preferred_element_type=BF16).reshape(nb, gl, t * hh, t * hh)
    m_intra = rows_sgi(jnp.einsum('cgab,gbn->cgan', toe, place, preferred_element_type=BF16))

    def state_cols(w):
        w = w[::-1].reshape(t, nb, gl, pp, hh).transpose(1, 2, 0, 4, 3).reshape(nb, gl, t * hh, pp)
        return rows_sgi(jnp.einsum('cgap,gpn->cgan', w.astype(BF16), spread, preferred_element_type=BF16))

    m_state = jnp.concatenate([state_cols(wr), state_cols(wi)], axis=-1)

    def in_rows(q):
        q = q.reshape(t, nb, gl, hh, pp).transpose(1, 2, 4, 0, 3).reshape(nb, gl, pp, t * hh)
        return jnp.einsum('cgpb,gbn->cgpn', q.astype(BF16), place,
                          preferred_element_type=BF16).reshape(nb, gl * pp, t * LANES)

    m_in_r = in_rows(pr[1:, :, None, :] * cr[None] - pi[1:, :, None, :] * ci[None])
    m_in_i = in_rows(-(pr[1:, :, None, :] * ci[None] + pi[1:, :, None, :] * cr[None]))

    n_lvl = int(math.log2(2048 // t))
    sr2, si2 = a_pow(t * (2 ** jnp.arange(n_lvl)))
    ap_r = sr2.reshape(n_lvl, nb, gl * pp).transpose(1, 0, 2)
    ap_i = si2.reshape(n_lvl, nb, gl * pp).transpose(1, 0, 2)
    return m_intra, m_state, m_in_r, m_in_i, ap_r, ap_i


def _ssm_kernel(u_ref, mintra_ref, mstate_ref, minr_ref, mini_ref, apr_ref, api_ref, d_ref, y_ref,
                x_scr, cr_scr, ci_scr, zr_scr, zi_scr, *, n_chunks, n_levels):
    t = SSM_CHUNK
    sc = SSM_STATE_COLS
    for bb in range(SSM_BATCH):
        for t0 in range(t):
            x_scr[bb * n_chunks:(bb + 1) * n_chunks, t0 * LANES:(t0 + 1) * LANES] = (
                u_ref[bb, pl.ds(t0, n_chunks, stride=t), :].astype(BF16))
    x = x_scr[...]
    s_loc = _dot(x, mstate_ref[0])

    zero = jnp.zeros((n_chunks, sc), F32)
    for p in range(2):
        zr_scr[p, 0:n_chunks, :] = zero
        zi_scr[p, 0:n_chunks, :] = zero
    for bb in range(SSM_BATCH):
        rows = slice(bb * n_chunks, (bb + 1) * n_chunks)
        zr_scr[0, n_chunks:, :] = s_loc[rows, :sc]
        zi_scr[0, n_chunks:, :] = s_loc[rows, sc:]
        for k in range(n_levels):
            src, dst = k % 2, 1 - (k % 2)
            sh = n_chunks - (1 << k)
            zr = zr_scr[src, n_chunks:, :]
            zi = zi_scr[src, n_chunks:, :]
            pr = zr_scr[src, sh:sh + n_chunks, :]
            pi = zi_scr[src, sh:sh + n_chunks, :]
            ar = apr_ref[0, k:k + 1, :]
            ai = api_ref[0, k:k + 1, :]
            zr_scr[dst, n_chunks:, :] = zr + ar * pr - ai * pi
            zi_scr[dst, n_chunks:, :] = zi + ar * pi + ai * pr
        fin = n_levels % 2
        cr_scr[rows, :] = zr_scr[fin, n_chunks - 1:2 * n_chunks - 1, :].astype(BF16)
        ci_scr[rows, :] = zi_scr[fin, n_chunks - 1:2 * n_chunks - 1, :].astype(BF16)

    carry_r = cr_scr[...]
    carry_i = ci_scr[...]
    for tp in range(t // 2):
        cols = slice(tp * 2 * LANES, (tp + 1) * 2 * LANES)
        yc = (_dot(x, mintra_ref[0, :, cols]) + _dot(carry_r, minr_ref[0, :, cols])
              + _dot(carry_i, mini_ref[0, :, cols]))
        for bb in range(SSM_BATCH):
            for j in range(2):
                y_ref[bb, pl.ds(2 * tp + j, n_chunks, stride=t), :] = (
                    yc[bb * n_chunks:(bb + 1) * n_chunks, j * LANES:(j + 1) * LANES])
    for bb in range(SSM_BATCH):
        y_ref[bb] = y_ref[bb] + d_ref[0] * u_ref[bb]


def _ssm(u, mats, d_skip):
    m_intra, m_state, m_in_r, m_in_i, ap_r, ap_i = mats
    bsz, seq, _ = u.shape
    n_chunks = seq // SSM_CHUNK
    n_levels = ap_r.shape[1]
    nb = SSM_BATCH
    once = pl.Buffered(1)
    blk = pl.BlockSpec((nb, seq, LANES), lambda c, b: (b, 0, c))
    mat = lambda m: pl.BlockSpec((1,) + m.shape[1:], lambda c, b: (c, 0, 0), pipeline_mode=once)
    lvl = pl.BlockSpec((1, n_levels, SSM_STATE_COLS), lambda c, b: (c, 0, 0))
    return pl.pallas_call(
        functools.partial(_ssm_kernel, n_chunks=n_chunks, n_levels=n_levels),
        grid=(SSM_BLOCKS, bsz // nb),
        in_specs=[blk, mat(m_intra), mat(m_state), mat(m_in_r), mat(m_in_i), lvl, lvl,
                  pl.BlockSpec((1, 1, LANES), lambda c, b: (c, 0, 0))],
        out_specs=blk,
        out_shape=jax.ShapeDtypeStruct(u.shape, F32),
        scratch_shapes=[pltpu.VMEM((nb * n_chunks, SSM_CHUNK * LANES), BF16),
                        pltpu.VMEM((nb * n_chunks, SSM_STATE_COLS), BF16),
                        pltpu.VMEM((nb * n_chunks, SSM_STATE_COLS), BF16),
                        pltpu.VMEM((2, 2 * n_chunks, SSM_STATE_COLS), F32),
                        pltpu.VMEM((2, 2 * n_chunks, SSM_STATE_COLS), F32)],
        compiler_params=_params(2),
        name="ssm",
    )(u, m_intra, m_state, m_in_r, m_in_i, ap_r, ap_i, d_skip.reshape(SSM_BLOCKS, 1, LANES))


def _finish(x_ref, gate_ref, out, nxt, xo_ref, ho_ref):
    xn = x_ref[0] + gate_ref[0] * out
    xo_ref[0] = xn
    if nxt is not None:
        nw_ref, sc_ref, sh_ref = nxt
        ho_ref[0] = _adaln(xn, nw_ref[...], sc_ref[0], sh_ref[0]).astype(BF16)


def _even_out_kernel(*refs, emit_h):
    if emit_h:
        (x_ref, ya_ref, ys_ref, bz_ref, gw_ref, gb_ref, wa_ref, ws_ref, gate_ref,
         nw_ref, sc_ref, sh_ref, xo_ref, ho_ref) = refs
        nxt = (nw_ref, sc_ref, sh_ref)
    else:
        (x_ref, ya_ref, ys_ref, bz_ref, gw_ref, gb_ref, wa_ref, ws_ref, gate_ref, xo_ref) = refs
        nxt, ho_ref = None, None
    y = ys_ref[0]
    y = 0.5 * y * (1.0 + lax.erf(y * (2.0 ** -0.5)))
    y = y * jax.nn.sigmoid(_dot(y.astype(BF16), gw_ref[...]) + gb_ref[...])
    y = y * jax.nn.silu(bz_ref[0])
    out = _dot(ya_ref[0], wa_ref[...]) + _dot(y.astype(BF16), ws_ref[...])
    _finish(x_ref, gate_ref, out, nxt, xo_ref, ho_ref)


def _odd_out_kernel(*refs, emit_h):
    if emit_h:
        (x_ref, h_ref, o_ref, wz_ref, wo_ref, gate_ref, nw_ref, sc_ref, sh_ref, xo_ref, ho_ref) = refs
        nxt = (nw_ref, sc_ref, sh_ref)
    else:
        (x_ref, h_ref, o_ref, wz_ref, wo_ref, gate_ref, xo_ref) = refs
        nxt, ho_ref = None, None
    z = _dot(h_ref[0], wz_ref[...])
    g = o_ref[0] * jax.nn.silu(z)
    out = _dot(g.astype(BF16), wo_ref[...])
    _finish(x_ref, gate_ref, out, nxt, xo_ref, ho_ref)


def _tail_call(body, name, x, rows, consts, gate, nxt):
    bsz, seq, _ = x.shape
    emit_h = nxt is not None
    row = lambda a: pl.BlockSpec((1, ROW_TILE, a.shape[-1]), lambda b, i: (b, i, 0))
    const = lambda a: pl.BlockSpec(a.shape, lambda b, i: (0,) * a.ndim)
    vec = pl.BlockSpec((1, 1, D_MODEL), lambda b, i: (b, 0, 0))
    args = [x, *rows, *consts, gate]
    in_specs = [row(x)] + [row(a) for a in rows] + [const(a) for a in consts] + [vec]
    out_specs = [row(x)]
    out_shape = [jax.ShapeDtypeStruct(x.shape, F32)]
    if emit_h:
        nw, sc, sh = nxt
        args += [nw, sc, sh]
        in_specs += [const(nw), vec, vec]
        out_specs.append(row(x))
        out_shape.append(jax.ShapeDtypeStruct(x.shape, BF16))
    res = pl.pallas_call(
        functools.partial(body, emit_h=emit_h),
        grid=(bsz, seq // ROW_TILE),
        in_specs=in_specs, out_specs=out_specs, out_shape=out_shape,
        compiler_params=_params(2),
        name=name,
    )(*args)
    return (res[0], res[1]) if emit_h else (res[0], None)


def _attn_kernel(h0_ref, h1_ref, h2_ref, w_ref, t0_ref, t1_ref, t2_ref,
                 qw_ref, kw_ref, o_ref, q_scr, k_scr, v_scr, on_scr, lse_scr, bias_scr, *, seq):
    wk = WINDOW_KEYS
    n_chunks = seq // PROJ_ROWS
    chunk_blocks = PROJ_ROWS // wk
    h_refs = (h0_ref, h1_ref, h2_ref)
    tabs = (t0_ref, t1_ref, t2_ref)

    qi = lax.broadcasted_iota(jnp.int32, (wk, 2 * wk), 0)
    kj = lax.broadcasted_iota(jnp.int32, (wk, 2 * wk), 1)
    band = (kj >= qi) & (kj <= qi + wk)
    bias_scr[0] = jnp.where(band, 0.0, NEG_INF)
    bias_scr[1] = jnp.where(band & (kj >= wk), 0.0, NEG_INF)

    def chunk_rows(c):
        start = c * PROJ_ROWS
        if not isinstance(c, int):
            start = pl.multiple_of(start, PROJ_ROWS)
        return pl.ds(start, PROJ_ROWS)

    def proj_dots(c):
        rows = chunk_rows(c)
        return [_dot(h_refs[g][0, rows, :], w_ref[0, :, g * 3 * HEAD_DIM:(g + 1) * 3 * HEAD_DIM])
                for g in range(N_PATTERNS)]

    def proj_store(c, prs):
        rows = chunk_rows(c)
        for g, pr in enumerate(prs):
            tab = tabs[g][0, rows, :]
            swapped = pltpu.roll(tab, HEAD_DIM // 2, 1)
            lower = lax.broadcasted_iota(jnp.int32, tab.shape, 1) < HEAD_DIM // 2
            cos = jnp.where(lower, tab, swapped)
            sin = jnp.where(lower, -swapped, tab)

            def norm_rope(t, w):
                t = t * lax.rsqrt(jnp.mean(t * t, axis=-1, keepdims=True) + EPS) * w
                return t * cos + pltpu.roll(t, HEAD_DIM // 2, 1) * sin

            q_scr[g, rows, :] = (norm_rope(pr[:, :HEAD_DIM], qw_ref[...]) * ATTN_SCALE).astype(BF16)
            k_scr[g, rows, :] = norm_rope(pr[:, HEAD_DIM:2 * HEAD_DIM], kw_ref[...]).astype(BF16)
            v_scr[g, rows, :] = pr[:, 2 * HEAD_DIM:].astype(BF16)

    def block_ids(c):
        return [(g, chunk_blocks * c + u, u) for g in range(N_PATTERNS) for u in range(chunk_blocks)]

    def is_first(g, j, u):
        n_blk = seq // ATTN_PATTERNS[g][1] // wk
        if n_blk == 1:
            return True
        if u % 2 == 1:
            return False
        if isinstance(j, int):
            return j % n_blk == 0
        return (j % n_blk) == 0

    def block_start(j):
        start = j * wk
        return start if isinstance(start, int) else pl.multiple_of(start, wk)

    def scores(g, j, u):
        first = is_first(g, j, u)
        q = q_scr[g, pl.ds(block_start(j), wk), :]
        if first is True:
            kk = k_scr[g, pl.ds(block_start(j), wk), :]
            bias = bias_scr[1, :, wk:]
        else:
            kk = k_scr[g, pl.ds(block_start(j - 1), 2 * wk), :]
            bias = bias_scr[0] if first is False else bias_scr[jnp.where(first, 1, 0)]
        return lax.dot_general(q, kk, (((1,), (1,)), ((), ())), preferred_element_type=F32) + bias

    def softmax(s):
        m = jnp.max(s, axis=-1, keepdims=True)
        p = jnp.exp(s - m)
        den = jnp.sum(p, axis=-1, keepdims=True)
        return p.astype(BF16), m, den

    def weighted(g, j, u, p):
        if is_first(g, j, u) is True:
            vv = v_scr[g, pl.ds(block_start(j), wk), :]
        else:
            vv = v_scr[g, pl.ds(block_start(j - 1), 2 * wk), :]
        return _dot(p, vv)

    def attn_store(g, j, o, m, den):
        dil = ATTN_PATTERNS[g][1]
        n_blk = seq // dil // wk
        if dil == 1:
            nat = pl.ds(block_start(j), wk)
        else:
            nat = pl.ds((j % n_blk) * (wk * dil) + j // n_blk, wk, stride=dil)
        on_scr[g, nat, :] = o * (1.0 / den)
        lse_scr[g, nat, :] = jnp.broadcast_to(m + jnp.log(den), (wk, HEAD_DIM))

    def step(c_attn, c_proj):
        ids = block_ids(c_attn) if c_attn is not None else []
        ss = [scores(g, j, u) for g, j, u in ids]
        prs = proj_dots(c_proj) if c_proj is not None else None
        sm = [softmax(s) for s in ss]
        os_ = [weighted(g, j, u, p) for (g, j, u), (p, _, _) in zip(ids, sm)]
        if prs is not None:
            proj_store(c_proj, prs)
        for (g, j, u), o, (_, m, den) in zip(ids, os_, sm):
            attn_store(g, j, o, m, den)

    step(None, 0)
    step(0, 1)

    def body(c, carry):
        step(c - 1, c)
        return carry

    lax.fori_loop(2, n_chunks, body, 0)
    step(n_chunks - 1, None)

    def merge_body(c, carry):
        rows = pl.ds(pl.multiple_of(c * PROJ_ROWS, PROJ_ROWS), PROJ_ROWS)
        ls = [lse_scr[g, rows, :] for g in range(N_PATTERNS)]
        mx = jnp.maximum(jnp.maximum(ls[0], ls[1]), ls[2])
        num = jnp.zeros((PROJ_ROWS, HEAD_DIM), F32)
        den = jnp.zeros((PROJ_ROWS, HEAD_DIM), F32)
        for g in range(N_PATTERNS):
            wgt = jnp.exp(ls[g] - mx)
            num = num + wgt * on_scr[g, rows, :]
            den = den + wgt
        o_ref[0, rows, :] = num / den
        return carry

    lax.fori_loop(0, seq // PROJ_ROWS, merge_body, 0)


def _attention(hs, w_qkv, tables, q_norm_w, k_norm_w):
    bsz, seq, _ = hs[0].shape
    stat = pltpu.VMEM((N_PATTERNS, seq, HEAD_DIM), F32)
    qkv = pltpu.VMEM((N_PATTERNS, seq, HEAD_DIM), BF16)
    hspec = pl.BlockSpec((1, seq, D_MODEL), lambda b, hd: (b, 0, 0))
    tspec = pl.BlockSpec((1, seq, HEAD_DIM), lambda b, hd: (b, 0, 0))
    vec = pl.BlockSpec((1, HEAD_DIM), lambda b, hd: (0, 0))
    return pl.pallas_call(
        functools.partial(_attn_kernel, seq=seq),
        grid=(bsz, HEADS),
        in_specs=[hspec, hspec, hspec,
                  pl.BlockSpec((1, D_MODEL, 3 * N_PATTERNS * HEAD_DIM), lambda b, hd: (hd, 0, 0)),
                  tspec, tspec, tspec, vec, vec],
        out_specs=pl.BlockSpec((1, seq, HEAD_DIM), lambda b, hd: (b, 0, hd)),
        out_shape=jax.ShapeDtypeStruct((bsz, seq, HEADS * HEAD_DIM), F32),
        scratch_shapes=[qkv, qkv, qkv, stat, stat,
                        pltpu.VMEM((2, WINDOW_KEYS, 2 * WINDOW_KEYS), F32)],
        compiler_params=_params(2),
        name="attention",
    )(*hs, w_qkv, *tables, q_norm_w.reshape(1, HEAD_DIM), k_norm_w.reshape(1, HEAD_DIM))


def _head_major_qkv(w_in):
    w = w_in[:, :3 * ATTN_QKV].reshape(D_MODEL, 3, N_PATTERNS, HEADS, HEAD_DIM)
    return w.transpose(3, 0, 2, 1, 4).reshape(HEADS, D_MODEL, N_PATTERNS * 3 * HEAD_DIM).astype(BF16)


def kernel(x, c, positions, mod_w, mod_b, norm_w, even_w_in, conv_dw_w, conv_dw_b, conv_ln_w, conv_ln_b, ssm_lam_re, ssm_lam_im, ssm_log_dt, ssm_b_re, ssm_b_im, ssm_c_re, ssm_c_im, ssm_d, ssm_glu_w, ssm_glu_b, even_w_out, attn_w_in, attn_q_norm_w, attn_k_norm_w, attn_w_out):
    bsz, seq, _ = x.shape
    assert seq == 2048 and x.shape[-1] == D_MODEL and bsz % SSM_BATCH == 0
    mod = _modulation(c, mod_w, mod_b)
    shift = [mod[l, :, None, :D_MODEL] for l in range(DEPTH)]
    scale = [mod[l, :, None, D_MODEL:2 * D_MODEL] for l in range(DEPTH)]
    gate = [mod[l, :, None, 2 * D_MODEL:] for l in range(DEPTH)]
    nw = [norm_w[l].reshape(1, D_MODEL) for l in range(DEPTH)]
    residue_major = lambda a, d: a.reshape(bsz, seq // d, d, *a.shape[2:]).swapaxes(1, 2).reshape(a.shape)
    dils = [d for _, d in ATTN_PATTERNS]
    table = _rope_table(positions)
    tables = [table if d == 1 else residue_major(table, d) for d in dils]

    h = _first_norm(x, nw[0], scale[0], shift[0])
    for layer in range(DEPTH):
        i = layer // 2
        nxt = (nw[layer + 1], scale[layer + 1], shift[layer + 1]) if layer + 1 < DEPTH else None
        if layer % 2 == 0:
            ya, u, bz = _even_in(h, even_w_in[i].astype(BF16), conv_dw_w[i], conv_dw_b[i],
                                 conv_ln_w[i], conv_ln_b[i])
            mats = _ssm_matrices(ssm_lam_re[i], ssm_lam_im[i], ssm_log_dt[i], ssm_b_re[i], ssm_b_im[i],
                                 ssm_c_re[i], ssm_c_im[i])
            ys = _ssm(u, mats, ssm_d[i])
            w_out = even_w_out[i].astype(BF16)
            x, h = _tail_call(_even_out_kernel, "even_out", x, [ya, ys, bz],
                              [ssm_glu_w[i].astype(BF16), ssm_glu_b[i].reshape(1, SSM_WIDTH),
                               w_out[:CONV_WIDTH], w_out[CONV_WIDTH:]], gate[layer], nxt)
        else:
            hs = [h if d == 1 else residue_major(h, d) for d in dils]
            o = _attention(hs, _head_major_qkv(attn_w_in[i]), tables, attn_q_norm_w[i], attn_k_norm_w[i])
            x, h = _tail_call(_odd_out_kernel, "odd_out", x, [h, o],
                              [attn_w_in[i][:, 3 * ATTN_QKV:].astype(BF16), attn_w_out[i].astype(BF16)],
                              gate[layer], nxt)
    return x
```

```python
import functools
import math

import jax
import jax.numpy as jnp
from jax import lax
from jax.experimental import pallas as pl
from jax.experimental.pallas import tpu as pltpu

F32 = jnp.float32
BF16 = jnp.bfloat16

D_MODEL = 1024
DEPTH = 4
CONV_WIDTH = 512
CONV_KERNEL = 31
SSM_WIDTH = 512
SSM_GROUP = 16
SSM_GROUPS = 32
SSM_STATE = 64
HEAD_DIM = 128
ATTN_PATTERNS = ((128, 1), (512, 4), (2048, 16))
N_PATTERNS = 3
HEADS = 8
ATTN_QKV = N_PATTERNS * HEADS * HEAD_DIM
ATTN_SCALE = HEAD_DIM ** -0.5
ROPE_THETA = 10000.0
EPS = 1e-6
NEG_INF = -1e30

LANES = 128
WINDOW_KEYS = 128
SSM_CHUNK = 16
SSM_LANE_GROUPS = LANES // SSM_GROUP
SSM_BLOCKS = SSM_WIDTH // LANES
SSM_STATE_COLS = SSM_LANE_GROUPS * SSM_STATE
SSM_BATCH = 4
ROW_TILE = 512
CONV_HALO = 32
CONV_ROWS = 64
PROJ_ROWS = 256
VMEM_LIMIT = 56 * 1024 * 1024
ATTN_VMEM_LIMIT = 58 * 1024 * 1024
HEAD_PAIR = 2


def _params(n_axes, vmem=VMEM_LIMIT):
    return pltpu.CompilerParams(dimension_semantics=("arbitrary",) * n_axes, vmem_limit_bytes=vmem)


def _adaln(x, nw, scale, shift):
    ms = jnp.mean(x * x, axis=-1, keepdims=True)
    return (x * lax.rsqrt(ms + EPS) * nw) * (1.0 + scale) + shift


def _dot(a, b):
    return jnp.dot(a, b, preferred_element_type=F32)


def _mod_kernel(c_ref, w_ref, b_ref, o_ref):
    o_ref[0] = jnp.dot(c_ref[...], w_ref[0], preferred_element_type=F32,
                       precision=lax.Precision.HIGHEST) + b_ref[0]


def _modulation(c, mod_w, mod_b):
    bsz = c.shape[0]
    nblk = 3 * D_MODEL // D_MODEL
    return pl.pallas_call(
        _mod_kernel,
        grid=(DEPTH, nblk),
        in_specs=[pl.BlockSpec((bsz, D_MODEL), lambda l, j: (0, 0)),
                  pl.BlockSpec((1, D_MODEL, D_MODEL), lambda l, j: (l, 0, j)),
                  pl.BlockSpec((1, 1, D_MODEL), lambda l, j: (l, 0, j))],
        out_specs=pl.BlockSpec((1, bsz, D_MODEL), lambda l, j: (l, 0, j)),
        out_shape=jax.ShapeDtypeStruct((DEPTH, bsz, 3 * D_MODEL), F32),
        compiler_params=_params(2),
        name="modulation",
    )(c, mod_w, mod_b.reshape(DEPTH, 1, 3 * D_MODEL))


def _rope_kernel(pos_ref, inv_ref, tab_ref):
    ang = pos_ref[0].astype(F32) * inv_ref[...]
    lane = lax.broadcasted_iota(jnp.int32, ang.shape, 1)
    tab_ref[0] = jnp.where(lane < HEAD_DIM // 2, jnp.cos(ang), jnp.sin(ang))


def _rope_table(positions):
    bsz, seq = positions.shape
    inv = ROPE_THETA ** (-jnp.arange(0, HEAD_DIM, 2, dtype=F32) / HEAD_DIM)
    inv2 = jnp.concatenate([inv, inv]).reshape(1, HEAD_DIM)
    return pl.pallas_call(
        _rope_kernel,
        grid=(bsz, seq // ROW_TILE),
        in_specs=[pl.BlockSpec((1, ROW_TILE, 1), lambda b, i: (b, i, 0)),
                  pl.BlockSpec((1, HEAD_DIM), lambda b, i: (0, 0))],
        out_specs=pl.BlockSpec((1, ROW_TILE, HEAD_DIM), lambda b, i: (b, i, 0)),
        out_shape=jax.ShapeDtypeStruct((bsz, seq, HEAD_DIM), F32),
        compiler_params=_params(2),
        name="rope_table",
    )(positions.reshape(bsz, seq, 1), inv2)


def _norm_kernel(x_ref, nw_ref, sc_ref, sh_ref, h_ref):
    h_ref[0] = _adaln(x_ref[0], nw_ref[...], sc_ref[0], sh_ref[0]).astype(BF16)


def _first_norm(x, nw, scale, shift):
    bsz, seq, _ = x.shape
    vec = pl.BlockSpec((1, 1, D_MODEL), lambda b, i: (b, 0, 0))
    row = pl.BlockSpec((1, ROW_TILE, D_MODEL), lambda b, i: (b, i, 0))
    return pl.pallas_call(
        _norm_kernel,
        grid=(bsz, seq // ROW_TILE),
        in_specs=[row, pl.BlockSpec((1, D_MODEL), lambda b, i: (0, 0)), vec, vec],
        out_specs=row,
        out_shape=jax.ShapeDtypeStruct(x.shape, BF16),
        compiler_params=_params(2),
        name="first_norm",
    )(x, nw, scale, shift)


def _even_in_kernel(h_ref, w_ref, dww_ref, dwb_ref, lnw_ref, lnb_ref,
                    ya_ref, u_ref, bz_ref, conv_scr, y_scr, az_scr):
    i = pl.program_id(1)
    h = h_ref[0]
    cw = CONV_WIDTH
    slabs = cw // LANES

    @pl.when(i == 0)
    def _():
        conv_scr[:, 0:CONV_HALO, :] = jnp.zeros((slabs, CONV_HALO, LANES), F32)

    a1 = _dot(h, w_ref[:, 0:cw])
    a2 = _dot(h, w_ref[:, cw:2 * cw])
    a = a1 * jax.nn.sigmoid(a2)
    for l in range(slabs):
        conv_scr[l, CONV_HALO:CONV_HALO + ROW_TILE, :] = a[:, l * LANES:(l + 1) * LANES]
    az_scr[...] = jax.nn.silu(_dot(h, w_ref[:, 2 * cw:3 * cw]))
    u_ref[0] = _dot(h, w_ref[:, 3 * cw:4 * cw])
    bz_ref[0] = _dot(h, w_ref[:, 4 * cw:5 * cw])

    first_tap = CONV_HALO - (CONV_KERNEL - 1)
    half = CONV_ROWS // 2

    def conv_block(r, carry):
        r0 = r * CONV_ROWS
        for l in range(slabs):
            lanes = slice(l * LANES, (l + 1) * LANES)
            for par in range(2):
                acc = jnp.broadcast_to(dwb_ref[:, lanes], (half, LANES))
                for k in range(CONV_KERNEL):
                    tap = conv_scr[l, pl.ds(r0 + first_tap + par + k, half, stride=2), :]
                    acc = acc + dww_ref[k:k + 1, lanes] * tap
                y_scr[l, pl.ds(r0 + par, half, stride=2), :] = acc
        return carry

    lax.fori_loop(0, ROW_TILE // CONV_ROWS, conv_block, 0)
    for l in range(slabs):
        conv_scr[l, 0:CONV_HALO, :] = conv_scr[l, ROW_TILE:ROW_TILE + CONV_HALO, :]

    def norm_block(r, carry):
        rows = pl.ds(pl.multiple_of(r * CONV_ROWS, CONV_ROWS), CONV_ROWS)
        acc = jnp.concatenate([y_scr[l, rows, :] for l in range(slabs)], axis=-1)
        mu = jnp.mean(acc, axis=-1, keepdims=True)
        xc = acc - mu
        y = xc * lax.rsqrt(jnp.mean(xc * xc, axis=-1, keepdims=True) + EPS)
        y = y * lnw_ref[...] + lnb_ref[...]
        ya_ref[0, rows, :] = (jax.nn.silu(y) * az_scr[rows, :]).astype(BF16)
        return carry

    lax.fori_loop(0, ROW_TILE // CONV_ROWS, norm_block, 0, unroll=4)


def _even_in(h, w_in, dw_w, dw_b, ln_w, ln_b):
    bsz, seq, _ = h.shape
    cw = CONV_WIDTH
    const = lambda shape: pl.BlockSpec(shape, lambda b, i: (0,) * len(shape))
    row = lambda width: pl.BlockSpec((1, ROW_TILE, width), lambda b, i: (b, i, 0))
    return pl.pallas_call(
        _even_in_kernel,
        grid=(bsz, seq // ROW_TILE),
        in_specs=[row(D_MODEL), const(w_in.shape), const((CONV_KERNEL, cw)),
                  const((1, cw)), const((1, cw)), const((1, cw))],
        out_specs=[row(cw), row(cw), row(cw)],
        out_shape=[jax.ShapeDtypeStruct((bsz, seq, cw), BF16),
                   jax.ShapeDtypeStruct((bsz, seq, cw), F32),
                   jax.ShapeDtypeStruct((bsz, seq, cw), F32)],
        scratch_shapes=[pltpu.VMEM((cw // LANES, CONV_HALO + ROW_TILE, LANES), F32),
                        pltpu.VMEM((cw // LANES, ROW_TILE, LANES), F32),
                        pltpu.VMEM((ROW_TILE, cw), F32)],
        compiler_params=_params(2),
        name="even_in",
    )(h, w_in, dw_w, dw_b.reshape(1, cw), ln_w.reshape(1, cw), ln_b.reshape(1, cw))


def _ssm_matrices(lam_re, lam_im, log_dt, b_re, b_im, c_re, c_im):
    hp = lax.Precision.HIGHEST
    t = SSM_CHUNK
    lr, li = lam_re.astype(F32), lam_im.astype(F32)
    dt = jnp.exp(log_dt.astype(F32))[:, None]

    def a_pow(k):
        kk = k.astype(F32)[:, None, None]
        mag = jnp.exp(kk * (lr * dt)[None])
        ang = kk * (li * dt)[None]
        return mag * jnp.cos(ang), mag * jnp.sin(ang)

    ar, ai = a_pow(jnp.ones((1,), F32))
    ar, ai = ar[0], ai[0]
    den = lr * lr + li * li
    nr = ar - 1.0
    kr = (nr * lr + ai * li) / den
    ki = (ai * lr - nr * li) / den
    br, bi = b_re.astype(F32), b_im.astype(F32)
    bbr = kr[..., None] * br - ki[..., None] * bi
    bbi = kr[..., None] * bi + ki[..., None] * br
    cr, ci = c_re.astype(F32), c_im.astype(F32)

    pr, pi = a_pow(jnp.arange(t + 1))
    wr = pr[:t, :, :, None] * bbr[None] - pi[:t, :, :, None] * bbi[None]
    wi = pr[:t, :, :, None] * bbi[None] + pi[:t, :, :, None] * bbr[None]
    kk = (jnp.einsum('gop,kgpi->kgoi', cr, wr, precision=hp)
          - jnp.einsum('gop,kgpi->kgoi', ci, wi, precision=hp))

    nb, gl, hh, pp = SSM_BLOCKS, SSM_LANE_GROUPS, SSM_GROUP, SSM_STATE
    grp = jnp.arange(gl)[:, None, None]
    row = jnp.arange(t * hh)[None, :, None]
    col = jnp.arange(t * LANES)[None, None, :]
    place = ((row // hh == col // LANES) & (row % hh == col % hh) & ((col % LANES) // hh == grp)).astype(BF16)
    prow = jnp.arange(pp)[None, :, None]
    pcol = jnp.arange(gl * pp)[None, None, :]
    spread = ((pcol % pp == prow) & (pcol // pp == grp)).astype(BF16)

    def rows_sgi(m):
        n = m.shape[-1]
        return m.reshape(nb, gl, t, hh, n).transpose(0, 2, 1, 3, 4).reshape(nb, t * LANES, n)

    lrow = jnp.arange(t * hh)[None, :, None]
    lcol = jnp.arange(t * hh)[None, None, :]
    s_idx = jnp.arange(t)[:, None, None]
    shift = ((lcol // hh == lrow // hh + s_idx) & (lcol % hh == lrow % hh)).astype(BF16)
    kk_i = kk.reshape(t, nb, gl, hh, hh).transpose(1, 2, 4, 0, 3).reshape(nb, gl, hh, t * hh)
    toe = jnp.einsum('cgil,sln->cgsin', kk_i.astype(BF16), shift,
                     preferred_element_type=BF16).reshape(nb, gl, t * hh, t * hh)
    m_intra = rows_sgi(jnp.einsum('cgab,gbn->cgan', toe, place, preferred_element_type=BF16))

    def state_cols(w):
        w = w[::-1].reshape(t, nb, gl, pp, hh).transpose(1, 2, 0, 4, 3).reshape(nb, gl, t * hh, pp)
        return rows_sgi(jnp.einsum('cgap,gpn->cgan', w.astype(BF16), spread, preferred_element_type=BF16))

    m_state = jnp.concatenate([state_cols(wr), state_cols(wi)], axis=-1)

    def in_rows(q):
        q = q.reshape(t, nb, gl, hh, pp).transpose(1, 2, 4, 0, 3).reshape(nb, gl, pp, t * hh)
        return jnp.einsum('cgpb,gbn->cgpn', q.astype(BF16), place,
                          preferred_element_type=BF16).reshape(nb, gl * pp, t * LANES)

    m_in_r = in_rows(pr[1:, :, None, :] * cr[None] - pi[1:, :, None, :] * ci[None])
    m_in_i = in_rows(-(pr[1:, :, None, :] * ci[None] + pi[1:, :, None, :] * cr[None]))

    n_lvl = int(math.log2(2048 // t))
    sr2, si2 = a_pow(t * (2 ** jnp.arange(n_lvl)))
    ap_r = sr2.reshape(n_lvl, nb, gl * pp).transpose(1, 0, 2)
    ap_i = si2.reshape(n_lvl, nb, gl * pp).transpose(1, 0, 2)
    return m_intra, m_state, m_in_r, m_in_i, ap_r, ap_i


def _ssm_kernel(u_ref, mintra_ref, mstate_ref, minr_ref, mini_ref, apr_ref, api_ref, d_ref, y_ref,
                x_scr, cr_scr, ci_scr, zr_scr, zi_scr, *, n_chunks, n_levels):
    t = SSM_CHUNK
    sc = SSM_STATE_COLS
    for bb in range(SSM_BATCH):
        for t0 in range(t):
            x_scr[bb * n_chunks:(bb + 1) * n_chunks, t0 * LANES:(t0 + 1) * LANES] = (
                u_ref[bb, pl.ds(t0, n_chunks, stride=t), :].astype(BF16))
    x = x_scr[...]
    s_loc = _dot(x, mstate_ref[0])

    zero = jnp.zeros((n_chunks, sc), F32)
    for p in range(2):
        zr_scr[p, 0:n_chunks, :] = zero
        zi_scr[p, 0:n_chunks, :] = zero
    for bb in range(SSM_BATCH):
        rows = slice(bb * n_chunks, (bb + 1) * n_chunks)
        zr_scr[0, n_chunks:, :] = s_loc[rows, :sc]
        zi_scr[0, n_chunks:, :] = s_loc[rows, sc:]
        for k in range(n_levels):
            src, dst = k % 2, 1 - (k % 2)
            sh = n_chunks - (1 << k)
            zr = zr_scr[src, n_chunks:, :]
            zi = zi_scr[src, n_chunks:, :]
            pr = zr_scr[src, sh:sh + n_chunks, :]
            pi = zi_scr[src, sh:sh + n_chunks, :]
            ar = apr_ref[0, k:k + 1, :]
            ai = api_ref[0, k:k + 1, :]
            zr_scr[dst, n_chunks:, :] = zr + ar * pr - ai * pi
            zi_scr[dst, n_chunks:, :] = zi + ar * pi + ai * pr
        fin = n_levels % 2
        cr_scr[rows, :] = zr_scr[fin, n_chunks - 1:2 * n_chunks - 1, :].astype(BF16)
        ci_scr[rows, :] = zi_scr[fin, n_chunks - 1:2 * n_chunks - 1, :].astype(BF16)

    carry_r = cr_scr[...]
    carry_i = ci_scr[...]
    for tp in range(t // 2):
        cols = slice(tp * 2 * LANES, (tp + 1) * 2 * LANES)
        yc = (_dot(x, mintra_ref[0, :, cols]) + _dot(carry_r, minr_ref[0, :, cols])
              + _dot(carry_i, mini_ref[0, :, cols]))
        for bb in range(SSM_BATCH):
            for j in range(2):
                y_ref[bb, pl.ds(2 * tp + j, n_chunks, stride=t), :] = (
                    yc[bb * n_chunks:(bb + 1) * n_chunks, j * LANES:(j + 1) * LANES])
    for bb in range(SSM_BATCH):
        y_ref[bb] = y_ref[bb] + d_ref[0] * u_ref[bb]


def _ssm(u, mats, d_skip):
    m_intra, m_state, m_in_r, m_in_i, ap_r, ap_i = mats
    bsz, seq, _ = u.shape
    n_chunks = seq // SSM_CHUNK
    n_levels = ap_r.shape[1]
    nb = SSM_BATCH
    once = pl.Buffered(1)
    blk = pl.BlockSpec((nb, seq, LANES), lambda c, b: (b, 0, c))
    mat = lambda m: pl.BlockSpec((1,) + m.shape[1:], lambda c, b: (c, 0, 0), pipeline_mode=once)
    lvl = pl.BlockSpec((1, n_levels, SSM_STATE_COLS), lambda c, b: (c, 0, 0))
    return pl.pallas_call(
        functools.partial(_ssm_kernel, n_chunks=n_chunks, n_levels=n_levels),
        grid=(SSM_BLOCKS, bsz // nb),
        in_specs=[blk, mat(m_intra), mat(m_state), mat(m_in_r), mat(m_in_i), lvl, lvl,
                  pl.BlockSpec((1, 1, LANES), lambda c, b: (c, 0, 0))],
        out_specs=blk,
        out_shape=jax.ShapeDtypeStruct(u.shape, F32),
        scratch_shapes=[pltpu.VMEM((nb * n_chunks, SSM_CHUNK * LANES), BF16),
                        pltpu.VMEM((nb * n_chunks, SSM_STATE_COLS), BF16),
                        pltpu.VMEM((nb * n_chunks, SSM_STATE_COLS), BF16),
                        pltpu.VMEM((2, 2 * n_chunks, SSM_STATE_COLS), F32),
                        pltpu.VMEM((2, 2 * n_chunks, SSM_STATE_COLS), F32)],
        compiler_params=_params(2),
        name="ssm",
    )(u, m_intra, m_state, m_in_r, m_in_i, ap_r, ap_i, d_skip.reshape(SSM_BLOCKS, 1, LANES))


def _finish(x_ref, gate_ref, out, nxt, xo_ref, ho_ref):
    xn = x_ref[0] + gate_ref[0] * out
    xo_ref[0] = xn
    if nxt is not None:
        nw_ref, sc_ref, sh_ref = nxt
        ho_ref[0] = _adaln(xn, nw_ref[...], sc_ref[0], sh_ref[0]).astype(BF16)


def _even_out_kernel(*refs, emit_h):
    if emit_h:
        (x_ref, ya_ref, ys_ref, bz_ref, gw_ref, gb_ref, wa_ref, ws_ref, gate_ref,
         nw_ref, sc_ref, sh_ref, xo_ref, ho_ref) = refs
        nxt = (nw_ref, sc_ref, sh_ref)
    else:
        (x_ref, ya_ref, ys_ref, bz_ref, gw_ref, gb_ref, wa_ref, ws_ref, gate_ref, xo_ref) = refs
        nxt, ho_ref = None, None
    y = ys_ref[0]
    y = 0.5 * y * (1.0 + lax.erf(y * (2.0 ** -0.5)))
    y = y * jax.nn.sigmoid(_dot(y.astype(BF16), gw_ref[...]) + gb_ref[...])
    y = y * jax.nn.silu(bz_ref[0])
    out = _dot(ya_ref[0], wa_ref[...]) + _dot(y.astype(BF16), ws_ref[...])
    _finish(x_ref, gate_ref, out, nxt, xo_ref, ho_ref)


def _odd_out_kernel(*refs, emit_h):
    if emit_h:
        (x_ref, h_ref, o_ref, wz_ref, wo_ref, gate_ref, nw_ref, sc_ref, sh_ref, xo_ref, ho_ref) = refs
        nxt = (nw_ref, sc_ref, sh_ref)
    else:
        (x_ref, h_ref, o_ref, wz_ref, wo_ref, gate_ref, xo_ref) = refs
        nxt, ho_ref = None, None
    z = _dot(h_ref[0], wz_ref[...])
    g = o_ref[0] * jax.nn.silu(z)
    out = _dot(g.astype(BF16), wo_ref[...])
    _finish(x_ref, gate_ref, out, nxt, xo_ref, ho_ref)


def _tail_call(body, name, x, rows, consts, gate, nxt):
    bsz, seq, _ = x.shape
    emit_h = nxt is not None
    row = lambda a: pl.BlockSpec((1, ROW_TILE, a.shape[-1]), lambda b, i: (b, i, 0))
    const = lambda a: pl.BlockSpec(a.shape, lambda b, i: (0,) * a.ndim)
    vec = pl.BlockSpec((1, 1, D_MODEL), lambda b, i: (b, 0, 0))
    args = [x, *rows, *consts, gate]
    in_specs = [row(x)] + [row(a) for a in rows] + [const(a) for a in consts] + [vec]
    out_specs = [row(x)]
    out_shape = [jax.ShapeDtypeStruct(x.shape, F32)]
    if emit_h:
        nw, sc, sh = nxt
        args += [nw, sc, sh]
        in_specs += [const(nw), vec, vec]
        out_specs.append(row(x))
        out_shape.append(jax.ShapeDtypeStruct(x.shape, BF16))
    res = pl.pallas_call(
        functools.partial(body, emit_h=emit_h),
        grid=(bsz, seq // ROW_TILE),
        in_specs=in_specs, out_specs=out_specs, out_shape=out_shape,
        compiler_params=_params(2),
        name=name,
    )(*args)
    return (res[0], res[1]) if emit_h else (res[0], None)


def _attn_kernel(h0_ref, h1_ref, h2_ref, w_ref, t0_ref, t1_ref, t2_ref,
                 qw_ref, kw_ref, o_ref, q_scr, k_scr, v_scr, on_scr, lse_scr, bias_scr, *, seq):
    wk = WINDOW_KEYS
    n_chunks = seq // PROJ_ROWS
    chunk_blocks = PROJ_ROWS // wk
    h_refs = (h0_ref, h1_ref, h2_ref)
    tabs = (t0_ref, t1_ref, t2_ref)
    head_cols = 3 * HEAD_DIM

    qi = lax.broadcasted_iota(jnp.int32, (wk, 2 * wk), 0)
    kj = lax.broadcasted_iota(jnp.int32, (wk, 2 * wk), 1)
    band = (kj >= qi) & (kj <= qi + wk)
    bias_scr[0] = jnp.where(band, 0.0, NEG_INF)
    bias_scr[1] = jnp.where(band & (kj >= wk), 0.0, NEG_INF)

    def chunk_rows(c):
        start = c * PROJ_ROWS
        if not isinstance(c, int):
            start = pl.multiple_of(start, PROJ_ROWS)
        return pl.ds(start, PROJ_ROWS)

    def proj_dots(c):
        rows = chunk_rows(c)
        width = HEAD_PAIR * head_cols
        return [_dot(h_refs[g][0, rows, :], w_ref[0, :, g * width:(g + 1) * width]) for g in range(N_PATTERNS)]

    def proj_store(c, prs):
        rows = chunk_rows(c)
        for g, pr in enumerate(prs):
            tab = tabs[g][0, rows, :]
            swapped = pltpu.roll(tab, HEAD_DIM // 2, 1)
            lower = lax.broadcasted_iota(jnp.int32, tab.shape, 1) < HEAD_DIM // 2
            cos = jnp.where(lower, tab, swapped)
            sin = jnp.where(lower, -swapped, tab)

            def norm_rope(t, w):
                t = t * lax.rsqrt(jnp.mean(t * t, axis=-1, keepdims=True) + EPS) * w
                return t * cos + pltpu.roll(t, HEAD_DIM // 2, 1) * sin

            for hl in range(HEAD_PAIR):
                off = hl * head_cols
                slot = g * HEAD_PAIR + hl
                q_scr[slot, rows, :] = (norm_rope(pr[:, off:off + HEAD_DIM], qw_ref[...]) * ATTN_SCALE).astype(BF16)
                k_scr[slot, rows, :] = norm_rope(pr[:, off + HEAD_DIM:off + 2 * HEAD_DIM], kw_ref[...]).astype(BF16)
                v_scr[slot, rows, :] = pr[:, off + 2 * HEAD_DIM:off + 3 * HEAD_DIM].astype(BF16)

    def block_ids(c):
        return [(g, hl, chunk_blocks * c + u, u) for g in range(N_PATTERNS) for hl in range(HEAD_PAIR)
                for u in range(chunk_blocks)]

    def is_first(g, j, u):
        n_blk = seq // ATTN_PATTERNS[g][1] // wk
        if n_blk == 1:
            return True
        if u % 2 == 1:
            return False
        if isinstance(j, int):
            return j % n_blk == 0
        return (j % n_blk) == 0

    def block_start(j):
        start = j * wk
        return start if isinstance(start, int) else pl.multiple_of(start, wk)

    def scores(g, hl, j, u):
        first = is_first(g, j, u)
        slot = g * HEAD_PAIR + hl
        q = q_scr[slot, pl.ds(block_start(j), wk), :]
        if first is True:
            kk = k_scr[slot, pl.ds(block_start(j), wk), :]
            bias = bias_scr[1, :, wk:]
        else:
            kk = k_scr[slot, pl.ds(block_start(j - 1), 2 * wk), :]
            bias = bias_scr[0] if first is False else bias_scr[jnp.where(first, 1, 0)]
        return lax.dot_general(q, kk, (((1,), (1,)), ((), ())), preferred_element_type=F32) + bias

    def softmax(s):
        m = jnp.max(s, axis=-1, keepdims=True)
        p = jnp.exp(s - m)
        den = jnp.sum(p, axis=-1, keepdims=True)
        return p.astype(BF16), m, den

    def weighted(g, hl, j, u, p):
        slot = g * HEAD_PAIR + hl
        if is_first(g, j, u) is True:
            vv = v_scr[slot, pl.ds(block_start(j), wk), :]
        else:
            vv = v_scr[slot, pl.ds(block_start(j - 1), 2 * wk), :]
        return _dot(p, vv)

    def attn_store(g, hl, j, o, m, den):
        dil = ATTN_PATTERNS[g][1]
        n_blk = seq // dil // wk
        slot = g * HEAD_PAIR + hl
        if dil == 1:
            nat = pl.ds(block_start(j), wk)
        else:
            nat = pl.ds((j % n_blk) * (wk * dil) + j // n_blk, wk, stride=dil)
        on_scr[slot, nat, :] = o * (1.0 / den)
        lse_scr[slot, nat, :] = jnp.broadcast_to(m + jnp.log(den), (wk, HEAD_DIM))

    def step(c_attn, c_proj):
        ids = block_ids(c_attn) if c_attn is not None else []
        ss = [scores(g, hl, j, u) for g, hl, j, u in ids]
        prs = proj_dots(c_proj) if c_proj is not None else None
        sm = [softmax(s) for s in ss]
        os_ = [weighted(g, hl, j, u, p) for (g, hl, j, u), (p, _, _) in zip(ids, sm)]
        if prs is not None:
            proj_store(c_proj, prs)
        for (g, hl, j, u), o, (_, m, den) in zip(ids, os_, sm):
            attn_store(g, hl, j, o, m, den)

    step(None, 0)
    step(0, 1)

    def body(c, carry):
        step(c - 1, c)
        return carry

    lax.fori_loop(2, n_chunks, body, 0)
    step(n_chunks - 1, None)

    def merge_body(c, carry):
        rows = pl.ds(pl.multiple_of(c * PROJ_ROWS, PROJ_ROWS), PROJ_ROWS)
        for hl in range(HEAD_PAIR):
            ls = [lse_scr[g * HEAD_PAIR + hl, rows, :] for g in range(N_PATTERNS)]
            mx = jnp.maximum(jnp.maximum(ls[0], ls[1]), ls[2])
            num = jnp.zeros((PROJ_ROWS, HEAD_DIM), F32)
            den = jnp.zeros((PROJ_ROWS, HEAD_DIM), F32)
            for g in range(N_PATTERNS):
                wgt = jnp.exp(ls[g] - mx)
                num = num + wgt * on_scr[g * HEAD_PAIR + hl, rows, :]
                den = den + wgt
            o_ref[0, rows, hl * HEAD_DIM:(hl + 1) * HEAD_DIM] = num / den
        return carry

    lax.fori_loop(0, seq // PROJ_ROWS, merge_body, 0)


def _attention(hs, w_qkv, tables, q_norm_w, k_norm_w):
    bsz, seq, _ = hs[0].shape
    slots = N_PATTERNS * HEAD_PAIR
    stat = pltpu.VMEM((slots, seq, HEAD_DIM), F32)
    qkv = pltpu.VMEM((slots, seq, HEAD_DIM), BF16)
    once = pl.Buffered(1)
    hspec = pl.BlockSpec((1, seq, D_MODEL), lambda b, hp: (b, 0, 0), pipeline_mode=once)
    tspec = pl.BlockSpec((1, seq, HEAD_DIM), lambda b, hp: (b, 0, 0), pipeline_mode=once)
    vec = pl.BlockSpec((1, HEAD_DIM), lambda b, hp: (0, 0))
    return pl.pallas_call(
        functools.partial(_attn_kernel, seq=seq),
        grid=(bsz, HEADS // HEAD_PAIR),
        in_specs=[hspec, hspec, hspec,
                  pl.BlockSpec((1, D_MODEL, HEAD_PAIR * 3 * N_PATTERNS * HEAD_DIM), lambda b, hp: (hp, 0, 0)),
                  tspec, tspec, tspec, vec, vec],
        out_specs=pl.BlockSpec((1, seq, HEAD_PAIR * HEAD_DIM), lambda b, hp: (b, 0, hp)),
        out_shape=jax.ShapeDtypeStruct((bsz, seq, HEADS * HEAD_DIM), F32),
        scratch_shapes=[qkv, qkv, qkv, stat, stat,
                        pltpu.VMEM((2, WINDOW_KEYS, 2 * WINDOW_KEYS), F32)],
        compiler_params=_params(2, ATTN_VMEM_LIMIT),
        name="attention",
    )(*hs, w_qkv, *tables, q_norm_w.reshape(1, HEAD_DIM), k_norm_w.reshape(1, HEAD_DIM))


def _pair_major_qkv(w_in):
    w = w_in[:, :3 * ATTN_QKV].reshape(D_MODEL, 3, N_PATTERNS, HEADS // HEAD_PAIR, HEAD_PAIR, HEAD_DIM)
    w = w.transpose(3, 0, 2, 4, 1, 5)
    return w.reshape(HEADS // HEAD_PAIR, D_MODEL, N_PATTERNS * HEAD_PAIR * 3 * HEAD_DIM).astype(BF16)


def kernel(x, c, positions, mod_w, mod_b, norm_w, even_w_in, conv_dw_w, conv_dw_b, conv_ln_w, conv_ln_b, ssm_lam_re, ssm_lam_im, ssm_log_dt, ssm_b_re, ssm_b_im, ssm_c_re, ssm_c_im, ssm_d, ssm_glu_w, ssm_glu_b, even_w_out, attn_w_in, attn_q_norm_w, attn_k_norm_w, attn_w_out):
    bsz, seq, _ = x.shape
    assert seq == 2048 and x.shape[-1] == D_MODEL and bsz % SSM_BATCH == 0
    mod = _modulation(c, mod_w, mod_b)
    shift = [mod[l, :, None, :D_MODEL] for l in range(DEPTH)]
    scale = [mod[l, :, None, D_MODEL:2 * D_MODEL] for l in range(DEPTH)]
    gate = [mod[l, :, None, 2 * D_MODEL:] for l in range(DEPTH)]
    nw = [norm_w[l].reshape(1, D_MODEL) for l in range(DEPTH)]
    residue_major = lambda a, d: a.reshape(bsz, seq // d, d, *a.shape[2:]).swapaxes(1, 2).reshape(a.shape)
    dils = [d for _, d in ATTN_PATTERNS]
    table = _rope_table(positions)
    tables = [table if d == 1 else residue_major(table, d) for d in dils]

    h = _first_norm(x, nw[0], scale[0], shift[0])
    for layer in range(DEPTH):
        i = layer // 2
        nxt = (nw[layer + 1], scale[layer + 1], shift[layer + 1]) if layer + 1 < DEPTH else None
        if layer % 2 == 0:
            ya, u, bz = _even_in(h, even_w_in[i].astype(BF16), conv_dw_w[i], conv_dw_b[i],
                                 conv_ln_w[i], conv_ln_b[i])
            mats = _ssm_matrices(ssm_lam_re[i], ssm_lam_im[i], ssm_log_dt[i], ssm_b_re[i], ssm_b_im[i],
                                 ssm_c_re[i], ssm_c_im[i])
            ys = _ssm(u, mats, ssm_d[i])
            w_out = even_w_out[i].astype(BF16)
            x, h = _tail_call(_even_out_kernel, "even_out", x, [ya, ys, bz],
                              [ssm_glu_w[i].astype(BF16), ssm_glu_b[i].reshape(1, SSM_WIDTH),
                               w_out[:CONV_WIDTH], w_out[CONV_WIDTH:]], gate[layer], nxt)
        else:
            hs = [h if d == 1 else residue_major(h, d) for d in dils]
            o = _attention(hs, _pair_major_qkv(attn_w_in[i]), tables, attn_q_norm_w[i], attn_k_norm_w[i])
            x, h = _tail_call(_odd_out_kernel, "odd_out", x, [h, o],
                              [attn_w_in[i][:, 3 * ATTN_QKV:].astype(BF16), attn_w_out[i].astype(BF16)],
                              gate[layer], nxt)
    return x
```

```python
import functools
import math

import jax
import jax.numpy as jnp
from jax import lax
from jax.experimental import pallas as pl
from jax.experimental.pallas import tpu as pltpu

F32 = jnp.float32
BF16 = jnp.bfloat16

D_MODEL = 1024
DEPTH = 4
CONV_WIDTH = 512
CONV_KERNEL = 31
SSM_WIDTH = 512
SSM_GROUP = 16
SSM_GROUPS = 32
SSM_STATE = 64
HEAD_DIM = 128
ATTN_PATTERNS = ((128, 1), (512, 4), (2048, 16))
N_PATTERNS = 3
HEADS = 8
ATTN_QKV = N_PATTERNS * HEADS * HEAD_DIM
ATTN_SCALE = HEAD_DIM ** -0.5
ROPE_THETA = 10000.0
EPS = 1e-6
NEG_INF = -1e30

LANES = 128
WINDOW_KEYS = 128
SSM_CHUNK = 16
SSM_LANE_GROUPS = LANES // SSM_GROUP
SSM_BLOCKS = SSM_WIDTH // LANES
SSM_STATE_COLS = SSM_LANE_GROUPS * SSM_STATE
SSM_BATCH = 4
ROW_TILE = 512
CONV_HALO = 32
CONV_ROWS = 64
PROJ_ROWS = 256
VMEM_LIMIT = 56 * 1024 * 1024
ATTN_VMEM_LIMIT = 58 * 1024 * 1024
HEAD_PAIR = 2


def _params(n_axes, vmem=VMEM_LIMIT):
    return pltpu.CompilerParams(dimension_semantics=("arbitrary",) * n_axes, vmem_limit_bytes=vmem)


def _adaln(x, nw, scale, shift):
    ms = jnp.mean(x * x, axis=-1, keepdims=True)
    return (x * lax.rsqrt(ms + EPS) * nw) * (1.0 + scale) + shift


def _dot(a, b):
    return jnp.dot(a, b, preferred_element_type=F32)


def _mod_kernel(c_ref, w_ref, b_ref, o_ref):
    o_ref[0] = jnp.dot(c_ref[...], w_ref[0], preferred_element_type=F32,
                       precision=lax.Precision.HIGHEST) + b_ref[0]


def _modulation(c, mod_w, mod_b):
    bsz = c.shape[0]
    nblk = 3 * D_MODEL // D_MODEL
    return pl.pallas_call(
        _mod_kernel,
        grid=(DEPTH, nblk),
        in_specs=[pl.BlockSpec((bsz, D_MODEL), lambda l, j: (0, 0)),
                  pl.BlockSpec((1, D_MODEL, D_MODEL), lambda l, j: (l, 0, j)),
                  pl.BlockSpec((1, 1, D_MODEL), lambda l, j: (l, 0, j))],
        out_specs=pl.BlockSpec((1, bsz, D_MODEL), lambda l, j: (l, 0, j)),
        out_shape=jax.ShapeDtypeStruct((DEPTH, bsz, 3 * D_MODEL), F32),
        compiler_params=_params(2),
        name="modulation",
    )(c, mod_w, mod_b.reshape(DEPTH, 1, 3 * D_MODEL))


def _rope_kernel(pos_ref, inv_ref, tab_ref):
    ang = pos_ref[0].astype(F32) * inv_ref[...]
    lane = lax.broadcasted_iota(jnp.int32, ang.shape, 1)
    tab_ref[0] = jnp.where(lane < HEAD_DIM // 2, jnp.cos(ang), jnp.sin(ang))


def _rope_table(positions):
    bsz, seq = positions.shape
    inv = ROPE_THETA ** (-jnp.arange(0, HEAD_DIM, 2, dtype=F32) / HEAD_DIM)
    inv2 = jnp.concatenate([inv, inv]).reshape(1, HEAD_DIM)
    return pl.pallas_call(
        _rope_kernel,
        grid=(bsz, seq // ROW_TILE),
        in_specs=[pl.BlockSpec((1, ROW_TILE, 1), lambda b, i: (b, i, 0)),
                  pl.BlockSpec((1, HEAD_DIM), lambda b, i: (0, 0))],
        out_specs=pl.BlockSpec((1, ROW_TILE, HEAD_DIM), lambda b, i: (b, i, 0)),
        out_shape=jax.ShapeDtypeStruct((bsz, seq, HEAD_DIM), F32),
        compiler_params=_params(2),
        name="rope_table",
    )(positions.reshape(bsz, seq, 1), inv2)


def _norm_kernel(x_ref, nw_ref, sc_ref, sh_ref, h_ref):
    h_ref[0] = _adaln(x_ref[0], nw_ref[...], sc_ref[0], sh_ref[0]).astype(BF16)


def _first_norm(x, nw, scale, shift):
    bsz, seq, _ = x.shape
    vec = pl.BlockSpec((1, 1, D_MODEL), lambda b, i: (b, 0, 0))
    row = pl.BlockSpec((1, ROW_TILE, D_MODEL), lambda b, i: (b, i, 0))
    return pl.pallas_call(
        _norm_kernel,
        grid=(bsz, seq // ROW_TILE),
        in_specs=[row, pl.BlockSpec((1, D_MODEL), lambda b, i: (0, 0)), vec, vec],
        out_specs=row,
        out_shape=jax.ShapeDtypeStruct(x.shape, BF16),
        compiler_params=_params(2),
        name="first_norm",
    )(x, nw, scale, shift)


def _even_in_kernel(h_ref, w_ref, dww_ref, dwb_ref, lnw_ref, lnb_ref,
                    ya_ref, u_ref, bz_ref, conv_scr, y_scr, az_scr):
    i = pl.program_id(1)
    h = h_ref[0]
    cw = CONV_WIDTH
    slabs = cw // LANES

    @pl.when(i == 0)
    def _():
        conv_scr[:, 0:CONV_HALO, :] = jnp.zeros((slabs, CONV_HALO, LANES), F32)

    a1 = _dot(h, w_ref[:, 0:cw])
    a2 = _dot(h, w_ref[:, cw:2 * cw])
    a = a1 * jax.nn.sigmoid(a2)
    for l in range(slabs):
        conv_scr[l, CONV_HALO:CONV_HALO + ROW_TILE, :] = a[:, l * LANES:(l + 1) * LANES]
    az_scr[...] = jax.nn.silu(_dot(h, w_ref[:, 2 * cw:3 * cw]))
    u_ref[0] = _dot(h, w_ref[:, 3 * cw:4 * cw])
    bz_ref[0] = _dot(h, w_ref[:, 4 * cw:5 * cw])

    first_tap = CONV_HALO - (CONV_KERNEL - 1)
    half = CONV_ROWS // 2

    def conv_block(r, carry):
        r0 = r * CONV_ROWS
        for l in range(slabs):
            lanes = slice(l * LANES, (l + 1) * LANES)
            for par in range(2):
                acc = jnp.broadcast_to(dwb_ref[:, lanes], (half, LANES))
                for k in range(CONV_KERNEL):
                    tap = conv_scr[l, pl.ds(r0 + first_tap + par + k, half, stride=2), :]
                    acc = acc + dww_ref[k:k + 1, lanes] * tap
                y_scr[l, pl.ds(r0 + par, half, stride=2), :] = acc
        return carry

    lax.fori_loop(0, ROW_TILE // CONV_ROWS, conv_block, 0)
    for l in range(slabs):
        conv_scr[l, 0:CONV_HALO, :] = conv_scr[l, ROW_TILE:ROW_TILE + CONV_HALO, :]

    def norm_block(r, carry):
        rows = pl.ds(pl.multiple_of(r * CONV_ROWS, CONV_ROWS), CONV_ROWS)
        acc = jnp.concatenate([y_scr[l, rows, :] for l in range(slabs)], axis=-1)
        mu = jnp.mean(acc, axis=-1, keepdims=True)
        xc = acc - mu
        y = xc * lax.rsqrt(jnp.mean(xc * xc, axis=-1, keepdims=True) + EPS)
        y = y * lnw_ref[...] + lnb_ref[...]
        ya_ref[0, rows, :] = (jax.nn.silu(y) * az_scr[rows, :]).astype(BF16)
        return carry

    lax.fori_loop(0, ROW_TILE // CONV_ROWS, norm_block, 0, unroll=4)


def _even_in(h, w_in, dw_w, dw_b, ln_w, ln_b):
    bsz, seq, _ = h.shape
    cw = CONV_WIDTH
    const = lambda shape: pl.BlockSpec(shape, lambda b, i: (0,) * len(shape))
    row = lambda width: pl.BlockSpec((1, ROW_TILE, width), lambda b, i: (b, i, 0))
    return pl.pallas_call(
        _even_in_kernel,
        grid=(bsz, seq // ROW_TILE),
        in_specs=[row(D_MODEL), const(w_in.shape), const((CONV_KERNEL, cw)),
                  const((1, cw)), const((1, cw)), const((1, cw))],
        out_specs=[row(cw), row(cw), row(cw)],
        out_shape=[jax.ShapeDtypeStruct((bsz, seq, cw), BF16),
                   jax.ShapeDtypeStruct((bsz, seq, cw), F32),
                   jax.ShapeDtypeStruct((bsz, seq, cw), F32)],
        scratch_shapes=[pltpu.VMEM((cw // LANES, CONV_HALO + ROW_TILE, LANES), F32),
                        pltpu.VMEM((cw // LANES, ROW_TILE, LANES), F32),
                        pltpu.VMEM((ROW_TILE, cw), F32)],
        compiler_params=_params(2),
        name="even_in",
    )(h, w_in, dw_w, dw_b.reshape(1, cw), ln_w.reshape(1, cw), ln_b.reshape(1, cw))


def _ssm_matrices(lam_re, lam_im, log_dt, b_re, b_im, c_re, c_im):
    hp = lax.Precision.HIGHEST
    t = SSM_CHUNK
    lr, li = lam_re.astype(F32), lam_im.astype(F32)
    dt = jnp.exp(log_dt.astype(F32))[:, None]

    def a_pow(k):
        kk = k.astype(F32)[:, None, None]
        mag = jnp.exp(kk * (lr * dt)[None])
        ang = kk * (li * dt)[None]
        return mag * jnp.cos(ang), mag * jnp.sin(ang)

    ar, ai = a_pow(jnp.ones((1,), F32))
    ar, ai = ar[0], ai[0]
    den = lr * lr + li * li
    nr = ar - 1.0
    kr = (nr * lr + ai * li) / den
    ki = (ai * lr - nr * li) / den
    br, bi = b_re.astype(F32), b_im.astype(F32)
    bbr = kr[..., None] * br - ki[..., None] * bi
    bbi = kr[..., None] * bi + ki[..., None] * br
    cr, ci = c_re.astype(F32), c_im.astype(F32)

    pr, pi = a_pow(jnp.arange(t + 1))
    wr = pr[:t, :, :, None] * bbr[None] - pi[:t, :, :, None] * bbi[None]
    wi = pr[:t, :, :, None] * bbi[None] + pi[:t, :, :, None] * bbr[None]
    kk = (jnp.einsum('gop,kgpi->kgoi', cr, wr, precision=hp)
          - jnp.einsum('gop,kgpi->kgoi', ci, wi, precision=hp))

    nb, gl, hh, pp = SSM_BLOCKS, SSM_LANE_GROUPS, SSM_GROUP, SSM_STATE
    grp = jnp.arange(gl)[:, None, None]
    row = jnp.arange(t * hh)[None, :, None]
    col = jnp.arange(t * LANES)[None, None, :]
    place = ((row // hh == col // LANES) & (row % hh == col % hh) & ((col % LANES) // hh == grp)).astype(BF16)
    prow = jnp.arange(pp)[None, :, None]
    pcol = jnp.arange(gl * pp)[None, None, :]
    spread = ((pcol % pp == prow) & (pcol // pp == grp)).astype(BF16)

    def rows_sgi(m):
        n = m.shape[-1]
        return m.reshape(nb, gl, t, hh, n).transpose(0, 2, 1, 3, 4).reshape(nb, t * LANES, n)

    lrow = jnp.arange(t * hh)[None, :, None]
    lcol = jnp.arange(t * hh)[None, None, :]
    s_idx = jnp.arange(t)[:, None, None]
    shift = ((lcol // hh == lrow // hh + s_idx) & (lcol % hh == lrow % hh)).astype(BF16)
    kk_i = kk.reshape(t, nb, gl, hh, hh).transpose(1, 2, 4, 0, 3).reshape(nb, gl, hh, t * hh)
    toe = jnp.einsum('cgil,sln->cgsin', kk_i.astype(BF16), shift,
                     preferred_element_type=BF16).reshape(nb, gl, t * hh, t * hh)
    m_intra = rows_sgi(jnp.einsum('cgab,gbn->cgan', toe, place, preferred_element_type=BF16))

    def state_cols(w):
        w = w[::-1].reshape(t, nb, gl, pp, hh).transpose(1, 2, 0, 4, 3).reshape(nb, gl, t * hh, pp)
        return rows_sgi(jnp.einsum('cgap,gpn->cgan', w.astype(BF16), spread, preferred_element_type=BF16))

    m_state = jnp.concatenate([state_cols(wr), state_cols(wi)], axis=-1)

    def in_rows(q):
        q = q.reshape(t, nb, gl, hh, pp).transpose(1, 2, 4, 0, 3).reshape(nb, gl, pp, t * hh)
        return jnp.einsum('cgpb,gbn->cgpn', q.astype(BF16), place,
                          preferred_element_type=BF16).reshape(nb, gl * pp, t * LANES)

    m_in_r = in_rows(pr[1:, :, None, :] * cr[None] - pi[1:, :, None, :] * ci[None])
    m_in_i = in_rows(-(pr[1:, :, None, :] * ci[None] + pi[1:, :, None, :] * cr[None]))

    n_lvl = int(math.log2(2048 // t))
    sr2, si2 = a_pow(t * (2 ** jnp.arange(n_lvl)))
    ap_r = sr2.reshape(n_lvl, nb, gl * pp).transpose(1, 0, 2)
    ap_i = si2.reshape(n_lvl, nb, gl * pp).transpose(1, 0, 2)
    return m_intra, m_state, m_in_r, m_in_i, ap_r, ap_i


def _ssm_kernel(u_ref, mintra_ref, mstate_ref, minr_ref, mini_ref, apr_ref, api_ref, d_ref, y_ref,
                x_scr, yi_scr, cr_scr, ci_scr, zr_scr, zi_scr, *, n_chunks, n_levels):
    t = SSM_CHUNK
    sc = SSM_STATE_COLS
    pair = 2 * LANES
    for bb in range(SSM_BATCH):
        for t0 in range(t):
            x_scr[bb * n_chunks:(bb + 1) * n_chunks, t0 * LANES:(t0 + 1) * LANES] = (
                u_ref[bb, pl.ds(t0, n_chunks, stride=t), :].astype(BF16))
    s_loc = _dot(x_scr[...], mstate_ref[0])
    for tp in range(t // 2):
        cols = slice(tp * pair, (tp + 1) * pair)
        yi_scr[:, cols] = _dot(x_scr[:, 0:(tp + 1) * pair], mintra_ref[0, 0:(tp + 1) * pair, cols])

    zero = jnp.zeros((n_chunks, sc), F32)
    for p in range(2):
        zr_scr[p, 0:n_chunks, :] = zero
        zi_scr[p, 0:n_chunks, :] = zero
    for bb in range(SSM_BATCH):
        rows = slice(bb * n_chunks, (bb + 1) * n_chunks)
        zr_scr[0, n_chunks:, :] = s_loc[rows, :sc]
        zi_scr[0, n_chunks:, :] = s_loc[rows, sc:]
        for k in range(n_levels):
            src, dst = k % 2, 1 - (k % 2)
            sh = n_chunks - (1 << k)
            zr = zr_scr[src, n_chunks:, :]
            zi = zi_scr[src, n_chunks:, :]
            pr = zr_scr[src, sh:sh + n_chunks, :]
            pi = zi_scr[src, sh:sh + n_chunks, :]
            ar = apr_ref[0, k:k + 1, :]
            ai = api_ref[0, k:k + 1, :]
            zr_scr[dst, n_chunks:, :] = zr + ar * pr - ai * pi
            zi_scr[dst, n_chunks:, :] = zi + ar * pi + ai * pr
        fin = n_levels % 2
        cr_scr[rows, :] = zr_scr[fin, n_chunks - 1:2 * n_chunks - 1, :].astype(BF16)
        ci_scr[rows, :] = zi_scr[fin, n_chunks - 1:2 * n_chunks - 1, :].astype(BF16)

    carry_r = cr_scr[...]
    carry_i = ci_scr[...]
    for tp in range(t // 2):
        cols = slice(tp * pair, (tp + 1) * pair)
        yc = yi_scr[:, cols] + _dot(carry_r, minr_ref[0, :, cols]) + _dot(carry_i, mini_ref[0, :, cols])
        for bb in range(SSM_BATCH):
            for j in range(2):
                y_ref[bb, pl.ds(2 * tp + j, n_chunks, stride=t), :] = (
                    yc[bb * n_chunks:(bb + 1) * n_chunks, j * LANES:(j + 1) * LANES])
    for bb in range(SSM_BATCH):
        y_ref[bb] = y_ref[bb] + d_ref[0] * u_ref[bb]


def _ssm(u, mats, d_skip):
    m_intra, m_state, m_in_r, m_in_i, ap_r, ap_i = mats
    bsz, seq, _ = u.shape
    n_chunks = seq // SSM_CHUNK
    n_levels = ap_r.shape[1]
    nb = SSM_BATCH
    once = pl.Buffered(1)
    blk = pl.BlockSpec((nb, seq, LANES), lambda c, b: (b, 0, c))
    mat = lambda m: pl.BlockSpec((1,) + m.shape[1:], lambda c, b: (c, 0, 0), pipeline_mode=once)
    lvl = pl.BlockSpec((1, n_levels, SSM_STATE_COLS), lambda c, b: (c, 0, 0))
    return pl.pallas_call(
        functools.partial(_ssm_kernel, n_chunks=n_chunks, n_levels=n_levels),
        grid=(SSM_BLOCKS, bsz // nb),
        in_specs=[blk, mat(m_intra), mat(m_state), mat(m_in_r), mat(m_in_i), lvl, lvl,
                  pl.BlockSpec((1, 1, LANES), lambda c, b: (c, 0, 0))],
        out_specs=blk,
        out_shape=jax.ShapeDtypeStruct(u.shape, F32),
        scratch_shapes=[pltpu.VMEM((nb * n_chunks, SSM_CHUNK * LANES), BF16),
                        pltpu.VMEM((nb * n_chunks, SSM_CHUNK * LANES), F32),
                        pltpu.VMEM((nb * n_chunks, SSM_STATE_COLS), BF16),
                        pltpu.VMEM((nb * n_chunks, SSM_STATE_COLS), BF16),
                        pltpu.VMEM((2, 2 * n_chunks, SSM_STATE_COLS), F32),
                        pltpu.VMEM((2, 2 * n_chunks, SSM_STATE_COLS), F32)],
        compiler_params=_params(2),
        name="ssm",
    )(u, m_intra, m_state, m_in_r, m_in_i, ap_r, ap_i, d_skip.reshape(SSM_BLOCKS, 1, LANES))


def _finish(x_ref, gate_ref, out, nxt, xo_ref, h_outs, stage_scr, stage2_scr):
    xn = x_ref[0] + gate_ref[0] * out
    xo_ref[0] = xn
    if nxt is None:
        return
    nw_ref, sc_ref, sh_ref = nxt
    hn = _adaln(xn, nw_ref[...], sc_ref[0], sh_ref[0])
    h_outs[0][0] = hn.astype(BF16)
    if len(h_outs) == 1:
        return
    slabs = D_MODEL // LANES
    d1, d2 = ATTN_PATTERNS[1][1], ATTN_PATTERNS[2][1]
    assert d2 == d1 * d1 and len(h_outs) == 3
    n1, n2 = ROW_TILE // d1, ROW_TILE // d2
    for l in range(slabs):
        stage_scr[l] = hn[:, l * LANES:(l + 1) * LANES]
    for r in range(d1):
        parts = [stage_scr[l, pl.ds(r, n1, stride=d1), :] for l in range(slabs)]
        h_outs[1][0, r] = jnp.concatenate(parts, axis=-1).astype(BF16)
        for l in range(slabs):
            stage2_scr[r * slabs + l] = parts[l]
    for r in range(d1):
        for a in range(d1):
            parts = [stage2_scr[r * slabs + l, pl.ds(a, n2, stride=d1), :] for l in range(slabs)]
            h_outs[2][0, a * d1 + r] = jnp.concatenate(parts, axis=-1).astype(BF16)


def _even_out_kernel(*refs, n_h):
    x_ref, ya_ref, ys_ref, bz_ref, gw_ref, gb_ref, wa_ref, ws_ref, gate_ref = refs[:9]
    nxt, xo_ref, h_outs, stages = _tail_refs(refs[9:], n_h)
    y = ys_ref[0]
    y = 0.5 * y * (1.0 + lax.erf(y * (2.0 ** -0.5)))
    y = y * jax.nn.sigmoid(_dot(y.astype(BF16), gw_ref[...]) + gb_ref[...])
    y = y * jax.nn.silu(bz_ref[0])
    out = _dot(ya_ref[0], wa_ref[...]) + _dot(y.astype(BF16), ws_ref[...])
    _finish(x_ref, gate_ref, out, nxt, xo_ref, h_outs, *stages)


def _odd_out_kernel(*refs, n_h):
    x_ref, h_ref, o_ref, wz_ref, wo_ref, gate_ref = refs[:6]
    nxt, xo_ref, h_outs, stages = _tail_refs(refs[6:], n_h)
    z = _dot(h_ref[0], wz_ref[...])
    g = o_ref[0] * jax.nn.silu(z)
    out = _dot(g.astype(BF16), wo_ref[...])
    _finish(x_ref, gate_ref, out, nxt, xo_ref, h_outs, *stages)


def _tail_refs(refs, n_h):
    if n_h == 0:
        return None, refs[0], (), (None, None)
    nxt, xo_ref, h_outs = refs[:3], refs[3], refs[4:4 + n_h]
    return nxt, xo_ref, h_outs, (tuple(refs[4 + n_h:6 + n_h]) if n_h > 1 else (None, None))


def _tail_call(body, name, x, rows, consts, gate, nxt, permuted):
    bsz, seq, _ = x.shape
    n_h = 0 if nxt is None else (N_PATTERNS if permuted else 1)
    row = lambda a: pl.BlockSpec((1, ROW_TILE, a.shape[-1]), lambda b, i: (b, i, 0))
    const = lambda a: pl.BlockSpec(a.shape, lambda b, i: (0,) * a.ndim)
    vec = pl.BlockSpec((1, 1, D_MODEL), lambda b, i: (b, 0, 0))
    args = [x, *rows, *consts, gate]
    in_specs = [row(x)] + [row(a) for a in rows] + [const(a) for a in consts] + [vec]
    out_specs = [row(x)]
    out_shape = [jax.ShapeDtypeStruct(x.shape, F32)]
    scratch = []
    if n_h:
        nw, sc, sh = nxt
        args += [nw, sc, sh]
        in_specs += [const(nw), vec, vec]
        out_specs.append(row(x))
        out_shape.append(jax.ShapeDtypeStruct(x.shape, BF16))
    if n_h > 1:
        for _, dil in ATTN_PATTERNS[1:]:
            out_specs.append(pl.BlockSpec((1, dil, ROW_TILE // dil, D_MODEL), lambda b, i: (b, 0, i, 0)))
            out_shape.append(jax.ShapeDtypeStruct((bsz, dil, seq // dil, D_MODEL), BF16))
        d1 = ATTN_PATTERNS[1][1]
        scratch.append(pltpu.VMEM((D_MODEL // LANES, ROW_TILE, LANES), F32))
        scratch.append(pltpu.VMEM((d1 * D_MODEL // LANES, ROW_TILE // d1, LANES), F32))
    res = pl.pallas_call(
        functools.partial(body, n_h=n_h),
        grid=(bsz, seq // ROW_TILE),
        in_specs=in_specs, out_specs=out_specs, out_shape=out_shape, scratch_shapes=scratch,
        compiler_params=_params(2),
        name=name,
    )(*args)
    return res[0], [h.reshape(x.shape) for h in res[1:]]


def _attn_kernel(h0_ref, h1_ref, h2_ref, w_ref, t0_ref, t1_ref, t2_ref,
                 qw_ref, kw_ref, o_ref, q_scr, k_scr, v_scr, on_scr, lse_scr, bias_scr, *, seq):
    wk = WINDOW_KEYS
    n_chunks = seq // PROJ_ROWS
    chunk_blocks = PROJ_ROWS // wk
    h_refs = (h0_ref, h1_ref, h2_ref)
    tabs = (t0_ref, t1_ref, t2_ref)
    head_cols = 3 * HEAD_DIM

    qi = lax.broadcasted_iota(jnp.int32, (wk, 2 * wk), 0)
    kj = lax.broadcasted_iota(jnp.int32, (wk, 2 * wk), 1)
    band = (kj >= qi) & (kj <= qi + wk)
    bias_scr[0] = jnp.where(band, 0.0, NEG_INF)
    bias_scr[1] = jnp.where(band & (kj >= wk), 0.0, NEG_INF)

    def chunk_rows(c):
        start = c * PROJ_ROWS
        if not isinstance(c, int):
            start = pl.multiple_of(start, PROJ_ROWS)
        return pl.ds(start, PROJ_ROWS)

    def proj_dots(c):
        rows = chunk_rows(c)
        width = HEAD_PAIR * head_cols
        return [_dot(h_refs[g][0, rows, :], w_ref[0, :, g * width:(g + 1) * width]) for g in range(N_PATTERNS)]

    def proj_store(c, prs):
        rows = chunk_rows(c)
        for g, pr in enumerate(prs):
            tab = tabs[g][0, rows, :]
            swapped = pltpu.roll(tab, HEAD_DIM // 2, 1)
            lower = lax.broadcasted_iota(jnp.int32, tab.shape, 1) < HEAD_DIM // 2
            cos = jnp.where(lower, tab, swapped)
            sin = jnp.where(lower, -swapped, tab)

            def norm_rope(t, w):
                t = t * lax.rsqrt(jnp.mean(t * t, axis=-1, keepdims=True) + EPS) * w
                return t * cos + pltpu.roll(t, HEAD_DIM // 2, 1) * sin

            for hl in range(HEAD_PAIR):
                off = hl * head_cols
                slot = g * HEAD_PAIR + hl
                q_scr[slot, rows, :] = (norm_rope(pr[:, off:off + HEAD_DIM], qw_ref[...]) * ATTN_SCALE).astype(BF16)
                k_scr[slot, rows, :] = norm_rope(pr[:, off + HEAD_DIM:off + 2 * HEAD_DIM], kw_ref[...]).astype(BF16)
                v_scr[slot, rows, :] = pr[:, off + 2 * HEAD_DIM:off + 3 * HEAD_DIM].astype(BF16)

    def block_ids(c):
        return [(g, hl, chunk_blocks * c + u, u) for g in range(N_PATTERNS) for hl in range(HEAD_PAIR)
                for u in range(chunk_blocks)]

    def is_first(g, j, u):
        n_blk = seq // ATTN_PATTERNS[g][1] // wk
        if n_blk == 1:
            return True
        if u % 2 == 1:
            return False
        if isinstance(j, int):
            return j % n_blk == 0
        return (j % n_blk) == 0

    def block_start(j):
        start = j * wk
        return start if isinstance(start, int) else pl.multiple_of(start, wk)

    def scores(g, hl, j, u):
        first = is_first(g, j, u)
        slot = g * HEAD_PAIR + hl
        q = q_scr[slot, pl.ds(block_start(j), wk), :]
        if first is True:
            kk = k_scr[slot, pl.ds(block_start(j), wk), :]
            bias = bias_scr[1, :, wk:]
        else:
            kk = k_scr[slot, pl.ds(block_start(j - 1), 2 * wk), :]
            bias = bias_scr[0] if first is False else bias_scr[jnp.where(first, 1, 0)]
        return lax.dot_general(q, kk, (((1,), (1,)), ((), ())), preferred_element_type=F32) + bias

    def softmax(s):
        m = jnp.max(s, axis=-1, keepdims=True)
        p = jnp.exp(s - m)
        den = jnp.sum(p, axis=-1, keepdims=True)
        return p.astype(BF16), m, den

    def weighted(g, hl, j, u, p):
        slot = g * HEAD_PAIR + hl
        if is_first(g, j, u) is True:
            vv = v_scr[slot, pl.ds(block_start(j), wk), :]
        else:
            vv = v_scr[slot, pl.ds(block_start(j - 1), 2 * wk), :]
        return _dot(p, vv)

    def attn_store(g, hl, j, o, m, den):
        dil = ATTN_PATTERNS[g][1]
        n_blk = seq // dil // wk
        slot = g * HEAD_PAIR + hl
        if dil == 1:
            nat = pl.ds(block_start(j), wk)
        else:
            nat = pl.ds((j % n_blk) * (wk * dil) + j // n_blk, wk, stride=dil)
        on_scr[slot, nat, :] = o * (1.0 / den)
        lse_scr[slot, nat, :] = jnp.broadcast_to(m + jnp.log(den), (wk, HEAD_DIM))

    def step(c_attn, c_proj):
        ids = block_ids(c_attn) if c_attn is not None else []
        ss = [scores(g, hl, j, u) for g, hl, j, u in ids]
        prs = proj_dots(c_proj) if c_proj is not None else None
        sm = [softmax(s) for s in ss]
        os_ = [weighted(g, hl, j, u, p) for (g, hl, j, u), (p, _, _) in zip(ids, sm)]
        if prs is not None:
            proj_store(c_proj, prs)
        for (g, hl, j, u), o, (_, m, den) in zip(ids, os_, sm):
            attn_store(g, hl, j, o, m, den)

    step(None, 0)
    step(0, 1)

    def body(c, carry):
        step(c - 1, c)
        return carry

    lax.fori_loop(2, n_chunks, body, 0)
    step(n_chunks - 1, None)

    def merge_body(c, carry):
        rows = pl.ds(pl.multiple_of(c * PROJ_ROWS, PROJ_ROWS), PROJ_ROWS)
        for hl in range(HEAD_PAIR):
            ls = [lse_scr[g * HEAD_PAIR + hl, rows, :] for g in range(N_PATTERNS)]
            mx = jnp.maximum(jnp.maximum(ls[0], ls[1]), ls[2])
            num = jnp.zeros((PROJ_ROWS, HEAD_DIM), F32)
            den = jnp.zeros((PROJ_ROWS, HEAD_DIM), F32)
            for g in range(N_PATTERNS):
                wgt = jnp.exp(ls[g] - mx)
                num = num + wgt * on_scr[g * HEAD_PAIR + hl, rows, :]
                den = den + wgt
            o_ref[0, rows, hl * HEAD_DIM:(hl + 1) * HEAD_DIM] = (num / den).astype(BF16)
        return carry

    lax.fori_loop(0, seq // PROJ_ROWS, merge_body, 0)


def _attention(hs, w_qkv, tables, q_norm_w, k_norm_w):
    bsz, seq, _ = hs[0].shape
    slots = N_PATTERNS * HEAD_PAIR
    stat = pltpu.VMEM((slots, seq, HEAD_DIM), F32)
    qkv = pltpu.VMEM((slots, seq, HEAD_DIM), BF16)
    once = pl.Buffered(1)
    hspec = pl.BlockSpec((1, seq, D_MODEL), lambda b, hp: (b, 0, 0), pipeline_mode=once)
    tspec = pl.BlockSpec((1, seq, HEAD_DIM), lambda b, hp: (b, 0, 0), pipeline_mode=once)
    vec = pl.BlockSpec((1, HEAD_DIM), lambda b, hp: (0, 0))
    return pl.pallas_call(
        functools.partial(_attn_kernel, seq=seq),
        grid=(bsz, HEADS // HEAD_PAIR),
        in_specs=[hspec, hspec, hspec,
                  pl.BlockSpec((1, D_MODEL, HEAD_PAIR * 3 * N_PATTERNS * HEAD_DIM), lambda b, hp: (hp, 0, 0)),
                  tspec, tspec, tspec, vec, vec],
        out_specs=pl.BlockSpec((1, seq, HEAD_PAIR * HEAD_DIM), lambda b, hp: (b, 0, hp)),
        out_shape=jax.ShapeDtypeStruct((bsz, seq, HEADS * HEAD_DIM), BF16),
        scratch_shapes=[qkv, qkv, qkv, stat, stat,
                        pltpu.VMEM((2, WINDOW_KEYS, 2 * WINDOW_KEYS), F32)],
        compiler_params=_params(2, ATTN_VMEM_LIMIT),
        name="attention",
    )(*hs, w_qkv, *tables, q_norm_w.reshape(1, HEAD_DIM), k_norm_w.reshape(1, HEAD_DIM))


def _pair_major_qkv(w_in):
    w = w_in[:, :3 * ATTN_QKV].reshape(D_MODEL, 3, N_PATTERNS, HEADS // HEAD_PAIR, HEAD_PAIR, HEAD_DIM)
    w = w.transpose(3, 0, 2, 4, 1, 5)
    return w.reshape(HEADS // HEAD_PAIR, D_MODEL, N_PATTERNS * HEAD_PAIR * 3 * HEAD_DIM).astype(BF16)


def kernel(x, c, positions, mod_w, mod_b, norm_w, even_w_in, conv_dw_w, conv_dw_b, conv_ln_w, conv_ln_b, ssm_lam_re, ssm_lam_im, ssm_log_dt, ssm_b_re, ssm_b_im, ssm_c_re, ssm_c_im, ssm_d, ssm_glu_w, ssm_glu_b, even_w_out, attn_w_in, attn_q_norm_w, attn_k_norm_w, attn_w_out):
    bsz, seq, _ = x.shape
    assert seq == 2048 and x.shape[-1] == D_MODEL and bsz % SSM_BATCH == 0
    mod = _modulation(c, mod_w, mod_b)
    shift = [mod[l, :, None, :D_MODEL] for l in range(DEPTH)]
    scale = [mod[l, :, None, D_MODEL:2 * D_MODEL] for l in range(DEPTH)]
    gate = [mod[l, :, None, 2 * D_MODEL:] for l in range(DEPTH)]
    nw = [norm_w[l].reshape(1, D_MODEL) for l in range(DEPTH)]
    residue_major = lambda a, d: a.reshape(bsz, seq // d, d, *a.shape[2:]).swapaxes(1, 2).reshape(a.shape)
    dils = [d for _, d in ATTN_PATTERNS]
    assert DEPTH % 2 == 0
    table = _rope_table(positions)
    tables = [table if d == 1 else residue_major(table, d) for d in dils]

    hs = [_first_norm(x, nw[0], scale[0], shift[0])]
    for layer in range(DEPTH):
        i = layer // 2
        nxt = (nw[layer + 1], scale[layer + 1], shift[layer + 1]) if layer + 1 < DEPTH else None
        if layer % 2 == 0:
            ya, u, bz = _even_in(hs[0], even_w_in[i].astype(BF16), conv_dw_w[i], conv_dw_b[i],
                                 conv_ln_w[i], conv_ln_b[i])
            mats = _ssm_matrices(ssm_lam_re[i], ssm_lam_im[i], ssm_log_dt[i], ssm_b_re[i], ssm_b_im[i],
                                 ssm_c_re[i], ssm_c_im[i])
            ys = _ssm(u, mats, ssm_d[i])
            w_out = even_w_out[i].astype(BF16)
            x, hs = _tail_call(_even_out_kernel, "even_out", x, [ya, ys, bz],
                               [ssm_glu_w[i].astype(BF16), ssm_glu_b[i].reshape(1, SSM_WIDTH),
                                w_out[:CONV_WIDTH], w_out[CONV_WIDTH:]], gate[layer], nxt, permuted=True)
        else:
            o = _attention(hs, _pair_major_qkv(attn_w_in[i]), tables, attn_q_norm_w[i], attn_k_norm_w[i])
            x, hs = _tail_call(_odd_out_kernel, "odd_out", x, [hs[0], o],
                               [attn_w_in[i][:, 3 * ATTN_QKV:].astype(BF16), attn_w_out[i].astype(BF16)],
                               gate[layer], nxt, permuted=False)
    return x
```

```python
import functools
import math

import jax
import jax.numpy as jnp
from jax import lax
from jax.experimental import pallas as pl
from jax.experimental.pallas import tpu as pltpu

F32 = jnp.float32
BF16 = jnp.bfloat16

D_MODEL = 1024
DEPTH = 4
CONV_WIDTH = 512
CONV_KERNEL = 31
SSM_WIDTH = 512
SSM_GROUP = 16
SSM_GROUPS = 32
SSM_STATE = 64
HEAD_DIM = 128
ATTN_PATTERNS = ((128, 1), (512, 4), (2048, 16))
N_PATTERNS = 3
HEADS = 8
ATTN_QKV = N_PATTERNS * HEADS * HEAD_DIM
ATTN_SCALE = HEAD_DIM ** -0.5
ROPE_THETA = 10000.0
EPS = 1e-6
NEG_INF = -1e30

LANES = 128
WINDOW_KEYS = 128
SSM_CHUNK = 16
SSM_LANE_GROUPS = LANES // SSM_GROUP
SSM_BLOCKS = SSM_WIDTH // LANES
SSM_STATE_COLS = SSM_LANE_GROUPS * SSM_STATE
SSM_BATCH = 2
ROW_TILE = 512
CONV_HALO = 32
CONV_ROWS = 64
PROJ_ROWS = 256
VMEM_LIMIT = 56 * 1024 * 1024
ATTN_VMEM_LIMIT = 58 * 1024 * 1024
HEAD_PAIR = 2


def _params(n_axes, vmem=VMEM_LIMIT):
    return pltpu.CompilerParams(dimension_semantics=("arbitrary",) * n_axes, vmem_limit_bytes=vmem)


def _adaln(x, nw, scale, shift):
    ms = jnp.mean(x * x, axis=-1, keepdims=True)
    return (x * lax.rsqrt(ms + EPS) * nw) * (1.0 + scale) + shift


def _dot(a, b):
    return jnp.dot(a, b, preferred_element_type=F32)


def _mod_kernel(c_ref, w_ref, b_ref, o_ref):
    o_ref[0] = _dot(c_ref[...].astype(BF16), w_ref[0].astype(BF16)) + b_ref[0]


def _modulation(c, mod_w, mod_b):
    bsz = c.shape[0]
    nblk = 3 * D_MODEL // D_MODEL
    return pl.pallas_call(
        _mod_kernel,
        grid=(DEPTH, nblk),
        in_specs=[pl.BlockSpec((bsz, D_MODEL), lambda l, j: (0, 0)),
                  pl.BlockSpec((1, D_MODEL, D_MODEL), lambda l, j: (l, 0, j)),
                  pl.BlockSpec((1, 1, D_MODEL), lambda l, j: (l, 0, j))],
        out_specs=pl.BlockSpec((1, bsz, D_MODEL), lambda l, j: (l, 0, j)),
        out_shape=jax.ShapeDtypeStruct((DEPTH, bsz, 3 * D_MODEL), F32),
        compiler_params=_params(2),
        name="modulation",
    )(c, mod_w, mod_b.reshape(DEPTH, 1, 3 * D_MODEL))


def _rope_kernel(pos_ref, inv_ref, tab_ref, tab1_ref, tab2_ref, stage_scr, stage2_scr):
    ang = pos_ref[0].astype(F32) * inv_ref[...]
    lane = lax.broadcasted_iota(jnp.int32, ang.shape, 1)
    tab = jnp.where(lane < HEAD_DIM // 2, jnp.cos(ang), jnp.sin(ang))
    tab_ref[0] = tab
    d1, d2 = ATTN_PATTERNS[1][1], ATTN_PATTERNS[2][1]
    assert d2 == d1 * d1
    n1, n2 = ROW_TILE // d1, ROW_TILE // d2
    stage_scr[...] = tab
    for r in range(d1):
        part = stage_scr[pl.ds(r, n1, stride=d1), :]
        tab1_ref[0, r] = part
        stage2_scr[r] = part
    for r in range(d1):
        for a in range(d1):
            tab2_ref[0, a * d1 + r] = stage2_scr[r, pl.ds(a, n2, stride=d1), :]


def _rope_tables(positions):
    bsz, seq = positions.shape
    inv = ROPE_THETA ** (-jnp.arange(0, HEAD_DIM, 2, dtype=F32) / HEAD_DIM)
    inv2 = jnp.concatenate([inv, inv]).reshape(1, HEAD_DIM)
    d1, d2 = ATTN_PATTERNS[1][1], ATTN_PATTERNS[2][1]
    perm = lambda d: (pl.BlockSpec((1, d, ROW_TILE // d, HEAD_DIM), lambda b, i: (b, 0, i, 0)),
                      jax.ShapeDtypeStruct((bsz, d, seq // d, HEAD_DIM), F32))
    specs, shapes = zip((pl.BlockSpec((1, ROW_TILE, HEAD_DIM), lambda b, i: (b, i, 0)),
                         jax.ShapeDtypeStruct((bsz, seq, HEAD_DIM), F32)), perm(d1), perm(d2))
    tabs = pl.pallas_call(
        _rope_kernel,
        grid=(bsz, seq // ROW_TILE),
        in_specs=[pl.BlockSpec((1, ROW_TILE, 1), lambda b, i: (b, i, 0)),
                  pl.BlockSpec((1, HEAD_DIM), lambda b, i: (0, 0))],
        out_specs=list(specs),
        out_shape=list(shapes),
        scratch_shapes=[pltpu.VMEM((ROW_TILE, HEAD_DIM), F32), pltpu.VMEM((d1, ROW_TILE // d1, HEAD_DIM), F32)],
        compiler_params=_params(2),
        name="rope_tables",
    )(positions.reshape(bsz, seq, 1), inv2)
    return [t.reshape(bsz, seq, HEAD_DIM) for t in tabs]


def _norm_kernel(x_ref, nw_ref, sc_ref, sh_ref, h_ref):
    h_ref[0] = _adaln(x_ref[0], nw_ref[...], sc_ref[0], sh_ref[0]).astype(BF16)


def _first_norm(x, nw, scale, shift):
    bsz, seq, _ = x.shape
    vec = pl.BlockSpec((1, 1, D_MODEL), lambda b, i: (b, 0, 0))
    row = pl.BlockSpec((1, ROW_TILE, D_MODEL), lambda b, i: (b, i, 0))
    return pl.pallas_call(
        _norm_kernel,
        grid=(bsz, seq // ROW_TILE),
        in_specs=[row, pl.BlockSpec((1, D_MODEL), lambda b, i: (0, 0)), vec, vec],
        out_specs=row,
        out_shape=jax.ShapeDtypeStruct(x.shape, BF16),
        compiler_params=_params(2),
        name="first_norm",
    )(x, nw, scale, shift)


def _even_in_kernel(h_ref, w_ref, dww_ref, dwb_ref, lnw_ref, lnb_ref,
                    ya_ref, u_ref, bz_ref, conv_scr, y_scr, az_scr):
    i = pl.program_id(1)
    h = h_ref[0]
    cw = CONV_WIDTH
    slabs = cw // LANES

    @pl.when(i == 0)
    def _():
        conv_scr[:, 0:CONV_HALO, :] = jnp.zeros((slabs, CONV_HALO, LANES), F32)

    a1 = _dot(h, w_ref[:, 0:cw])
    a2 = _dot(h, w_ref[:, cw:2 * cw])
    a = a1 * jax.nn.sigmoid(a2)
    for l in range(slabs):
        conv_scr[l, CONV_HALO:CONV_HALO + ROW_TILE, :] = a[:, l * LANES:(l + 1) * LANES]
    az_scr[...] = jax.nn.silu(_dot(h, w_ref[:, 2 * cw:3 * cw]))
    u_ref[0] = _dot(h, w_ref[:, 3 * cw:4 * cw])
    bz_ref[0] = _dot(h, w_ref[:, 4 * cw:5 * cw])

    first_tap = CONV_HALO - (CONV_KERNEL - 1)
    half = CONV_ROWS // 2

    def conv_block(r, carry):
        r0 = r * CONV_ROWS
        for l in range(slabs):
            lanes = slice(l * LANES, (l + 1) * LANES)
            for par in range(2):
                acc = jnp.broadcast_to(dwb_ref[:, lanes], (half, LANES))
                for k in range(CONV_KERNEL):
                    tap = conv_scr[l, pl.ds(r0 + first_tap + par + k, half, stride=2), :]
                    acc = acc + dww_ref[k:k + 1, lanes] * tap
                y_scr[l, pl.ds(r0 + par, half, stride=2), :] = acc
        return carry

    lax.fori_loop(0, ROW_TILE // CONV_ROWS, conv_block, 0)
    for l in range(slabs):
        conv_scr[l, 0:CONV_HALO, :] = conv_scr[l, ROW_TILE:ROW_TILE + CONV_HALO, :]

    def norm_block(r, carry):
        rows = pl.ds(pl.multiple_of(r * CONV_ROWS, CONV_ROWS), CONV_ROWS)
        acc = jnp.concatenate([y_scr[l, rows, :] for l in range(slabs)], axis=-1)
        mu = jnp.mean(acc, axis=-1, keepdims=True)
        xc = acc - mu
        y = xc * lax.rsqrt(jnp.mean(xc * xc, axis=-1, keepdims=True) + EPS)
        y = y * lnw_ref[...] + lnb_ref[...]
        ya_ref[0, rows, :] = (jax.nn.silu(y) * az_scr[rows, :]).astype(BF16)
        return carry

    lax.fori_loop(0, ROW_TILE // CONV_ROWS, norm_block, 0, unroll=4)


def _even_in(h, w_in, dw_w, dw_b, ln_w, ln_b):
    bsz, seq, _ = h.shape
    cw = CONV_WIDTH
    const = lambda shape: pl.BlockSpec(shape, lambda b, i: (0,) * len(shape))
    row = lambda width: pl.BlockSpec((1, ROW_TILE, width), lambda b, i: (b, i, 0))
    return pl.pallas_call(
        _even_in_kernel,
        grid=(bsz, seq // ROW_TILE),
        in_specs=[row(D_MODEL), const(w_in.shape), const((CONV_KERNEL, cw)),
                  const((1, cw)), const((1, cw)), const((1, cw))],
        out_specs=[row(cw), row(cw), row(cw)],
        out_shape=[jax.ShapeDtypeStruct((bsz, seq, cw), BF16),
                   jax.ShapeDtypeStruct((bsz, seq, cw), F32),
                   jax.ShapeDtypeStruct((bsz, seq, cw), F32)],
        scratch_shapes=[pltpu.VMEM((cw // LANES, CONV_HALO + ROW_TILE, LANES), F32),
                        pltpu.VMEM((cw // LANES, ROW_TILE, LANES), F32),
                        pltpu.VMEM((ROW_TILE, cw), F32)],
        compiler_params=_params(2),
        name="even_in",
    )(h, w_in, dw_w, dw_b.reshape(1, cw), ln_w.reshape(1, cw), ln_b.reshape(1, cw))


def _ssm_matrices(lam_re, lam_im, log_dt, b_re, b_im, c_re, c_im):
    hp = lax.Precision.HIGHEST
    t = SSM_CHUNK
    lr, li = lam_re.astype(F32), lam_im.astype(F32)
    dt = jnp.exp(log_dt.astype(F32))[:, None]

    def a_pow(k):
        kk = k.astype(F32)[:, None, None]
        mag = jnp.exp(kk * (lr * dt)[None])
        ang = kk * (li * dt)[None]
        return mag * jnp.cos(ang), mag * jnp.sin(ang)

    ar, ai = a_pow(jnp.ones((1,), F32))
    ar, ai = ar[0], ai[0]
    den = lr * lr + li * li
    nr = ar - 1.0
    kr = (nr * lr + ai * li) / den
    ki = (ai * lr - nr * li) / den
    br, bi = b_re.astype(F32), b_im.astype(F32)
    bbr = kr[..., None] * br - ki[..., None] * bi
    bbi = kr[..., None] * bi + ki[..., None] * br
    cr, ci = c_re.astype(F32), c_im.astype(F32)

    pr, pi = a_pow(jnp.arange(t + 1))
    wr = pr[:t, :, :, None] * bbr[None] - pi[:t, :, :, None] * bbi[None]
    wi = pr[:t, :, :, None] * bbi[None] + pi[:t, :, :, None] * bbr[None]
    kk = (jnp.einsum('gop,kgpi->kgoi', cr, wr, precision=hp)
          - jnp.einsum('gop,kgpi->kgoi', ci, wi, precision=hp))

    nb, gl, hh, pp = SSM_BLOCKS, SSM_LANE_GROUPS, SSM_GROUP, SSM_STATE
    grp = jnp.arange(gl)[:, None, None]
    row = jnp.arange(t * hh)[None, :, None]
    col = jnp.arange(t * LANES)[None, None, :]
    place = ((row // hh == col // LANES) & (row % hh == col % hh) & ((col % LANES) // hh == grp)).astype(BF16)
    prow = jnp.arange(pp)[None, :, None]
    pcol = jnp.arange(gl * pp)[None, None, :]
    spread = ((pcol % pp == prow) & (pcol // pp == grp)).astype(BF16)

    def rows_sgi(m):
        n = m.shape[-1]
        return m.reshape(nb, gl, t, hh, n).transpose(0, 2, 1, 3, 4).reshape(nb, t * LANES, n)

    lrow = jnp.arange(t * hh)[None, :, None]
    lcol = jnp.arange(t * hh)[None, None, :]
    s_idx = jnp.arange(t)[:, None, None]
    shift = ((lcol // hh == lrow // hh + s_idx) & (lcol % hh == lrow % hh)).astype(BF16)
    kk_i = kk.reshape(t, nb, gl, hh, hh).transpose(1, 2, 4, 0, 3).reshape(nb, gl, hh, t * hh)
    toe = jnp.einsum('cgil,sln->cgsin', kk_i.astype(BF16), shift,
                     preferred_element_type=BF16).reshape(nb, gl, t * hh, t * hh)
    m_intra = rows_sgi(jnp.einsum('cgab,gbn->cgan', toe, place, preferred_element_type=BF16))

    def state_cols(w):
        w = w[::-1].reshape(t, nb, gl, pp, hh).transpose(1, 2, 0, 4, 3).reshape(nb, gl, t * hh, pp)
        return rows_sgi(jnp.einsum('cgap,gpn->cgan', w.astype(BF16), spread, preferred_element_type=BF16))

    m_state = jnp.concatenate([state_cols(wr), state_cols(wi)], axis=-1)

    def in_rows(q):
        q = q.reshape(t, nb, gl, hh, pp).transpose(1, 2, 4, 0, 3).reshape(nb, gl, pp, t * hh)
        return jnp.einsum('cgpb,gbn->cgpn', q.astype(BF16), place,
                          preferred_element_type=BF16).reshape(nb, gl * pp, t * LANES)

    m_in_r = in_rows(pr[1:, :, None, :] * cr[None] - pi[1:, :, None, :] * ci[None])
    m_in_i = in_rows(-(pr[1:, :, None, :] * ci[None] + pi[1:, :, None, :] * cr[None]))

    n_lvl = int(math.log2(2048 // t))
    sr2, si2 = a_pow(t * (2 ** jnp.arange(n_lvl)))
    ap_r = sr2.reshape(n_lvl, nb, gl * pp).transpose(1, 0, 2)
    ap_i = si2.reshape(n_lvl, nb, gl * pp).transpose(1, 0, 2)
    return m_intra, m_state, m_in_r, m_in_i, ap_r, ap_i


def _ssm_kernel(u_ref, mintra_ref, mstate_ref, minr_ref, mini_ref, apr_ref, api_ref, d_ref, y_ref,
                x_scr, yi_scr, cr_scr, ci_scr, zr_scr, zi_scr, *, n_chunks, n_levels):
    t = SSM_CHUNK
    sc = SSM_STATE_COLS
    pair = 2 * LANES
    for bb in range(SSM_BATCH):
        for t0 in range(t):
            x_scr[bb * n_chunks:(bb + 1) * n_chunks, t0 * LANES:(t0 + 1) * LANES] = (
                u_ref[bb, pl.ds(t0, n_chunks, stride=t), :].astype(BF16))
    s_loc = _dot(x_scr[...], mstate_ref[0])
    for tp in range(t // 2):
        cols = slice(tp * pair, (tp + 1) * pair)
        yi_scr[:, cols] = _dot(x_scr[:, 0:(tp + 1) * pair], mintra_ref[0, 0:(tp + 1) * pair, cols])

    zero = jnp.zeros((n_chunks, sc), F32)
    for p in range(2):
        zr_scr[p, 0:n_chunks, :] = zero
        zi_scr[p, 0:n_chunks, :] = zero
    for bb in range(SSM_BATCH):
        rows = slice(bb * n_chunks, (bb + 1) * n_chunks)
        zr_scr[0, n_chunks:, :] = s_loc[rows, :sc]
        zi_scr[0, n_chunks:, :] = s_loc[rows, sc:]
        for k in range(n_levels):
            src, dst = k % 2, 1 - (k % 2)
            sh = n_chunks - (1 << k)
            zr = zr_scr[src, n_chunks:, :]
            zi = zi_scr[src, n_chunks:, :]
            pr = zr_scr[src, sh:sh + n_chunks, :]
            pi = zi_scr[src, sh:sh + n_chunks, :]
            ar = apr_ref[0, k:k + 1, :]
            ai = api_ref[0, k:k + 1, :]
            zr_scr[dst, n_chunks:, :] = zr + ar * pr - ai * pi
            zi_scr[dst, n_chunks:, :] = zi + ar * pi + ai * pr
        fin = n_levels % 2
        cr_scr[rows, :] = zr_scr[fin, n_chunks - 1:2 * n_chunks - 1, :].astype(BF16)
        ci_scr[rows, :] = zi_scr[fin, n_chunks - 1:2 * n_chunks - 1, :].astype(BF16)

    carry_r = cr_scr[...]
    carry_i = ci_scr[...]
    for tp in range(t // 2):
        cols = slice(tp * pair, (tp + 1) * pair)
        yc = yi_scr[:, cols] + _dot(carry_r, minr_ref[0, :, cols]) + _dot(carry_i, mini_ref[0, :, cols])
        for bb in range(SSM_BATCH):
            for j in range(2):
                y_ref[bb, pl.ds(2 * tp + j, n_chunks, stride=t), :] = (
                    yc[bb * n_chunks:(bb + 1) * n_chunks, j * LANES:(j + 1) * LANES])
    for bb in range(SSM_BATCH):
        y_ref[bb] = y_ref[bb] + d_ref[0] * u_ref[bb]


def _ssm(u, mats, d_skip):
    m_intra, m_state, m_in_r, m_in_i, ap_r, ap_i = mats
    bsz, seq, _ = u.shape
    n_chunks = seq // SSM_CHUNK
    n_levels = ap_r.shape[1]
    nb = SSM_BATCH
    once = pl.Buffered(1)
    blk = pl.BlockSpec((nb, seq, LANES), lambda c, b: (b, 0, c))
    mat = lambda m: pl.BlockSpec((1,) + m.shape[1:], lambda c, b: (c, 0, 0), pipeline_mode=once)
    lvl = pl.BlockSpec((1, n_levels, SSM_STATE_COLS), lambda c, b: (c, 0, 0))
    return pl.pallas_call(
        functools.partial(_ssm_kernel, n_chunks=n_chunks, n_levels=n_levels),
        grid=(SSM_BLOCKS, bsz // nb),
        in_specs=[blk, mat(m_intra), mat(m_state), mat(m_in_r), mat(m_in_i), lvl, lvl,
                  pl.BlockSpec((1, 1, LANES), lambda c, b: (c, 0, 0))],
        out_specs=blk,
        out_shape=jax.ShapeDtypeStruct(u.shape, F32),
        scratch_shapes=[pltpu.VMEM((nb * n_chunks, SSM_CHUNK * LANES), BF16),
                        pltpu.VMEM((nb * n_chunks, SSM_CHUNK * LANES), F32),
                        pltpu.VMEM((nb * n_chunks, SSM_STATE_COLS), BF16),
                        pltpu.VMEM((nb * n_chunks, SSM_STATE_COLS), BF16),
                        pltpu.VMEM((2, 2 * n_chunks, SSM_STATE_COLS), F32),
                        pltpu.VMEM((2, 2 * n_chunks, SSM_STATE_COLS), F32)],
        compiler_params=_params(2),
        name="ssm",
    )(u, m_intra, m_state, m_in_r, m_in_i, ap_r, ap_i, d_skip.reshape(SSM_BLOCKS, 1, LANES))


def _finish(x_ref, gate_ref, out, nxt, xo_ref, h_outs, stage_scr, stage2_scr):
    xn = x_ref[0] + gate_ref[0] * out
    xo_ref[0] = xn
    if nxt is None:
        return
    nw_ref, sc_ref, sh_ref = nxt
    hn = _adaln(xn, nw_ref[...], sc_ref[0], sh_ref[0])
    h_outs[0][0] = hn.astype(BF16)
    if len(h_outs) == 1:
        return
    slabs = D_MODEL // LANES
    d1, d2 = ATTN_PATTERNS[1][1], ATTN_PATTERNS[2][1]
    assert d2 == d1 * d1 and len(h_outs) == 3
    n1, n2 = ROW_TILE // d1, ROW_TILE // d2
    for l in range(slabs):
        stage_scr[l] = hn[:, l * LANES:(l + 1) * LANES]
    for r in range(d1):
        parts = [stage_scr[l, pl.ds(r, n1, stride=d1), :] for l in range(slabs)]
        h_outs[1][0, r] = jnp.concatenate(parts, axis=-1).astype(BF16)
        for l in range(slabs):
            stage2_scr[r * slabs + l] = parts[l]
    for r in range(d1):
        for a in range(d1):
            parts = [stage2_scr[r * slabs + l, pl.ds(a, n2, stride=d1), :] for l in range(slabs)]
            h_outs[2][0, a * d1 + r] = jnp.concatenate(parts, axis=-1).astype(BF16)


def _even_out_kernel(*refs, n_h):
    x_ref, ya_ref, ys_ref, bz_ref, gw_ref, gb_ref, wa_ref, ws_ref, gate_ref = refs[:9]
    nxt, xo_ref, h_outs, stages = _tail_refs(refs[9:], n_h)
    y = ys_ref[0]
    y = 0.5 * y * (1.0 + lax.erf(y * (2.0 ** -0.5)))
    y = y * jax.nn.sigmoid(_dot(y.astype(BF16), gw_ref[...]) + gb_ref[...])
    y = y * jax.nn.silu(bz_ref[0])
    out = _dot(ya_ref[0], wa_ref[...]) + _dot(y.astype(BF16), ws_ref[...])
    _finish(x_ref, gate_ref, out, nxt, xo_ref, h_outs, *stages)


def _odd_out_kernel(*refs, n_h):
    x_ref, h_ref, o_ref, wz_ref, wo_ref, gate_ref = refs[:6]
    nxt, xo_ref, h_outs, stages = _tail_refs(refs[6:], n_h)
    z = _dot(h_ref[0], wz_ref[...])
    g = o_ref[0] * jax.nn.silu(z)
    out = _dot(g.astype(BF16), wo_ref[...])
    _finish(x_ref, gate_ref, out, nxt, xo_ref, h_outs, *stages)


def _tail_refs(refs, n_h):
    if n_h == 0:
        return None, refs[0], (), (None, None)
    nxt, xo_ref, h_outs = refs[:3], refs[3], refs[4:4 + n_h]
    return nxt, xo_ref, h_outs, (tuple(refs[4 + n_h:6 + n_h]) if n_h > 1 else (None, None))


def _tail_call(body, name, x, rows, consts, gate, nxt, permuted):
    bsz, seq, _ = x.shape
    n_h = 0 if nxt is None else (N_PATTERNS if permuted else 1)
    row = lambda a: pl.BlockSpec((1, ROW_TILE, a.shape[-1]), lambda b, i: (b, i, 0))
    const = lambda a: pl.BlockSpec(a.shape, lambda b, i: (0,) * a.ndim)
    vec = pl.BlockSpec((1, 1, D_MODEL), lambda b, i: (b, 0, 0))
    args = [x, *rows, *consts, gate]
    in_specs = [row(x)] + [row(a) for a in rows] + [const(a) for a in consts] + [vec]
    out_specs = [row(x)]
    out_shape = [jax.ShapeDtypeStruct(x.shape, F32)]
    scratch = []
    if n_h:
        nw, sc, sh = nxt
        args += [nw, sc, sh]
        in_specs += [const(nw), vec, vec]
        out_specs.append(row(x))
        out_shape.append(jax.ShapeDtypeStruct(x.shape, BF16))
    if n_h > 1:
        for _, dil in ATTN_PATTERNS[1:]:
            out_specs.append(pl.BlockSpec((1, dil, ROW_TILE // dil, D_MODEL), lambda b, i: (b, 0, i, 0)))
            out_shape.append(jax.ShapeDtypeStruct((bsz, dil, seq // dil, D_MODEL), BF16))
        d1 = ATTN_PATTERNS[1][1]
        scratch.append(pltpu.VMEM((D_MODEL // LANES, ROW_TILE, LANES), F32))
        scratch.append(pltpu.VMEM((d1 * D_MODEL // LANES, ROW_TILE // d1, LANES), F32))
    res = pl.pallas_call(
        functools.partial(body, n_h=n_h),
        grid=(bsz, seq // ROW_TILE),
        in_specs=in_specs, out_specs=out_specs, out_shape=out_shape, scratch_shapes=scratch,
        compiler_params=_params(2),
        name=name,
    )(*args)
    return res[0], [h.reshape(x.shape) for h in res[1:]]


def _attn_kernel(h0_ref, h1_ref, h2_ref, w_ref, t0_ref, t1_ref, t2_ref,
                 qw_ref, kw_ref, o_ref, q_scr, k_scr, v_scr, on_scr, lse_scr, bias_scr, *, seq):
    wk = WINDOW_KEYS
    n_chunks = seq // PROJ_ROWS
    chunk_blocks = PROJ_ROWS // wk
    h_refs = (h0_ref, h1_ref, h2_ref)
    tabs = (t0_ref, t1_ref, t2_ref)
    head_cols = 3 * HEAD_DIM

    qi = lax.broadcasted_iota(jnp.int32, (wk, 2 * wk), 0)
    kj = lax.broadcasted_iota(jnp.int32, (wk, 2 * wk), 1)
    band = (kj >= qi) & (kj <= qi + wk)
    bias_scr[0] = jnp.where(band, 0.0, NEG_INF)
    bias_scr[1] = jnp.where(band & (kj >= wk), 0.0, NEG_INF)

    def chunk_rows(c):
        start = c * PROJ_ROWS
        if not isinstance(c, int):
            start = pl.multiple_of(start, PROJ_ROWS)
        return pl.ds(start, PROJ_ROWS)

    def proj_dots(c):
        rows = chunk_rows(c)
        width = HEAD_PAIR * head_cols
        return [_dot(h_refs[g][0, rows, :], w_ref[0, :, g * width:(g + 1) * width]) for g in range(N_PATTERNS)]

    def proj_store(c, prs):
        rows = chunk_rows(c)
        for g, pr in enumerate(prs):
            tab = tabs[g][0, rows, :]
            swapped = pltpu.roll(tab, HEAD_DIM // 2, 1)
            lower = lax.broadcasted_iota(jnp.int32, tab.shape, 1) < HEAD_DIM // 2
            cos = jnp.where(lower, tab, swapped)
            sin = jnp.where(lower, -swapped, tab)

            def norm_rope(t, w):
                t = t * lax.rsqrt(jnp.mean(t * t, axis=-1, keepdims=True) + EPS) * w
                return t * cos + pltpu.roll(t, HEAD_DIM // 2, 1) * sin

            for hl in range(HEAD_PAIR):
                off = hl * head_cols
                slot = g * HEAD_PAIR + hl
                q_scr[slot, rows, :] = (norm_rope(pr[:, off:off + HEAD_DIM], qw_ref[...]) * ATTN_SCALE).astype(BF16)
                k_scr[slot, rows, :] = norm_rope(pr[:, off + HEAD_DIM:off + 2 * HEAD_DIM], kw_ref[...]).astype(BF16)
                v_scr[slot, rows, :] = pr[:, off + 2 * HEAD_DIM:off + 3 * HEAD_DIM].astype(BF16)

    def block_ids(c):
        return [(g, hl, chunk_blocks * c + u, u) for g in range(N_PATTERNS) for hl in range(HEAD_PAIR)
                for u in range(chunk_blocks)]

    def is_first(g, j, u):
        n_blk = seq // ATTN_PATTERNS[g][1] // wk
        if n_blk == 1:
            return True
        if u % 2 == 1:
            return False
        if isinstance(j, int):
            return j % n_blk == 0
        return (j % n_blk) == 0

    def block_start(j):
        start = j * wk
        return start if isinstance(start, int) else pl.multiple_of(start, wk)

    def scores(g, hl, j, u):
        first = is_first(g, j, u)
        slot = g * HEAD_PAIR + hl
        q = q_scr[slot, pl.ds(block_start(j), wk), :]
        if first is True:
            kk = k_scr[slot, pl.ds(block_start(j), wk), :]
            bias = bias_scr[1, :, wk:]
        else:
            kk = k_scr[slot, pl.ds(block_start(j - 1), 2 * wk), :]
            bias = bias_scr[0] if first is False else bias_scr[jnp.where(first, 1, 0)]
        return lax.dot_general(q, kk, (((1,), (1,)), ((), ())), preferred_element_type=F32) + bias

    def softmax(s):
        m = jnp.max(s, axis=-1, keepdims=True)
        p = jnp.exp(s - m)
        den = jnp.sum(p, axis=-1, keepdims=True)
        return p.astype(BF16), m, den

    def weighted(g, hl, j, u, p):
        slot = g * HEAD_PAIR + hl
        if is_first(g, j, u) is True:
            vv = v_scr[slot, pl.ds(block_start(j), wk), :]
        else:
            vv = v_scr[slot, pl.ds(block_start(j - 1), 2 * wk), :]
        return _dot(p, vv)

    def attn_store(g, hl, j, o, m, den):
        dil = ATTN_PATTERNS[g][1]
        n_blk = seq // dil // wk
        slot = g * HEAD_PAIR + hl
        if dil == 1:
            nat = pl.ds(block_start(j), wk)
        else:
            nat = pl.ds((j % n_blk) * (wk * dil) + j // n_blk, wk, stride=dil)
        on_scr[slot, nat, :] = o * (1.0 / den)
        lse_scr[slot, nat, :] = jnp.broadcast_to(m + jnp.log(den), (wk, HEAD_DIM))

    def step(c_attn, c_proj):
        ids = block_ids(c_attn) if c_attn is not None else []
        ss = [scores(g, hl, j, u) for g, hl, j, u in ids]
        prs = proj_dots(c_proj) if c_proj is not None else None
        sm = [softmax(s) for s in ss]
        os_ = [weighted(g, hl, j, u, p) for (g, hl, j, u), (p, _, _) in zip(ids, sm)]
        if prs is not None:
            proj_store(c_proj, prs)
        for (g, hl, j, u), o, (_, m, den) in zip(ids, os_, sm):
            attn_store(g, hl, j, o, m, den)

    step(None, 0)
    step(0, 1)

    def body(c, carry):
        step(c - 1, c)
        return carry

    lax.fori_loop(2, n_chunks, body, 0)
    step(n_chunks - 1, None)

    def merge_body(c, carry):
        rows = pl.ds(pl.multiple_of(c * PROJ_ROWS, PROJ_ROWS), PROJ_ROWS)
        for hl in range(HEAD_PAIR):
            ls = [lse_scr[g * HEAD_PAIR + hl, rows, :] for g in range(N_PATTERNS)]
            mx = jnp.maximum(jnp.maximum(ls[0], ls[1]), ls[2])
            num = jnp.zeros((PROJ_ROWS, HEAD_DIM), F32)
            den = jnp.zeros((PROJ_ROWS, HEAD_DIM), F32)
            for g in range(N_PATTERNS):
                wgt = jnp.exp(ls[g] - mx)
                num = num + wgt * on_scr[g * HEAD_PAIR + hl, rows, :]
                den = den + wgt
            o_ref[0, rows, hl * HEAD_DIM:(hl + 1) * HEAD_DIM] = (num / den).astype(BF16)
        return carry

    lax.fori_loop(0, seq // PROJ_ROWS, merge_body, 0)


def _attention(hs, w_qkv, tables, q_norm_w, k_norm_w):
    bsz, seq, _ = hs[0].shape
    slots = N_PATTERNS * HEAD_PAIR
    stat = pltpu.VMEM((slots, seq, HEAD_DIM), F32)
    qkv = pltpu.VMEM((slots, seq, HEAD_DIM), BF16)
    once = pl.Buffered(1)
    hspec = pl.BlockSpec((1, seq, D_MODEL), lambda b, hp: (b, 0, 0), pipeline_mode=once)
    tspec = pl.BlockSpec((1, seq, HEAD_DIM), lambda b, hp: (b, 0, 0), pipeline_mode=once)
    vec = pl.BlockSpec((1, HEAD_DIM), lambda b, hp: (0, 0))
    return pl.pallas_call(
        functools.partial(_attn_kernel, seq=seq),
        grid=(bsz, HEADS // HEAD_PAIR),
        in_specs=[hspec, hspec, hspec,
                  pl.BlockSpec((1, D_MODEL, HEAD_PAIR * 3 * N_PATTERNS * HEAD_DIM), lambda b, hp: (hp, 0, 0)),
                  tspec, tspec, tspec, vec, vec],
        out_specs=pl.BlockSpec((1, seq, HEAD_PAIR * HEAD_DIM), lambda b, hp: (b, 0, hp)),
        out_shape=jax.ShapeDtypeStruct((bsz, seq, HEADS * HEAD_DIM), BF16),
        scratch_shapes=[qkv, qkv, qkv, stat, stat,
                        pltpu.VMEM((2, WINDOW_KEYS, 2 * WINDOW_KEYS), F32)],
        compiler_params=_params(2, ATTN_VMEM_LIMIT),
        name="attention",
    )(*hs, w_qkv, *tables, q_norm_w.reshape(1, HEAD_DIM), k_norm_w.reshape(1, HEAD_DIM))


def _pair_major_qkv(w_in):
    w = w_in[:, :3 * ATTN_QKV].reshape(D_MODEL, 3, N_PATTERNS, HEADS // HEAD_PAIR, HEAD_PAIR, HEAD_DIM)
    w = w.transpose(3, 0, 2, 4, 1, 5)
    return w.reshape(HEADS // HEAD_PAIR, D_MODEL, N_PATTERNS * HEAD_PAIR * 3 * HEAD_DIM).astype(BF16)


def kernel(x, c, positions, mod_w, mod_b, norm_w, even_w_in, conv_dw_w, conv_dw_b, conv_ln_w, conv_ln_b, ssm_lam_re, ssm_lam_im, ssm_log_dt, ssm_b_re, ssm_b_im, ssm_c_re, ssm_c_im, ssm_d, ssm_glu_w, ssm_glu_b, even_w_out, attn_w_in, attn_q_norm_w, attn_k_norm_w, attn_w_out):
    bsz, seq, _ = x.shape
    assert seq == 2048 and x.shape[-1] == D_MODEL and bsz % SSM_BATCH == 0
    mod = _modulation(c, mod_w, mod_b)
    shift = [mod[l, :, None, :D_MODEL] for l in range(DEPTH)]
    scale = [mod[l, :, None, D_MODEL:2 * D_MODEL] for l in range(DEPTH)]
    gate = [mod[l, :, None, 2 * D_MODEL:] for l in range(DEPTH)]
    nw = [norm_w[l].reshape(1, D_MODEL) for l in range(DEPTH)]
    assert DEPTH % 2 == 0
    tables = _rope_tables(positions)

    hs = [_first_norm(x, nw[0], scale[0], shift[0])]
    for layer in range(DEPTH):
        i = layer // 2
        nxt = (nw[layer + 1], scale[layer + 1], shift[layer + 1]) if layer + 1 < DEPTH else None
        if layer % 2 == 0:
            ya, u, bz = _even_in(hs[0], even_w_in[i].astype(BF16), conv_dw_w[i], conv_dw_b[i],
                                 conv_ln_w[i], conv_ln_b[i])
            mats = _ssm_matrices(ssm_lam_re[i], ssm_lam_im[i], ssm_log_dt[i], ssm_b_re[i], ssm_b_im[i],
                                 ssm_c_re[i], ssm_c_im[i])
            ys = _ssm(u, mats, ssm_d[i])
            w_out = even_w_out[i].astype(BF16)
            x, hs = _tail_call(_even_out_kernel, "even_out", x, [ya, ys, bz],
                               [ssm_glu_w[i].astype(BF16), ssm_glu_b[i].reshape(1, SSM_WIDTH),
                                w_out[:CONV_WIDTH], w_out[CONV_WIDTH:]], gate[layer], nxt, permuted=True)
        else:
            o = _attention(hs, _pair_major_qkv(attn_w_in[i]), tables, attn_q_norm_w[i], attn_k_norm_w[i])
            x, hs = _tail_call(_odd_out_kernel, "odd_out", x, [hs[0], o],
                               [attn_w_in[i][:, 3 * ATTN_QKV:].astype(BF16), attn_w_out[i].astype(BF16)],
                               gate[layer], nxt, permuted=False)
    return x
```

```python
import functools
import math

import jax
import jax.numpy as jnp
from jax import lax
from jax.experimental import pallas as pl
from jax.experimental.pallas import tpu as pltpu

F32 = jnp.float32
BF16 = jnp.bfloat16

D_MODEL = 1024
DEPTH = 4
CONV_WIDTH = 512
CONV_KERNEL = 31
SSM_WIDTH = 512
SSM_GROUP = 16
SSM_GROUPS = 32
SSM_STATE = 64
HEAD_DIM = 128
ATTN_PATTERNS = ((128, 1), (512, 4), (2048, 16))
N_PATTERNS = 3
HEADS = 8
ATTN_QKV = N_PATTERNS * HEADS * HEAD_DIM
ATTN_SCALE = HEAD_DIM ** -0.5
ROPE_THETA = 10000.0
EPS = 1e-6
NEG_INF = -1e30

LANES = 128
WINDOW_KEYS = 128
SSM_CHUNK = 16
SSM_LANE_GROUPS = LANES // SSM_GROUP
SSM_BLOCKS = SSM_WIDTH // LANES
SSM_STATE_COLS = SSM_LANE_GROUPS * SSM_STATE
SSM_BATCH = 2
ROW_TILE = 512
CONV_HALO = 32
CONV_ROWS = 64
PROJ_ROWS = 256
VMEM_LIMIT = 56 * 1024 * 1024
ATTN_VMEM_LIMIT = 58 * 1024 * 1024
HEAD_PAIR = 2


def _params(n_axes, vmem=VMEM_LIMIT):
    return pltpu.CompilerParams(dimension_semantics=("arbitrary",) * n_axes, vmem_limit_bytes=vmem)


def _adaln(x, nw, scale, shift):
    ms = jnp.mean(x * x, axis=-1, keepdims=True)
    return (x * lax.rsqrt(ms + EPS) * nw) * (1.0 + scale) + shift


def _dot(a, b):
    return jnp.dot(a, b, preferred_element_type=F32)


def _mod_kernel(c_ref, w_ref, b_ref, o_ref):
    o_ref[0] = _dot(c_ref[...].astype(BF16), w_ref[0].astype(BF16)) + b_ref[0]


def _modulation(c, mod_w, mod_b):
    bsz = c.shape[0]
    nblk = 3 * D_MODEL // D_MODEL
    return pl.pallas_call(
        _mod_kernel,
        grid=(DEPTH, nblk),
        in_specs=[pl.BlockSpec((bsz, D_MODEL), lambda l, j: (0, 0)),
                  pl.BlockSpec((1, D_MODEL, D_MODEL), lambda l, j: (l, 0, j)),
                  pl.BlockSpec((1, 1, D_MODEL), lambda l, j: (l, 0, j))],
        out_specs=pl.BlockSpec((1, bsz, D_MODEL), lambda l, j: (l, 0, j)),
        out_shape=jax.ShapeDtypeStruct((DEPTH, bsz, 3 * D_MODEL), F32),
        compiler_params=_params(2),
        name="modulation",
    )(c, mod_w, mod_b.reshape(DEPTH, 1, 3 * D_MODEL))


def _rope_kernel(pos_ref, inv_ref, tab_ref, tab1_ref, tab2_ref, stage_scr, stage2_scr):
    ang = pos_ref[0].astype(F32) * inv_ref[...]
    lane = lax.broadcasted_iota(jnp.int32, ang.shape, 1)
    tab = jnp.where(lane < HEAD_DIM // 2, jnp.cos(ang), jnp.sin(ang))
    tab_ref[0] = tab
    d1, d2 = ATTN_PATTERNS[1][1], ATTN_PATTERNS[2][1]
    assert d2 == d1 * d1
    n1, n2 = ROW_TILE // d1, ROW_TILE // d2
    stage_scr[...] = tab
    for r in range(d1):
        part = stage_scr[pl.ds(r, n1, stride=d1), :]
        tab1_ref[0, r] = part
        stage2_scr[r] = part
    for r in range(d1):
        for a in range(d1):
            tab2_ref[0, a * d1 + r] = stage2_scr[r, pl.ds(a, n2, stride=d1), :]


def _rope_tables(positions):
    bsz, seq = positions.shape
    inv = ROPE_THETA ** (-jnp.arange(0, HEAD_DIM, 2, dtype=F32) / HEAD_DIM)
    inv2 = jnp.concatenate([inv, inv]).reshape(1, HEAD_DIM)
    d1, d2 = ATTN_PATTERNS[1][1], ATTN_PATTERNS[2][1]
    perm = lambda d: (pl.BlockSpec((1, d, ROW_TILE // d, HEAD_DIM), lambda b, i: (b, 0, i, 0)),
                      jax.ShapeDtypeStruct((bsz, d, seq // d, HEAD_DIM), F32))
    specs, shapes = zip((pl.BlockSpec((1, ROW_TILE, HEAD_DIM), lambda b, i: (b, i, 0)),
                         jax.ShapeDtypeStruct((bsz, seq, HEAD_DIM), F32)), perm(d1), perm(d2))
    tabs = pl.pallas_call(
        _rope_kernel,
        grid=(bsz, seq // ROW_TILE),
        in_specs=[pl.BlockSpec((1, ROW_TILE, 1), lambda b, i: (b, i, 0)),
                  pl.BlockSpec((1, HEAD_DIM), lambda b, i: (0, 0))],
        out_specs=list(specs),
        out_shape=list(shapes),
        scratch_shapes=[pltpu.VMEM((ROW_TILE, HEAD_DIM), F32), pltpu.VMEM((d1, ROW_TILE // d1, HEAD_DIM), F32)],
        compiler_params=_params(2),
        name="rope_tables",
    )(positions.reshape(bsz, seq, 1), inv2)
    return [t.reshape(bsz, seq, HEAD_DIM) for t in tabs]


def _norm_kernel(x_ref, nw_ref, sc_ref, sh_ref, h_ref):
    h_ref[0] = _adaln(x_ref[0], nw_ref[...], sc_ref[0], sh_ref[0]).astype(BF16)


def _first_norm(x, nw, scale, shift):
    bsz, seq, _ = x.shape
    vec = pl.BlockSpec((1, 1, D_MODEL), lambda b, i: (b, 0, 0))
    row = pl.BlockSpec((1, ROW_TILE, D_MODEL), lambda b, i: (b, i, 0))
    return pl.pallas_call(
        _norm_kernel,
        grid=(bsz, seq // ROW_TILE),
        in_specs=[row, pl.BlockSpec((1, D_MODEL), lambda b, i: (0, 0)), vec, vec],
        out_specs=row,
        out_shape=jax.ShapeDtypeStruct(x.shape, BF16),
        compiler_params=_params(2),
        name="first_norm",
    )(x, nw, scale, shift)


def _even_in_kernel(h_ref, w_ref, dww_ref, dwb_ref, lnw_ref, lnb_ref,
                    ya_ref, u_ref, bz_ref, conv_scr, y_scr, az_scr):
    i = pl.program_id(1)
    h = h_ref[0]
    cw = CONV_WIDTH
    slabs = cw // LANES

    @pl.when(i == 0)
    def _():
        conv_scr[:, 0:CONV_HALO, :] = jnp.zeros((slabs, CONV_HALO, LANES), F32)

    a1 = _dot(h, w_ref[:, 0:cw])
    a2 = _dot(h, w_ref[:, cw:2 * cw])
    a = a1 * jax.nn.sigmoid(a2)
    for l in range(slabs):
        conv_scr[l, CONV_HALO:CONV_HALO + ROW_TILE, :] = a[:, l * LANES:(l + 1) * LANES]
    az_scr[...] = jax.nn.silu(_dot(h, w_ref[:, 2 * cw:3 * cw]))
    u_ref[0] = _dot(h, w_ref[:, 3 * cw:4 * cw])
    bz_ref[0] = _dot(h, w_ref[:, 4 * cw:5 * cw])

    first_tap = CONV_HALO - (CONV_KERNEL - 1)
    half = CONV_ROWS // 2

    def conv_block(r, carry):
        r0 = r * CONV_ROWS
        for l in range(slabs):
            lanes = slice(l * LANES, (l + 1) * LANES)
            for par in range(2):
                acc = jnp.broadcast_to(dwb_ref[:, lanes], (half, LANES))
                for k in range(CONV_KERNEL):
                    tap = conv_scr[l, pl.ds(r0 + first_tap + par + k, half, stride=2), :]
                    acc = acc + dww_ref[k:k + 1, lanes] * tap
                y_scr[l, pl.ds(r0 + par, half, stride=2), :] = acc
        return carry

    lax.fori_loop(0, ROW_TILE // CONV_ROWS, conv_block, 0)
    for l in range(slabs):
        conv_scr[l, 0:CONV_HALO, :] = conv_scr[l, ROW_TILE:ROW_TILE + CONV_HALO, :]

    def norm_block(r, carry):
        rows = pl.ds(pl.multiple_of(r * CONV_ROWS, CONV_ROWS), CONV_ROWS)
        acc = jnp.concatenate([y_scr[l, rows, :] for l in range(slabs)], axis=-1)
        mu = jnp.mean(acc, axis=-1, keepdims=True)
        xc = acc - mu
        y = xc * lax.rsqrt(jnp.mean(xc * xc, axis=-1, keepdims=True) + EPS)
        y = y * lnw_ref[...] + lnb_ref[...]
        ya_ref[0, rows, :] = (jax.nn.silu(y) * az_scr[rows, :]).astype(BF16)
        return carry

    lax.fori_loop(0, ROW_TILE // CONV_ROWS, norm_block, 0, unroll=4)


def _even_in(h, w_in, dw_w, dw_b, ln_w, ln_b):
    bsz, seq, _ = h.shape
    cw = CONV_WIDTH
    const = lambda shape: pl.BlockSpec(shape, lambda b, i: (0,) * len(shape))
    row = lambda width: pl.BlockSpec((1, ROW_TILE, width), lambda b, i: (b, i, 0))
    return pl.pallas_call(
        _even_in_kernel,
        grid=(bsz, seq // ROW_TILE),
        in_specs=[row(D_MODEL), const(w_in.shape), const((CONV_KERNEL, cw)),
                  const((1, cw)), const((1, cw)), const((1, cw))],
        out_specs=[row(cw), row(cw), row(cw)],
        out_shape=[jax.ShapeDtypeStruct((bsz, seq, cw), BF16),
                   jax.ShapeDtypeStruct((bsz, seq, cw), F32),
                   jax.ShapeDtypeStruct((bsz, seq, cw), F32)],
        scratch_shapes=[pltpu.VMEM((cw // LANES, CONV_HALO + ROW_TILE, LANES), F32),
                        pltpu.VMEM((cw // LANES, ROW_TILE, LANES), F32),
                        pltpu.VMEM((ROW_TILE, cw), F32)],
        compiler_params=_params(2),
        name="even_in",
    )(h, w_in, dw_w, dw_b.reshape(1, cw), ln_w.reshape(1, cw), ln_b.reshape(1, cw))


def _ssm_matrices(lam_re, lam_im, log_dt, b_re, b_im, c_re, c_im):
    hp = lax.Precision.HIGHEST
    t = SSM_CHUNK
    lr, li = lam_re.astype(F32), lam_im.astype(F32)
    dt = jnp.exp(log_dt.astype(F32))[:, None]

    def a_pow(k):
        kk = k.astype(F32)[:, None, None]
        mag = jnp.exp(kk * (lr * dt)[None])
        ang = kk * (li * dt)[None]
        return mag * jnp.cos(ang), mag * jnp.sin(ang)

    ar, ai = a_pow(jnp.ones((1,), F32))
    ar, ai = ar[0], ai[0]
    den = lr * lr + li * li
    nr = ar - 1.0
    kr = (nr * lr + ai * li) / den
    ki = (ai * lr - nr * li) / den
    br, bi = b_re.astype(F32), b_im.astype(F32)
    bbr = kr[..., None] * br - ki[..., None] * bi
    bbi = kr[..., None] * bi + ki[..., None] * br
    cr, ci = c_re.astype(F32), c_im.astype(F32)

    pr, pi = a_pow(jnp.arange(t + 1))
    wr = pr[:t, :, :, None] * bbr[None] - pi[:t, :, :, None] * bbi[None]
    wi = pr[:t, :, :, None] * bbi[None] + pi[:t, :, :, None] * bbr[None]
    kk = (jnp.einsum('gop,kgpi->kgoi', cr, wr, precision=hp)
          - jnp.einsum('gop,kgpi->kgoi', ci, wi, precision=hp))

    nb, gl, hh, pp = SSM_BLOCKS, SSM_LANE_GROUPS, SSM_GROUP, SSM_STATE
    grp = jnp.arange(gl)[:, None, None]
    row = jnp.arange(t * hh)[None, :, None]
    col = jnp.arange(t * LANES)[None, None, :]
    place = ((row // hh == col // LANES) & (row % hh == col % hh) & ((col % LANES) // hh == grp)).astype(BF16)
    prow = jnp.arange(pp)[None, :, None]
    pcol = jnp.arange(gl * pp)[None, None, :]
    spread = ((pcol % pp == prow) & (pcol // pp == grp)).astype(BF16)

    def rows_sgi(m):
        n = m.shape[-1]
        return m.reshape(nb, gl, t, hh, n).transpose(0, 2, 1, 3, 4).reshape(nb, t * LANES, n)

    lrow = jnp.arange(t * hh)[None, :, None]
    lcol = jnp.arange(t * hh)[None, None, :]
    s_idx = jnp.arange(t)[:, None, None]
    shift = ((lcol // hh == lrow // hh + s_idx) & (lcol % hh == lrow % hh)).astype(BF16)
    kk_i = kk.reshape(t, nb, gl, hh, hh).transpose(1, 2, 4, 0, 3).reshape(nb, gl, hh, t * hh)
    toe = jnp.einsum('cgil,sln->cgsin', kk_i.astype(BF16), shift,
                     preferred_element_type=BF16).reshape(nb, gl, t * hh, t * hh)
    m_intra = rows_sgi(jnp.einsum('cgab,gbn->cgan', toe, place, preferred_element_type=BF16))

    def state_cols(w):
        w = w[::-1].reshape(t, nb, gl, pp, hh).transpose(1, 2, 0, 4, 3).reshape(nb, gl, t * hh, pp)
        return rows_sgi(jnp.einsum('cgap,gpn->cgan', w.astype(BF16), spread, preferred_element_type=BF16))

    m_state = jnp.concatenate([state_cols(wr), state_cols(wi)], axis=-1)

    def in_rows(q):
        q = q.reshape(t, nb, gl, hh, pp).transpose(1, 2, 4, 0, 3).reshape(nb, gl, pp, t * hh)
        return jnp.einsum('cgpb,gbn->cgpn', q.astype(BF16), place,
                          preferred_element_type=BF16).reshape(nb, gl * pp, t * LANES)

    m_in_r = in_rows(pr[1:, :, None, :] * cr[None] - pi[1:, :, None, :] * ci[None])
    m_in_i = in_rows(-(pr[1:, :, None, :] * ci[None] + pi[1:, :, None, :] * cr[None]))

    n_lvl = int(math.log2(2048 // t))
    sr2, si2 = a_pow(t * (2 ** jnp.arange(n_lvl)))
    ap_r = sr2.reshape(n_lvl, nb, gl * pp).transpose(1, 0, 2)
    ap_i = si2.reshape(n_lvl, nb, gl * pp).transpose(1, 0, 2)
    return m_intra, m_state, m_in_r, m_in_i, ap_r, ap_i


def _ssm_kernel(u_ref, mintra_ref, mstate_ref, minr_ref, mini_ref, apr_ref, api_ref, d_ref, y_ref,
                x_scr, yi_scr, cr_scr, ci_scr, zr_scr, zi_scr, *, n_chunks, n_levels):
    t = SSM_CHUNK
    sc = SSM_STATE_COLS
    pair = 2 * LANES
    for bb in range(SSM_BATCH):
        for t0 in range(t):
            x_scr[bb * n_chunks:(bb + 1) * n_chunks, t0 * LANES:(t0 + 1) * LANES] = (
                u_ref[bb, pl.ds(t0, n_chunks, stride=t), :].astype(BF16))
    s_loc = _dot(x_scr[...], mstate_ref[0])
    for tp in range(t // 2):
        cols = slice(tp * pair, (tp + 1) * pair)
        yi_scr[:, cols] = _dot(x_scr[:, 0:(tp + 1) * pair], mintra_ref[0, 0:(tp + 1) * pair, cols])

    zero = jnp.zeros((n_chunks, sc), F32)
    for p in range(2):
        zr_scr[p, 0:n_chunks, :] = zero
        zi_scr[p, 0:n_chunks, :] = zero
    for bb in range(SSM_BATCH):
        rows = slice(bb * n_chunks, (bb + 1) * n_chunks)
        zr_scr[0, n_chunks:, :] = s_loc[rows, :sc]
        zi_scr[0, n_chunks:, :] = s_loc[rows, sc:]
        for k in range(n_levels):
            src, dst = k % 2, 1 - (k % 2)
            sh = n_chunks - (1 << k)
            zr = zr_scr[src, n_chunks:, :]
            zi = zi_scr[src, n_chunks:, :]
            pr = zr_scr[src, sh:sh + n_chunks, :]
            pi = zi_scr[src, sh:sh + n_chunks, :]
            ar = apr_ref[0, k:k + 1, :]
            ai = api_ref[0, k:k + 1, :]
            zr_scr[dst, n_chunks:, :] = zr + ar * pr - ai * pi
            zi_scr[dst, n_chunks:, :] = zi + ar * pi + ai * pr
        fin = n_levels % 2
        cr_scr[rows, :] = zr_scr[fin, n_chunks - 1:2 * n_chunks - 1, :].astype(BF16)
        ci_scr[rows, :] = zi_scr[fin, n_chunks - 1:2 * n_chunks - 1, :].astype(BF16)

    carry_r = cr_scr[...]
    carry_i = ci_scr[...]
    for tp in range(t // 2):
        cols = slice(tp * pair, (tp + 1) * pair)
        yc = yi_scr[:, cols] + _dot(carry_r, minr_ref[0, :, cols]) + _dot(carry_i, mini_ref[0, :, cols])
        for bb in range(SSM_BATCH):
            for j in range(2):
                y_ref[bb, pl.ds(2 * tp + j, n_chunks, stride=t), :] = (
                    yc[bb * n_chunks:(bb + 1) * n_chunks, j * LANES:(j + 1) * LANES])
    for bb in range(SSM_BATCH):
        y_ref[bb] = y_ref[bb] + d_ref[0] * u_ref[bb]


def _ssm(u, mats, d_skip):
    m_intra, m_state, m_in_r, m_in_i, ap_r, ap_i = mats
    bsz, seq, _ = u.shape
    n_chunks = seq // SSM_CHUNK
    n_levels = ap_r.shape[1]
    nb = SSM_BATCH
    once = pl.Buffered(1)
    blk = pl.BlockSpec((nb, seq, LANES), lambda c, b: (b, 0, c))
    mat = lambda m: pl.BlockSpec((1,) + m.shape[1:], lambda c, b: (c, 0, 0), pipeline_mode=once)
    lvl = pl.BlockSpec((1, n_levels, SSM_STATE_COLS), lambda c, b: (c, 0, 0))
    return pl.pallas_call(
        functools.partial(_ssm_kernel, n_chunks=n_chunks, n_levels=n_levels),
        grid=(SSM_BLOCKS, bsz // nb),
        in_specs=[blk, mat(m_intra), mat(m_state), mat(m_in_r), mat(m_in_i), lvl, lvl,
                  pl.BlockSpec((1, 1, LANES), lambda c, b: (c, 0, 0))],
        out_specs=blk,
        out_shape=jax.ShapeDtypeStruct(u.shape, F32),
        scratch_shapes=[pltpu.VMEM((nb * n_chunks, SSM_CHUNK * LANES), BF16),
                        pltpu.VMEM((nb * n_chunks, SSM_CHUNK * LANES), F32),
                        pltpu.VMEM((nb * n_chunks, SSM_STATE_COLS), BF16),
                        pltpu.VMEM((nb * n_chunks, SSM_STATE_COLS), BF16),
                        pltpu.VMEM((2, 2 * n_chunks, SSM_STATE_COLS), F32),
                        pltpu.VMEM((2, 2 * n_chunks, SSM_STATE_COLS), F32)],
        compiler_params=_params(2),
        name="ssm",
    )(u, m_intra, m_state, m_in_r, m_in_i, ap_r, ap_i, d_skip.reshape(SSM_BLOCKS, 1, LANES))


def _finish(x_ref, gate_ref, out, nxt, xo_ref, h_outs, stage_scr, stage2_scr):
    xn = x_ref[0] + gate_ref[0] * out
    xo_ref[0] = xn
    if nxt is None:
        return
    nw_ref, sc_ref, sh_ref = nxt
    hn = _adaln(xn, nw_ref[...], sc_ref[0], sh_ref[0])
    h_outs[0][0] = hn.astype(BF16)
    if len(h_outs) == 1:
        return
    slabs = D_MODEL // LANES
    d1, d2 = ATTN_PATTERNS[1][1], ATTN_PATTERNS[2][1]
    assert d2 == d1 * d1 and len(h_outs) == 3
    n1, n2 = ROW_TILE // d1, ROW_TILE // d2
    for l in range(slabs):
        stage_scr[l] = hn[:, l * LANES:(l + 1) * LANES]
    for r in range(d1):
        parts = [stage_scr[l, pl.ds(r, n1, stride=d1), :] for l in range(slabs)]
        h_outs[1][0, r] = jnp.concatenate(parts, axis=-1).astype(BF16)
        for l in range(slabs):
            stage2_scr[r * slabs + l] = parts[l]
    for r in range(d1):
        for a in range(d1):
            parts = [stage2_scr[r * slabs + l, pl.ds(a, n2, stride=d1), :] for l in range(slabs)]
            h_outs[2][0, a * d1 + r] = jnp.concatenate(parts, axis=-1).astype(BF16)


def _even_out_kernel(*refs, n_h):
    x_ref, ya_ref, ys_ref, bz_ref, gw_ref, gb_ref, wa_ref, ws_ref, gate_ref = refs[:9]
    nxt, xo_ref, h_outs, stages = _tail_refs(refs[9:], n_h)
    y = ys_ref[0]
    y = 0.5 * y * (1.0 + lax.erf(y * (2.0 ** -0.5)))
    y = y * jax.nn.sigmoid(_dot(y.astype(BF16), gw_ref[...]) + gb_ref[...])
    y = y * jax.nn.silu(bz_ref[0])
    out = _dot(ya_ref[0], wa_ref[...]) + _dot(y.astype(BF16), ws_ref[...])
    _finish(x_ref, gate_ref, out, nxt, xo_ref, h_outs, *stages)


def _odd_out_kernel(*refs, n_h):
    x_ref, h_ref, o_ref, wz_ref, wo_ref, gate_ref = refs[:6]
    nxt, xo_ref, h_outs, stages = _tail_refs(refs[6:], n_h)
    z = _dot(h_ref[0], wz_ref[...])
    g = o_ref[0] * jax.nn.silu(z)
    out = _dot(g.astype(BF16), wo_ref[...])
    _finish(x_ref, gate_ref, out, nxt, xo_ref, h_outs, *stages)


def _tail_refs(refs, n_h):
    if n_h == 0:
        return None, refs[0], (), (None, None)
    nxt, xo_ref, h_outs = refs[:3], refs[3], refs[4:4 + n_h]
    return nxt, xo_ref, h_outs, (tuple(refs[4 + n_h:6 + n_h]) if n_h > 1 else (None, None))


def _tail_call(body, name, x, rows, consts, gate, nxt, permuted):
    bsz, seq, _ = x.shape
    n_h = 0 if nxt is None else (N_PATTERNS if permuted else 1)
    row = lambda a: pl.BlockSpec((1, ROW_TILE, a.shape[-1]), lambda b, i: (b, i, 0))
    const = lambda a: pl.BlockSpec(a.shape, lambda b, i: (0,) * a.ndim)
    vec = pl.BlockSpec((1, 1, D_MODEL), lambda b, i: (b, 0, 0))
    args = [x, *rows, *consts, gate]
    in_specs = [row(x)] + [row(a) for a in rows] + [const(a) for a in consts] + [vec]
    out_specs = [row(x)]
    out_shape = [jax.ShapeDtypeStruct(x.shape, F32)]
    scratch = []
    if n_h:
        nw, sc, sh = nxt
        args += [nw, sc, sh]
        in_specs += [const(nw), vec, vec]
        out_specs.append(row(x))
        out_shape.append(jax.ShapeDtypeStruct(x.shape, BF16))
    if n_h > 1:
        for _, dil in ATTN_PATTERNS[1:]:
            out_specs.append(pl.BlockSpec((1, dil, ROW_TILE // dil, D_MODEL), lambda b, i: (b, 0, i, 0)))
            out_shape.append(jax.ShapeDtypeStruct((bsz, dil, seq // dil, D_MODEL), BF16))
        d1 = ATTN_PATTERNS[1][1]
        scratch.append(pltpu.VMEM((D_MODEL // LANES, ROW_TILE, LANES), F32))
        scratch.append(pltpu.VMEM((d1 * D_MODEL // LANES, ROW_TILE // d1, LANES), F32))
    res = pl.pallas_call(
        functools.partial(body, n_h=n_h),
        grid=(bsz, seq // ROW_TILE),
        in_specs=in_specs, out_specs=out_specs, out_shape=out_shape, scratch_shapes=scratch,
        compiler_params=_params(2),
        name=name,
    )(*args)
    return res[0], [h.reshape(x.shape) for h in res[1:]]


def _attn_kernel(h0_ref, h1_ref, h2_ref, *refs, seq):
    w_refs = refs[:N_PATTERNS]
    (t0_ref, t1_ref, t2_ref, qw_ref, kw_ref, o_ref,
     q_scr, k_scr, v_scr, on_scr, lse_scr, bias_scr) = refs[N_PATTERNS:]
    wk = WINDOW_KEYS
    n_chunks = seq // PROJ_ROWS
    chunk_blocks = PROJ_ROWS // wk
    h_refs = (h0_ref, h1_ref, h2_ref)
    tabs = (t0_ref, t1_ref, t2_ref)

    qi = lax.broadcasted_iota(jnp.int32, (wk, 2 * wk), 0)
    kj = lax.broadcasted_iota(jnp.int32, (wk, 2 * wk), 1)
    band = (kj >= qi) & (kj <= qi + wk)
    bias_scr[0] = jnp.where(band, 0.0, NEG_INF)
    bias_scr[1] = jnp.where(band & (kj >= wk), 0.0, NEG_INF)

    def chunk_rows(c):
        start = c * PROJ_ROWS
        if not isinstance(c, int):
            start = pl.multiple_of(start, PROJ_ROWS)
        return pl.ds(start, PROJ_ROWS)

    def proj_dots(c):
        rows = chunk_rows(c)
        return [_dot(h_refs[g][0, rows, :], w_refs[g][...]) for g in range(N_PATTERNS)]

    def proj_store(c, prs):
        rows = chunk_rows(c)
        for g, pr in enumerate(prs):
            tab = tabs[g][0, rows, :]
            swapped = pltpu.roll(tab, HEAD_DIM // 2, 1)
            lower = lax.broadcasted_iota(jnp.int32, tab.shape, 1) < HEAD_DIM // 2
            cos = jnp.where(lower, tab, swapped)
            sin = jnp.where(lower, -swapped, tab)

            def norm_rope(t, w):
                t = t * lax.rsqrt(jnp.mean(t * t, axis=-1, keepdims=True) + EPS) * w
                return t * cos + pltpu.roll(t, HEAD_DIM // 2, 1) * sin

            for hl in range(HEAD_PAIR):
                off = hl * 3 * HEAD_DIM
                slot = g * HEAD_PAIR + hl
                q_scr[slot, rows, :] = (norm_rope(pr[:, off:off + HEAD_DIM], qw_ref[...]) * ATTN_SCALE).astype(BF16)
                k_scr[slot, rows, :] = norm_rope(pr[:, off + HEAD_DIM:off + 2 * HEAD_DIM], kw_ref[...]).astype(BF16)
                v_scr[slot, rows, :] = pr[:, off + 2 * HEAD_DIM:off + 3 * HEAD_DIM].astype(BF16)

    def block_ids(c):
        return [(g, hl, chunk_blocks * c + u, u) for g in range(N_PATTERNS) for hl in range(HEAD_PAIR)
                for u in range(chunk_blocks)]

    def is_first(g, j, u):
        n_blk = seq // ATTN_PATTERNS[g][1] // wk
        if n_blk == 1:
            return True
        if u % 2 == 1:
            return False
        if isinstance(j, int):
            return j % n_blk == 0
        return (j % n_blk) == 0

    def block_start(j):
        start = j * wk
        return start if isinstance(start, int) else pl.multiple_of(start, wk)

    def scores(g, hl, j, u):
        first = is_first(g, j, u)
        slot = g * HEAD_PAIR + hl
        q = q_scr[slot, pl.ds(block_start(j), wk), :]
        if first is True:
            kk = k_scr[slot, pl.ds(block_start(j), wk), :]
            bias = bias_scr[1, :, wk:]
        else:
            kk = k_scr[slot, pl.ds(block_start(j - 1), 2 * wk), :]
            bias = bias_scr[0] if first is False else bias_scr[jnp.where(first, 1, 0)]
        return lax.dot_general(q, kk, (((1,), (1,)), ((), ())), preferred_element_type=F32) + bias

    def softmax(s):
        m = jnp.max(s, axis=-1, keepdims=True)
        p = jnp.exp(s - m)
        den = jnp.sum(p, axis=-1, keepdims=True)
        return p.astype(BF16), m, den

    def weighted(g, hl, j, u, p):
        slot = g * HEAD_PAIR + hl
        if is_first(g, j, u) is True:
            vv = v_scr[slot, pl.ds(block_start(j), wk), :]
        else:
            vv = v_scr[slot, pl.ds(block_start(j - 1), 2 * wk), :]
        return _dot(p, vv)

    def attn_store(g, hl, j, o, m, den):
        dil = ATTN_PATTERNS[g][1]
        n_blk = seq // dil // wk
        slot = g * HEAD_PAIR + hl
        if dil == 1:
            nat = pl.ds(block_start(j), wk)
        else:
            nat = pl.ds((j % n_blk) * (wk * dil) + j // n_blk, wk, stride=dil)
        on_scr[slot, nat, :] = o * (1.0 / den)
        lse_scr[slot, nat, :] = jnp.broadcast_to(m + jnp.log(den), (wk, HEAD_DIM))

    def step(c_attn, c_proj):
        ids = block_ids(c_attn) if c_attn is not None else []
        ss = [scores(g, hl, j, u) for g, hl, j, u in ids]
        prs = proj_dots(c_proj) if c_proj is not None else None
        sm = [softmax(s) for s in ss]
        os_ = [weighted(g, hl, j, u, p) for (g, hl, j, u), (p, _, _) in zip(ids, sm)]
        if prs is not None:
            proj_store(c_proj, prs)
        for (g, hl, j, u), o, (_, m, den) in zip(ids, os_, sm):
            attn_store(g, hl, j, o, m, den)

    step(None, 0)
    step(0, 1)

    def body(c, carry):
        step(c - 1, c)
        return carry

    lax.fori_loop(2, n_chunks, body, 0)
    step(n_chunks - 1, None)

    def merge_body(c, carry):
        rows = pl.ds(pl.multiple_of(c * PROJ_ROWS, PROJ_ROWS), PROJ_ROWS)
        for hl in range(HEAD_PAIR):
            ls = [lse_scr[g * HEAD_PAIR + hl, rows, :] for g in range(N_PATTERNS)]
            mx = jnp.maximum(jnp.maximum(ls[0], ls[1]), ls[2])
            num = jnp.zeros((PROJ_ROWS, HEAD_DIM), F32)
            den = jnp.zeros((PROJ_ROWS, HEAD_DIM), F32)
            for g in range(N_PATTERNS):
                wgt = jnp.exp(ls[g] - mx)
                num = num + wgt * on_scr[g * HEAD_PAIR + hl, rows, :]
                den = den + wgt
            o_ref[0, rows, hl * HEAD_DIM:(hl + 1) * HEAD_DIM] = (num / den).astype(BF16)
        return carry

    lax.fori_loop(0, seq // PROJ_ROWS, merge_body, 0)


def _attention(hs, w_qkv, tables, q_norm_w, k_norm_w):
    bsz, seq, _ = hs[0].shape
    pair_cols = HEAD_PAIR * 3 * HEAD_DIM
    wspec = lambda g: pl.BlockSpec((D_MODEL, pair_cols), lambda b, hp: (0, g * (HEADS // HEAD_PAIR) + hp))
    slots = N_PATTERNS * HEAD_PAIR
    stat = pltpu.VMEM((slots, seq, HEAD_DIM), F32)
    qkv = pltpu.VMEM((slots, seq, HEAD_DIM), BF16)
    once = pl.Buffered(1)
    hspec = pl.BlockSpec((1, seq, D_MODEL), lambda b, hp: (b, 0, 0), pipeline_mode=once)
    tspec = pl.BlockSpec((1, seq, HEAD_DIM), lambda b, hp: (b, 0, 0), pipeline_mode=once)
    vec = pl.BlockSpec((1, HEAD_DIM), lambda b, hp: (0, 0))
    return pl.pallas_call(
        functools.partial(_attn_kernel, seq=seq),
        grid=(bsz, HEADS // HEAD_PAIR),
        in_specs=[hspec, hspec, hspec, *[wspec(g) for g in range(N_PATTERNS)],
                  tspec, tspec, tspec, vec, vec],
        out_specs=pl.BlockSpec((1, seq, HEAD_PAIR * HEAD_DIM), lambda b, hp: (b, 0, hp)),
        out_shape=jax.ShapeDtypeStruct((bsz, seq, HEADS * HEAD_DIM), BF16),
        scratch_shapes=[qkv, qkv, qkv, stat, stat,
                        pltpu.VMEM((2, WINDOW_KEYS, 2 * WINDOW_KEYS), F32)],
        compiler_params=_params(2, ATTN_VMEM_LIMIT),
        name="attention",
    )(*hs, *[w_qkv] * N_PATTERNS, *tables, q_norm_w.reshape(1, HEAD_DIM), k_norm_w.reshape(1, HEAD_DIM))


def _cast_kernel(w_ref, o_ref):
    o_ref[...] = w_ref[...].astype(BF16)


def _pair_major_qkv(w_in):
    n_heads = N_PATTERNS * HEADS

    def source_block(j):
        head, part = j // 3, j % 3
        return (0, part * n_heads + head)

    return pl.pallas_call(
        _cast_kernel,
        grid=(3 * n_heads,),
        in_specs=[pl.BlockSpec((D_MODEL, HEAD_DIM), source_block)],
        out_specs=pl.BlockSpec((D_MODEL, HEAD_DIM), lambda j: (0, j)),
        out_shape=jax.ShapeDtypeStruct((D_MODEL, 3 * ATTN_QKV), BF16),
        compiler_params=_params(1),
        name="qkv_weights",
    )(w_in)


def kernel(x, c, positions, mod_w, mod_b, norm_w, even_w_in, conv_dw_w, conv_dw_b, conv_ln_w, conv_ln_b, ssm_lam_re, ssm_lam_im, ssm_log_dt, ssm_b_re, ssm_b_im, ssm_c_re, ssm_c_im, ssm_d, ssm_glu_w, ssm_glu_b, even_w_out, attn_w_in, attn_q_norm_w, attn_k_norm_w, attn_w_out):
    bsz, seq, _ = x.shape
    assert seq == 2048 and x.shape[-1] == D_MODEL and bsz % SSM_BATCH == 0
    mod = _modulation(c, mod_w, mod_b)
    shift = [mod[l, :, None, :D_MODEL] for l in range(DEPTH)]
    scale = [mod[l, :, None, D_MODEL:2 * D_MODEL] for l in range(DEPTH)]
    gate = [mod[l, :, None, 2 * D_MODEL:] for l in range(DEPTH)]
    nw = [norm_w[l].reshape(1, D_MODEL) for l in range(DEPTH)]
    assert DEPTH % 2 == 0
    tables = _rope_tables(positions)

    hs = [_first_norm(x, nw[0], scale[0], shift[0])]
    for layer in range(DEPTH):
        i = layer // 2
        nxt = (nw[layer + 1], scale[layer + 1], shift[layer + 1]) if layer + 1 < DEPTH else None
        if layer % 2 == 0:
            ya, u, bz = _even_in(hs[0], even_w_in[i].astype(BF16), conv_dw_w[i], conv_dw_b[i],
                                 conv_ln_w[i], conv_ln_b[i])
            mats = _ssm_matrices(ssm_lam_re[i], ssm_lam_im[i], ssm_log_dt[i], ssm_b_re[i], ssm_b_im[i],
                                 ssm_c_re[i], ssm_c_im[i])
            ys = _ssm(u, mats, ssm_d[i])
            w_out = even_w_out[i].astype(BF16)
            x, hs = _tail_call(_even_out_kernel, "even_out", x, [ya, ys, bz],
                               [ssm_glu_w[i].astype(BF16), ssm_glu_b[i].reshape(1, SSM_WIDTH),
                                w_out[:CONV_WIDTH], w_out[CONV_WIDTH:]], gate[layer], nxt, permuted=True)
        else:
            o = _attention(hs, _pair_major_qkv(attn_w_in[i]), tables, attn_q_norm_w[i], attn_k_norm_w[i])
            x, hs = _tail_call(_odd_out_kernel, "odd_out", x, [hs[0], o],
                               [attn_w_in[i][:, 3 * ATTN_QKV:].astype(BF16), attn_w_out[i].astype(BF16)],
                               gate[layer], nxt, permuted=False)
    return x
```

```python
import functools
import math

import jax
import jax.numpy as jnp
from jax import lax
from jax.experimental import pallas as pl
from jax.experimental.pallas import tpu as pltpu

F32 = jnp.float32
BF16 = jnp.bfloat16

D_MODEL = 1024
DEPTH = 4
CONV_WIDTH = 512
CONV_KERNEL = 31
SSM_WIDTH = 512
SSM_GROUP = 16
SSM_GROUPS = 32
SSM_STATE = 64
HEAD_DIM = 128
ATTN_PATTERNS = ((128, 1), (512, 4), (2048, 16))
N_PATTERNS = 3
HEADS = 8
ATTN_QKV = N_PATTERNS * HEADS * HEAD_DIM
ATTN_SCALE = HEAD_DIM ** -0.5
ROPE_THETA = 10000.0
EPS = 1e-6
NEG_INF = -1e30

LANES = 128
WINDOW_KEYS = 128
SSM_CHUNK = 16
SSM_LANE_GROUPS = LANES // SSM_GROUP
SSM_BLOCKS = SSM_WIDTH // LANES
SSM_STATE_COLS = SSM_LANE_GROUPS * SSM_STATE
SSM_BATCH = 2
ROW_TILE = 512
CONV_HALO = 32
CONV_ROWS = 64
PROJ_ROWS = 256
VMEM_LIMIT = 56 * 1024 * 1024
ATTN_VMEM_LIMIT = 58 * 1024 * 1024
HEAD_PAIR = 2


def _params(n_axes, vmem=VMEM_LIMIT):
    return pltpu.CompilerParams(dimension_semantics=("arbitrary",) * n_axes, vmem_limit_bytes=vmem)


def _adaln(x, nw, scale, shift):
    ms = jnp.mean(x * x, axis=-1, keepdims=True)
    return (x * lax.rsqrt(ms + EPS) * nw) * (1.0 + scale) + shift


def _dot(a, b):
    return jnp.dot(a, b, preferred_element_type=F32)


def _mod_kernel(c_ref, w_ref, b_ref, o_ref):
    o_ref[0] = _dot(c_ref[...].astype(BF16), w_ref[0].astype(BF16)) + b_ref[0]


def _modulation(c, mod_w, mod_b):
    bsz = c.shape[0]
    nblk = 3 * D_MODEL // D_MODEL
    return pl.pallas_call(
        _mod_kernel,
        grid=(DEPTH, nblk),
        in_specs=[pl.BlockSpec((bsz, D_MODEL), lambda l, j: (0, 0)),
                  pl.BlockSpec((1, D_MODEL, D_MODEL), lambda l, j: (l, 0, j)),
                  pl.BlockSpec((1, 1, D_MODEL), lambda l, j: (l, 0, j))],
        out_specs=pl.BlockSpec((1, bsz, D_MODEL), lambda l, j: (l, 0, j)),
        out_shape=jax.ShapeDtypeStruct((DEPTH, bsz, 3 * D_MODEL), F32),
        compiler_params=_params(2),
        name="modulation",
    )(c, mod_w, mod_b.reshape(DEPTH, 1, 3 * D_MODEL))


def _rope_kernel(pos_ref, inv_ref, tab_ref, tab1_ref, tab2_ref, stage_scr, stage2_scr):
    half_rows, half_lanes = ROW_TILE // 2, HEAD_DIM // 2
    pos = pos_ref[0].astype(F32)
    lower = lax.broadcasted_iota(jnp.int32, (half_rows, HEAD_DIM), 1) < half_lanes
    ang = jnp.where(lower, pos[:half_rows], pos[half_rows:]) * inv_ref[...]
    cos, sin = jnp.cos(ang), jnp.sin(ang)
    tab = jnp.concatenate([jnp.where(lower, cos, pltpu.roll(sin, half_lanes, 1)),
                           jnp.where(lower, pltpu.roll(cos, half_lanes, 1), sin)], axis=0)
    tab_ref[0] = tab
    d1, d2 = ATTN_PATTERNS[1][1], ATTN_PATTERNS[2][1]
    assert d2 == d1 * d1
    n1, n2 = ROW_TILE // d1, ROW_TILE // d2
    stage_scr[...] = tab
    for r in range(d1):
        part = stage_scr[pl.ds(r, n1, stride=d1), :]
        tab1_ref[0, r] = part
        stage2_scr[r] = part
    for r in range(d1):
        for a in range(d1):
            tab2_ref[0, a * d1 + r] = stage2_scr[r, pl.ds(a, n2, stride=d1), :]


def _rope_tables(positions):
    bsz, seq = positions.shape
    inv = ROPE_THETA ** (-jnp.arange(0, HEAD_DIM, 2, dtype=F32) / HEAD_DIM)
    inv2 = jnp.concatenate([inv, inv]).reshape(1, HEAD_DIM)
    d1, d2 = ATTN_PATTERNS[1][1], ATTN_PATTERNS[2][1]
    perm = lambda d: (pl.BlockSpec((1, d, ROW_TILE // d, HEAD_DIM), lambda b, i: (b, 0, i, 0)),
                      jax.ShapeDtypeStruct((bsz, d, seq // d, HEAD_DIM), F32))
    specs, shapes = zip((pl.BlockSpec((1, ROW_TILE, HEAD_DIM), lambda b, i: (b, i, 0)),
                         jax.ShapeDtypeStruct((bsz, seq, HEAD_DIM), F32)), perm(d1), perm(d2))
    tabs = pl.pallas_call(
        _rope_kernel,
        grid=(bsz, seq // ROW_TILE),
        in_specs=[pl.BlockSpec((1, ROW_TILE, 1), lambda b, i: (b, i, 0)),
                  pl.BlockSpec((1, HEAD_DIM), lambda b, i: (0, 0))],
        out_specs=list(specs),
        out_shape=list(shapes),
        scratch_shapes=[pltpu.VMEM((ROW_TILE, HEAD_DIM), F32), pltpu.VMEM((d1, ROW_TILE // d1, HEAD_DIM), F32)],
        compiler_params=_params(2),
        name="rope_tables",
    )(positions.reshape(bsz, seq, 1), inv2)
    return [t.reshape(bsz, seq, HEAD_DIM) for t in tabs]


def _norm_kernel(x_ref, nw_ref, sc_ref, sh_ref, h_ref):
    h_ref[0] = _adaln(x_ref[0], nw_ref[...], sc_ref[0], sh_ref[0]).astype(BF16)


def _first_norm(x, nw, scale, shift):
    bsz, seq, _ = x.shape
    vec = pl.BlockSpec((1, 1, D_MODEL), lambda b, i: (b, 0, 0))
    row = pl.BlockSpec((1, ROW_TILE, D_MODEL), lambda b, i: (b, i, 0))
    return pl.pallas_call(
        _norm_kernel,
        grid=(bsz, seq // ROW_TILE),
        in_specs=[row, pl.BlockSpec((1, D_MODEL), lambda b, i: (0, 0)), vec, vec],
        out_specs=row,
        out_shape=jax.ShapeDtypeStruct(x.shape, BF16),
        compiler_params=_params(2),
        name="first_norm",
    )(x, nw, scale, shift)


def _even_in_kernel(h_ref, w_ref, dww_ref, dwb_ref, lnw_ref, lnb_ref,
                    ya_ref, u_ref, bz_ref, conv_scr, y_scr, az_scr):
    i = pl.program_id(1)
    h = h_ref[0]
    cw = CONV_WIDTH
    slabs = cw // LANES

    @pl.when(i == 0)
    def _():
        conv_scr[:, 0:CONV_HALO, :] = jnp.zeros((slabs, CONV_HALO, LANES), F32)

    a1 = _dot(h, w_ref[:, 0:cw])
    a2 = _dot(h, w_ref[:, cw:2 * cw])
    a = a1 * jax.nn.sigmoid(a2)
    for l in range(slabs):
        conv_scr[l, CONV_HALO:CONV_HALO + ROW_TILE, :] = a[:, l * LANES:(l + 1) * LANES]
    az_scr[...] = jax.nn.silu(_dot(h, w_ref[:, 2 * cw:3 * cw]))
    u_ref[0] = _dot(h, w_ref[:, 3 * cw:4 * cw])
    bz_ref[0] = _dot(h, w_ref[:, 4 * cw:5 * cw])

    first_tap = CONV_HALO - (CONV_KERNEL - 1)
    half = CONV_ROWS // 2

    def conv_block(r, carry):
        r0 = r * CONV_ROWS
        for l in range(slabs):
            lanes = slice(l * LANES, (l + 1) * LANES)
            for par in range(2):
                acc = jnp.broadcast_to(dwb_ref[:, lanes], (half, LANES))
                for k in range(CONV_KERNEL):
                    tap = conv_scr[l, pl.ds(r0 + first_tap + par + k, half, stride=2), :]
                    acc = acc + dww_ref[k:k + 1, lanes] * tap
                y_scr[l, pl.ds(r0 + par, half, stride=2), :] = acc
        return carry

    lax.fori_loop(0, ROW_TILE // CONV_ROWS, conv_block, 0)
    for l in range(slabs):
        conv_scr[l, 0:CONV_HALO, :] = conv_scr[l, ROW_TILE:ROW_TILE + CONV_HALO, :]

    def norm_block(r, carry):
        rows = pl.ds(pl.multiple_of(r * CONV_ROWS, CONV_ROWS), CONV_ROWS)
        acc = jnp.concatenate([y_scr[l, rows, :] for l in range(slabs)], axis=-1)
        mu = jnp.mean(acc, axis=-1, keepdims=True)
        xc = acc - mu
        y = xc * lax.rsqrt(jnp.mean(xc * xc, axis=-1, keepdims=True) + EPS)
        y = y * lnw_ref[...] + lnb_ref[...]
        ya_ref[0, rows, :] = (jax.nn.silu(y) * az_scr[rows, :]).astype(BF16)
        return carry

    lax.fori_loop(0, ROW_TILE // CONV_ROWS, norm_block, 0, unroll=4)


def _even_in(h, w_in, dw_w, dw_b, ln_w, ln_b):
    bsz, seq, _ = h.shape
    cw = CONV_WIDTH
    const = lambda shape: pl.BlockSpec(shape, lambda b, i: (0,) * len(shape))
    row = lambda width: pl.BlockSpec((1, ROW_TILE, width), lambda b, i: (b, i, 0))
    return pl.pallas_call(
        _even_in_kernel,
        grid=(bsz, seq // ROW_TILE),
        in_specs=[row(D_MODEL), const(w_in.shape), const((CONV_KERNEL, cw)),
                  const((1, cw)), const((1, cw)), const((1, cw))],
        out_specs=[row(cw), row(cw), row(cw)],
        out_shape=[jax.ShapeDtypeStruct((bsz, seq, cw), BF16),
                   jax.ShapeDtypeStruct((bsz, seq, cw), F32),
                   jax.ShapeDtypeStruct((bsz, seq, cw), F32)],
        scratch_shapes=[pltpu.VMEM((cw // LANES, CONV_HALO + ROW_TILE, LANES), F32),
                        pltpu.VMEM((cw // LANES, ROW_TILE, LANES), F32),
                        pltpu.VMEM((ROW_TILE, cw), F32)],
        compiler_params=_params(2),
        name="even_in",
    )(h, w_in, dw_w, dw_b.reshape(1, cw), ln_w.reshape(1, cw), ln_b.reshape(1, cw))


def _ssm_matrices(lam_re, lam_im, log_dt, b_re, b_im, c_re, c_im):
    hp = lax.Precision.HIGHEST
    t = SSM_CHUNK
    lr, li = lam_re.astype(F32), lam_im.astype(F32)
    dt = jnp.exp(log_dt.astype(F32))[:, None]

    def a_pow(k):
        kk = k.astype(F32)[:, None, None]
        mag = jnp.exp(kk * (lr * dt)[None])
        ang = kk * (li * dt)[None]
        return mag * jnp.cos(ang), mag * jnp.sin(ang)

    ar, ai = a_pow(jnp.ones((1,), F32))
    ar, ai = ar[0], ai[0]
    den = lr * lr + li * li
    nr = ar - 1.0
    kr = (nr * lr + ai * li) / den
    ki = (ai * lr - nr * li) / den
    br, bi = b_re.astype(F32), b_im.astype(F32)
    bbr = kr[..., None] * br - ki[..., None] * bi
    bbi = kr[..., None] * bi + ki[..., None] * br
    cr, ci = c_re.astype(F32), c_im.astype(F32)

    pr, pi = a_pow(jnp.arange(t + 1))
    wr = pr[:t, :, :, None] * bbr[None] - pi[:t, :, :, None] * bbi[None]
    wi = pr[:t, :, :, None] * bbi[None] + pi[:t, :, :, None] * bbr[None]
    kk = (jnp.einsum('gop,kgpi->kgoi', cr, wr, precision=hp)
          - jnp.einsum('gop,kgpi->kgoi', ci, wi, precision=hp))

    nb, gl, hh, pp = SSM_BLOCKS, SSM_LANE_GROUPS, SSM_GROUP, SSM_STATE
    grp = jnp.arange(gl)[:, None, None]
    row = jnp.arange(t * hh)[None, :, None]
    col = jnp.arange(t * LANES)[None, None, :]
    place = ((row // hh == col // LANES) & (row % hh == col % hh) & ((col % LANES) // hh == grp)).astype(BF16)
    prow = jnp.arange(pp)[None, :, None]
    pcol = jnp.arange(gl * pp)[None, None, :]
    spread = ((pcol % pp == prow) & (pcol // pp == grp)).astype(BF16)

    def rows_sgi(m):
        n = m.shape[-1]
        return m.reshape(nb, gl, t, hh, n).transpose(0, 2, 1, 3, 4).reshape(nb, t * LANES, n)

    lrow = jnp.arange(t * hh)[None, :, None]
    lcol = jnp.arange(t * hh)[None, None, :]
    s_idx = jnp.arange(t)[:, None, None]
    shift = ((lcol // hh == lrow // hh + s_idx) & (lcol % hh == lrow % hh)).astype(BF16)
    kk_i = kk.reshape(t, nb, gl, hh, hh).transpose(1, 2, 4, 0, 3).reshape(nb, gl, hh, t * hh)
    toe = jnp.einsum('cgil,sln->cgsin', kk_i.astype(BF16), shift,
                     preferred_element_type=BF16).reshape(nb, gl, t * hh, t * hh)
    m_intra = rows_sgi(jnp.einsum('cgab,gbn->cgan', toe, place, preferred_element_type=BF16))

    def state_cols(w):
        w = w[::-1].reshape(t, nb, gl, pp, hh).transpose(1, 2, 0, 4, 3).reshape(nb, gl, t * hh, pp)
        return rows_sgi(jnp.einsum('cgap,gpn->cgan', w.astype(BF16), spread, preferred_element_type=BF16))

    m_state = jnp.concatenate([state_cols(wr), state_cols(wi)], axis=-1)

    def in_rows(q):
        q = q.reshape(t, nb, gl, hh, pp).transpose(1, 2, 4, 0, 3).reshape(nb, gl, pp, t * hh)
        return jnp.einsum('cgpb,gbn->cgpn', q.astype(BF16), place,
                          preferred_element_type=BF16).reshape(nb, gl * pp, t * LANES)

    m_in_r = in_rows(pr[1:, :, None, :] * cr[None] - pi[1:, :, None, :] * ci[None])
    m_in_i = in_rows(-(pr[1:, :, None, :] * ci[None] + pi[1:, :, None, :] * cr[None]))

    n_lvl = int(math.log2(2048 // t))
    sr2, si2 = a_pow(t * (2 ** jnp.arange(n_lvl)))
    ap_r = sr2.reshape(n_lvl, nb, gl * pp).transpose(1, 0, 2)
    ap_i = si2.reshape(n_lvl, nb, gl * pp).transpose(1, 0, 2)
    return m_intra, m_state, m_in_r, m_in_i, ap_r, ap_i


def _ssm_kernel(u_ref, mintra_ref, mstate_ref, minr_ref, mini_ref, apr_ref, api_ref, d_ref, y_ref,
                x_scr, yi_scr, cr_scr, ci_scr, zr_scr, zi_scr, *, n_chunks, n_levels):
    t = SSM_CHUNK
    sc = SSM_STATE_COLS
    pair = 2 * LANES
    for bb in range(SSM_BATCH):
        for t0 in range(t):
            x_scr[bb * n_chunks:(bb + 1) * n_chunks, t0 * LANES:(t0 + 1) * LANES] = (
                u_ref[bb, pl.ds(t0, n_chunks, stride=t), :].astype(BF16))
    s_loc = _dot(x_scr[...], mstate_ref[0])
    for tp in range(t // 2):
        cols = slice(tp * pair, (tp + 1) * pair)
        yi_scr[:, cols] = _dot(x_scr[:, 0:(tp + 1) * pair], mintra_ref[0, 0:(tp + 1) * pair, cols])

    zero = jnp.zeros((n_chunks, sc), F32)
    for p in range(2):
        zr_scr[p, 0:n_chunks, :] = zero
        zi_scr[p, 0:n_chunks, :] = zero
    for bb in range(SSM_BATCH):
        rows = slice(bb * n_chunks, (bb + 1) * n_chunks)
        zr_scr[0, n_chunks:, :] = s_loc[rows, :sc]
        zi_scr[0, n_chunks:, :] = s_loc[rows, sc:]
        for k in range(n_levels):
            src, dst = k % 2, 1 - (k % 2)
            sh = n_chunks - (1 << k)
            zr = zr_scr[src, n_chunks:, :]
            zi = zi_scr[src, n_chunks:, :]
            pr = zr_scr[src, sh:sh + n_chunks, :]
            pi = zi_scr[src, sh:sh + n_chunks, :]
            ar = apr_ref[0, k:k + 1, :]
            ai = api_ref[0, k:k + 1, :]
            zr_scr[dst, n_chunks:, :] = zr + ar * pr - ai * pi
            zi_scr[dst, n_chunks:, :] = zi + ar * pi + ai * pr
        fin = n_levels % 2
        cr_scr[rows, :] = zr_scr[fin, n_chunks - 1:2 * n_chunks - 1, :].astype(BF16)
        ci_scr[rows, :] = zi_scr[fin, n_chunks - 1:2 * n_chunks - 1, :].astype(BF16)

    carry_r = cr_scr[...]
    carry_i = ci_scr[...]
    for tp in range(t // 2):
        cols = slice(tp * pair, (tp + 1) * pair)
        yc = yi_scr[:, cols] + _dot(carry_r, minr_ref[0, :, cols]) + _dot(carry_i, mini_ref[0, :, cols])
        for bb in range(SSM_BATCH):
            for j in range(2):
                y_ref[bb, pl.ds(2 * tp + j, n_chunks, stride=t), :] = (
                    yc[bb * n_chunks:(bb + 1) * n_chunks, j * LANES:(j + 1) * LANES])
    for bb in range(SSM_BATCH):
        y_ref[bb] = y_ref[bb] + d_ref[0] * u_ref[bb]


def _ssm(u, mats, layer, d_skip):
    m_intra, m_state, m_in_r, m_in_i, ap_r, ap_i = mats
    bsz, seq, _ = u.shape
    n_chunks = seq // SSM_CHUNK
    n_levels = ap_r.shape[2]
    nb = SSM_BATCH
    once = pl.Buffered(1)
    blk = pl.BlockSpec((nb, seq, LANES), lambda c, b: (b, 0, c))
    mat = lambda m: pl.BlockSpec((None, 1) + m.shape[2:], lambda c, b: (layer, c, 0, 0), pipeline_mode=once)
    lvl = pl.BlockSpec((None, 1, n_levels, SSM_STATE_COLS), lambda c, b: (layer, c, 0, 0))
    return pl.pallas_call(
        functools.partial(_ssm_kernel, n_chunks=n_chunks, n_levels=n_levels),
        grid=(SSM_BLOCKS, bsz // nb),
        in_specs=[blk, mat(m_intra), mat(m_state), mat(m_in_r), mat(m_in_i), lvl, lvl,
                  pl.BlockSpec((1, 1, LANES), lambda c, b: (c, 0, 0))],
        out_specs=blk,
        out_shape=jax.ShapeDtypeStruct(u.shape, F32),
        scratch_shapes=[pltpu.VMEM((nb * n_chunks, SSM_CHUNK * LANES), BF16),
                        pltpu.VMEM((nb * n_chunks, SSM_CHUNK * LANES), F32),
                        pltpu.VMEM((nb * n_chunks, SSM_STATE_COLS), BF16),
                        pltpu.VMEM((nb * n_chunks, SSM_STATE_COLS), BF16),
                        pltpu.VMEM((2, 2 * n_chunks, SSM_STATE_COLS), F32),
                        pltpu.VMEM((2, 2 * n_chunks, SSM_STATE_COLS), F32)],
        compiler_params=_params(2),
        name="ssm",
    )(u, m_intra, m_state, m_in_r, m_in_i, ap_r, ap_i, d_skip.reshape(SSM_BLOCKS, 1, LANES))


def _finish(x_ref, gate_ref, out, nxt, xo_ref, h_outs, stage_scr, stage2_scr):
    xn = x_ref[0] + gate_ref[0] * out
    xo_ref[0] = xn
    if nxt is None:
        return
    nw_ref, sc_ref, sh_ref = nxt
    hn = _adaln(xn, nw_ref[...], sc_ref[0], sh_ref[0])
    h_outs[0][0] = hn.astype(BF16)
    if len(h_outs) == 1:
        return
    slabs = D_MODEL // LANES
    d1, d2 = ATTN_PATTERNS[1][1], ATTN_PATTERNS[2][1]
    assert d2 == d1 * d1 and len(h_outs) == 3
    n1, n2 = ROW_TILE // d1, ROW_TILE // d2
    for l in range(slabs):
        stage_scr[l] = hn[:, l * LANES:(l + 1) * LANES]
    for r in range(d1):
        parts = [stage_scr[l, pl.ds(r, n1, stride=d1), :] for l in range(slabs)]
        h_outs[1][0, r] = jnp.concatenate(parts, axis=-1).astype(BF16)
        for l in range(slabs):
            stage2_scr[r * slabs + l] = parts[l]
    for r in range(d1):
        for a in range(d1):
            parts = [stage2_scr[r * slabs + l, pl.ds(a, n2, stride=d1), :] for l in range(slabs)]
            h_outs[2][0, a * d1 + r] = jnp.concatenate(parts, axis=-1).astype(BF16)


def _even_out_kernel(*refs, n_h):
    x_ref, ya_ref, ys_ref, bz_ref, gw_ref, gb_ref, wa_ref, ws_ref, gate_ref = refs[:9]
    nxt, xo_ref, h_outs, stages = _tail_refs(refs[9:], n_h)
    y = ys_ref[0]
    y = 0.5 * y * (1.0 + lax.erf(y * (2.0 ** -0.5)))
    y = y * jax.nn.sigmoid(_dot(y.astype(BF16), gw_ref[...]) + gb_ref[...])
    y = y * jax.nn.silu(bz_ref[0])
    out = _dot(ya_ref[0], wa_ref[...]) + _dot(y.astype(BF16), ws_ref[...])
    _finish(x_ref, gate_ref, out, nxt, xo_ref, h_outs, *stages)


def _odd_out_kernel(*refs, n_h):
    x_ref, h_ref, o_ref, wz_ref, wo_ref, gate_ref = refs[:6]
    nxt, xo_ref, h_outs, stages = _tail_refs(refs[6:], n_h)
    z = _dot(h_ref[0], wz_ref[...])
    g = o_ref[0] * jax.nn.silu(z)
    out = _dot(g.astype(BF16), wo_ref[...])
    _finish(x_ref, gate_ref, out, nxt, xo_ref, h_outs, *stages)


def _tail_refs(refs, n_h):
    if n_h == 0:
        return None, refs[0], (), (None, None)
    nxt, xo_ref, h_outs = refs[:3], refs[3], refs[4:4 + n_h]
    return nxt, xo_ref, h_outs, (tuple(refs[4 + n_h:6 + n_h]) if n_h > 1 else (None, None))


def _tail_call(body, name, x, rows, consts, gate, nxt, permuted):
    bsz, seq, _ = x.shape
    n_h = 0 if nxt is None else (N_PATTERNS if permuted else 1)
    row = lambda a: pl.BlockSpec((1, ROW_TILE, a.shape[-1]), lambda b, i: (b, i, 0))
    const = lambda a: pl.BlockSpec(a.shape, lambda b, i: (0,) * a.ndim)
    vec = pl.BlockSpec((1, 1, D_MODEL), lambda b, i: (b, 0, 0))
    args = [x, *rows, *consts, gate]
    in_specs = [row(x)] + [row(a) for a in rows] + [const(a) for a in consts] + [vec]
    out_specs = [row(x)]
    out_shape = [jax.ShapeDtypeStruct(x.shape, F32)]
    scratch = []
    if n_h:
        nw, sc, sh = nxt
        args += [nw, sc, sh]
        in_specs += [const(nw), vec, vec]
        out_specs.append(row(x))
        out_shape.append(jax.ShapeDtypeStruct(x.shape, BF16))
    if n_h > 1:
        for _, dil in ATTN_PATTERNS[1:]:
            out_specs.append(pl.BlockSpec((1, dil, ROW_TILE // dil, D_MODEL), lambda b, i: (b, 0, i, 0)))
            out_shape.append(jax.ShapeDtypeStruct((bsz, dil, seq // dil, D_MODEL), BF16))
        d1 = ATTN_PATTERNS[1][1]
        scratch.append(pltpu.VMEM((D_MODEL // LANES, ROW_TILE, LANES), F32))
        scratch.append(pltpu.VMEM((d1 * D_MODEL // LANES, ROW_TILE // d1, LANES), F32))
    res = pl.pallas_call(
        functools.partial(body, n_h=n_h),
        grid=(bsz, seq // ROW_TILE),
        in_specs=in_specs, out_specs=out_specs, out_shape=out_shape, scratch_shapes=scratch,
        compiler_params=_params(2),
        name=name,
    )(*args)
    return res[0], [h.reshape(x.shape) for h in res[1:]]


def _attn_kernel(h0_ref, h1_ref, h2_ref, *refs, seq):
    w_refs = refs[:N_PATTERNS]
    (t0_ref, t1_ref, t2_ref, qw_ref, kw_ref, o_ref,
     q_scr, k_scr, v_scr, on_scr, lse_scr, bias_scr) = refs[N_PATTERNS:]
    wk = WINDOW_KEYS
    n_chunks = seq // PROJ_ROWS
    chunk_blocks = PROJ_ROWS // wk
    h_refs = (h0_ref, h1_ref, h2_ref)
    tabs = (t0_ref, t1_ref, t2_ref)

    qi = lax.broadcasted_iota(jnp.int32, (wk, 2 * wk), 0)
    kj = lax.broadcasted_iota(jnp.int32, (wk, 2 * wk), 1)
    band = (kj >= qi) & (kj <= qi + wk)
    bias_scr[0] = jnp.where(band, 0.0, NEG_INF)
    bias_scr[1] = jnp.where(band & (kj >= wk), 0.0, NEG_INF)

    def chunk_rows(c):
        start = c * PROJ_ROWS
        if not isinstance(c, int):
            start = pl.multiple_of(start, PROJ_ROWS)
        return pl.ds(start, PROJ_ROWS)

    def proj_dots(c):
        rows = chunk_rows(c)
        return [_dot(h_refs[g][0, rows, :], w_refs[g][...]) for g in range(N_PATTERNS)]

    def proj_store(c, prs):
        rows = chunk_rows(c)
        for g, pr in enumerate(prs):
            tab = tabs[g][0, rows, :]
            swapped = pltpu.roll(tab, HEAD_DIM // 2, 1)
            lower = lax.broadcasted_iota(jnp.int32, tab.shape, 1) < HEAD_DIM // 2
            cos = jnp.where(lower, tab, swapped)
            sin = jnp.where(lower, -swapped, tab)

            def norm_rope(t, w):
                t = t * lax.rsqrt(jnp.mean(t * t, axis=-1, keepdims=True) + EPS) * w
                return t * cos + pltpu.roll(t, HEAD_DIM // 2, 1) * sin

            for hl in range(HEAD_PAIR):
                off = hl * 3 * HEAD_DIM
                slot = g * HEAD_PAIR + hl
                q_scr[slot, rows, :] = (norm_rope(pr[:, off:off + HEAD_DIM], qw_ref[...]) * ATTN_SCALE).astype(BF16)
                k_scr[slot, rows, :] = norm_rope(pr[:, off + HEAD_DIM:off + 2 * HEAD_DIM], kw_ref[...]).astype(BF16)
                v_scr[slot, rows, :] = pr[:, off + 2 * HEAD_DIM:off + 3 * HEAD_DIM].astype(BF16)

    def block_ids(c):
        return [(g, hl, chunk_blocks * c + u, u) for g in range(N_PATTERNS) for hl in range(HEAD_PAIR)
                for u in range(chunk_blocks)]

    def is_first(g, j, u):
        n_blk = seq // ATTN_PATTERNS[g][1] // wk
        if n_blk == 1:
            return True
        if u % 2 == 1:
            return False
        if isinstance(j, int):
            return j % n_blk == 0
        return (j % n_blk) == 0

    def block_start(j):
        start = j * wk
        return start if isinstance(start, int) else pl.multiple_of(start, wk)

    def scores(g, hl, j, u):
        first = is_first(g, j, u)
        slot = g * HEAD_PAIR + hl
        q = q_scr[slot, pl.ds(block_start(j), wk), :]
        if first is True:
            kk = k_scr[slot, pl.ds(block_start(j), wk), :]
            bias = bias_scr[1, :, wk:]
        else:
            kk = k_scr[slot, pl.ds(block_start(j - 1), 2 * wk), :]
            bias = bias_scr[0] if first is False else bias_scr[jnp.where(first, 1, 0)]
        return lax.dot_general(q, kk, (((1,), (1,)), ((), ())), preferred_element_type=F32) + bias

    def softmax(s):
        m = jnp.max(s, axis=-1, keepdims=True)
        p = jnp.exp(s - m)
        den = jnp.sum(p, axis=-1, keepdims=True)
        return p.astype(BF16), m, den

    def weighted(g, hl, j, u, p):
        slot = g * HEAD_PAIR + hl
        if is_first(g, j, u) is True:
            vv = v_scr[slot, pl.ds(block_start(j), wk), :]
        else:
            vv = v_scr[slot, pl.ds(block_start(j - 1), 2 * wk), :]
        return _dot(p, vv)

    def attn_store(g, hl, j, o, m, den):
        dil = ATTN_PATTERNS[g][1]
        n_blk = seq // dil // wk
        slot = g * HEAD_PAIR + hl
        if dil == 1:
            nat = pl.ds(block_start(j), wk)
        else:
            nat = pl.ds((j % n_blk) * (wk * dil) + j // n_blk, wk, stride=dil)
        on_scr[slot, nat, :] = o * (1.0 / den)
        lse_scr[slot, nat, :] = jnp.broadcast_to(m + jnp.log(den), (wk, HEAD_DIM))

    def step(c_attn, c_proj):
        ids = block_ids(c_attn) if c_attn is not None else []
        ss = [scores(g, hl, j, u) for g, hl, j, u in ids]
        prs = proj_dots(c_proj) if c_proj is not None else None
        sm = [softmax(s) for s in ss]
        os_ = [weighted(g, hl, j, u, p) for (g, hl, j, u), (p, _, _) in zip(ids, sm)]
        if prs is not None:
            proj_store(c_proj, prs)
        for (g, hl, j, u), o, (_, m, den) in zip(ids, os_, sm):
            attn_store(g, hl, j, o, m, den)

    step(None, 0)
    step(0, 1)

    def body(c, carry):
        step(c - 1, c)
        return carry

    lax.fori_loop(2, n_chunks, body, 0)
    step(n_chunks - 1, None)

    def merge_body(c, carry):
        rows = pl.ds(pl.multiple_of(c * PROJ_ROWS, PROJ_ROWS), PROJ_ROWS)
        for hl in range(HEAD_PAIR):
            ls = [lse_scr[g * HEAD_PAIR + hl, rows, :] for g in range(N_PATTERNS)]
            mx = jnp.maximum(jnp.maximum(ls[0], ls[1]), ls[2])
            num = jnp.zeros((PROJ_ROWS, HEAD_DIM), F32)
            den = jnp.zeros((PROJ_ROWS, HEAD_DIM), F32)
            for g in range(N_PATTERNS):
                wgt = jnp.exp(ls[g] - mx)
                num = num + wgt * on_scr[g * HEAD_PAIR + hl, rows, :]
                den = den + wgt
            o_ref[0, rows, hl * HEAD_DIM:(hl + 1) * HEAD_DIM] = (num / den).astype(BF16)
        return carry

    lax.fori_loop(0, seq // PROJ_ROWS, merge_body, 0)


def _attention(hs, w_qkv, tables, q_norm_w, k_norm_w):
    bsz, seq, _ = hs[0].shape
    pair_cols = HEAD_PAIR * 3 * HEAD_DIM
    wspec = lambda g: pl.BlockSpec((D_MODEL, pair_cols), lambda b, hp: (0, g * (HEADS // HEAD_PAIR) + hp))
    slots = N_PATTERNS * HEAD_PAIR
    stat = pltpu.VMEM((slots, seq, HEAD_DIM), F32)
    qkv = pltpu.VMEM((slots, seq, HEAD_DIM), BF16)
    once = pl.Buffered(1)
    hspec = pl.BlockSpec((1, seq, D_MODEL), lambda b, hp: (b, 0, 0), pipeline_mode=once)
    tspec = pl.BlockSpec((1, seq, HEAD_DIM), lambda b, hp: (b, 0, 0), pipeline_mode=once)
    vec = pl.BlockSpec((1, HEAD_DIM), lambda b, hp: (0, 0))
    return pl.pallas_call(
        functools.partial(_attn_kernel, seq=seq),
        grid=(bsz, HEADS // HEAD_PAIR),
        in_specs=[hspec, hspec, hspec, *[wspec(g) for g in range(N_PATTERNS)],
                  tspec, tspec, tspec, vec, vec],
        out_specs=pl.BlockSpec((1, seq, HEAD_PAIR * HEAD_DIM), lambda b, hp: (b, 0, hp)),
        out_shape=jax.ShapeDtypeStruct((bsz, seq, HEADS * HEAD_DIM), BF16),
        scratch_shapes=[qkv, qkv, qkv, stat, stat,
                        pltpu.VMEM((2, WINDOW_KEYS, 2 * WINDOW_KEYS), F32)],
        compiler_params=_params(2, ATTN_VMEM_LIMIT),
        name="attention",
    )(*hs, *[w_qkv] * N_PATTERNS, *tables, q_norm_w.reshape(1, HEAD_DIM), k_norm_w.reshape(1, HEAD_DIM))


def _qkv_cast_kernel(q_ref, k_ref, v_ref, o_ref):
    for part, ref in enumerate((q_ref, k_ref, v_ref)):
        o_ref[:, part * HEAD_DIM:(part + 1) * HEAD_DIM] = ref[...].astype(BF16)


def _pair_major_qkv(w_all, layer):
    n_heads = N_PATTERNS * HEADS
    part = lambda p: pl.BlockSpec((None, D_MODEL, HEAD_DIM), lambda j: (layer, 0, p * n_heads + j))
    return pl.pallas_call(
        _qkv_cast_kernel,
        grid=(n_heads,),
        in_specs=[part(0), part(1), part(2)],
        out_specs=pl.BlockSpec((D_MODEL, 3 * HEAD_DIM), lambda j: (0, j)),
        out_shape=jax.ShapeDtypeStruct((D_MODEL, 3 * ATTN_QKV), BF16),
        compiler_params=_params(1),
        name="qkv_weights",
    )(w_all, w_all, w_all)


def kernel(x, c, positions, mod_w, mod_b, norm_w, even_w_in, conv_dw_w, conv_dw_b, conv_ln_w, conv_ln_b, ssm_lam_re, ssm_lam_im, ssm_log_dt, ssm_b_re, ssm_b_im, ssm_c_re, ssm_c_im, ssm_d, ssm_glu_w, ssm_glu_b, even_w_out, attn_w_in, attn_q_norm_w, attn_k_norm_w, attn_w_out):
    bsz, seq, _ = x.shape
    assert seq == 2048 and x.shape[-1] == D_MODEL and bsz % SSM_BATCH == 0
    mod = _modulation(c, mod_w, mod_b)
    shift = [mod[l, :, None, :D_MODEL] for l in range(DEPTH)]
    scale = [mod[l, :, None, D_MODEL:2 * D_MODEL] for l in range(DEPTH)]
    gate = [mod[l, :, None, 2 * D_MODEL:] for l in range(DEPTH)]
    nw = [norm_w[l].reshape(1, D_MODEL) for l in range(DEPTH)]
    assert DEPTH % 2 == 0
    tables = _rope_tables(positions)
    ssm_mats = jax.vmap(_ssm_matrices)(ssm_lam_re, ssm_lam_im, ssm_log_dt, ssm_b_re, ssm_b_im, ssm_c_re, ssm_c_im)

    hs = [_first_norm(x, nw[0], scale[0], shift[0])]
    for layer in range(DEPTH):
        i = layer // 2
        nxt = (nw[layer + 1], scale[layer + 1], shift[layer + 1]) if layer + 1 < DEPTH else None
        if layer % 2 == 0:
            ya, u, bz = _even_in(hs[0], even_w_in[i].astype(BF16), conv_dw_w[i], conv_dw_b[i],
                                 conv_ln_w[i], conv_ln_b[i])
            ys = _ssm(u, ssm_mats, i, ssm_d[i])
            w_out = even_w_out[i].astype(BF16)
            x, hs = _tail_call(_even_out_kernel, "even_out", x, [ya, ys, bz],
                               [ssm_glu_w[i].astype(BF16), ssm_glu_b[i].reshape(1, SSM_WIDTH),
                                w_out[:CONV_WIDTH], w_out[CONV_WIDTH:]], gate[layer], nxt, permuted=True)
        else:
            o = _attention(hs, _pair_major_qkv(attn_w_in, i), tables, attn_q_norm_w[i], attn_k_norm_w[i])
            x, hs = _tail_call(_odd_out_kernel, "odd_out", x, [hs[0], o],
                               [attn_w_in[i][:, 3 * ATTN_QKV:].astype(BF16), attn_w_out[i].astype(BF16)],
                               gate[layer], nxt, permuted=False)
    return x
```

```python
import functools
import math

import jax
import jax.numpy as jnp
from jax import lax
from jax.experimental import pallas as pl
from jax.experimental.pallas import tpu as pltpu

F32 = jnp.float32
BF16 = jnp.bfloat16

D_MODEL = 1024
DEPTH = 4
CONV_WIDTH = 512
CONV_KERNEL = 31
SSM_WIDTH = 512
SSM_GROUP = 16
SSM_GROUPS = 32
SSM_STATE = 64
HEAD_DIM = 128
ATTN_PATTERNS = ((128, 1), (512, 4), (2048, 16))
N_PATTERNS = 3
HEADS = 8
ATTN_QKV = N_PATTERNS * HEADS * HEAD_DIM
ATTN_SCALE = HEAD_DIM ** -0.5
ROPE_THETA = 10000.0
EPS = 1e-6
NEG_INF = -1e30

LANES = 128
WINDOW_KEYS = 128
SSM_CHUNK = 16
SSM_LANE_GROUPS = LANES // SSM_GROUP
SSM_BLOCKS = SSM_WIDTH // LANES
SSM_STATE_COLS = SSM_LANE_GROUPS * SSM_STATE
SSM_BATCH = 2
ROW_TILE = 512
CONV_HALO = 32
CONV_ROWS = 64
PROJ_ROWS = 256
VMEM_LIMIT = 56 * 1024 * 1024
ATTN_VMEM_LIMIT = 58 * 1024 * 1024
HEAD_PAIR = 2


def _params(n_axes, vmem=VMEM_LIMIT):
    return pltpu.CompilerParams(dimension_semantics=("arbitrary",) * n_axes, vmem_limit_bytes=vmem)


def _adaln(x, nw, scale, shift):
    ms = jnp.mean(x * x, axis=-1, keepdims=True)
    return (x * lax.rsqrt(ms + EPS) * nw) * (1.0 + scale) + shift


def _dot(a, b):
    return jnp.dot(a, b, preferred_element_type=F32)


def _mod_kernel(c_ref, w_ref, b_ref, o_ref):
    o_ref[0] = _dot(c_ref[...].astype(BF16), w_ref[0].astype(BF16)) + b_ref[0]


def _modulation(c, mod_w, mod_b):
    bsz = c.shape[0]
    nblk = 3 * D_MODEL // D_MODEL
    return pl.pallas_call(
        _mod_kernel,
        grid=(DEPTH, nblk),
        in_specs=[pl.BlockSpec((bsz, D_MODEL), lambda l, j: (0, 0)),
                  pl.BlockSpec((1, D_MODEL, D_MODEL), lambda l, j: (l, 0, j)),
                  pl.BlockSpec((1, 1, D_MODEL), lambda l, j: (l, 0, j))],
        out_specs=pl.BlockSpec((1, bsz, D_MODEL), lambda l, j: (l, 0, j)),
        out_shape=jax.ShapeDtypeStruct((DEPTH, bsz, 3 * D_MODEL), F32),
        compiler_params=_params(2),
        name="modulation",
    )(c, mod_w, mod_b.reshape(DEPTH, 1, 3 * D_MODEL))


def _rope_kernel(pos_ref, inv_ref, tab_ref, tab1_ref, tab2_ref, stage_scr, stage2_scr):
    half_rows, half_lanes = ROW_TILE // 2, HEAD_DIM // 2
    pos = pos_ref[0].astype(F32)
    lower = lax.broadcasted_iota(jnp.int32, (half_rows, HEAD_DIM), 1) < half_lanes
    ang = jnp.where(lower, pos[:half_rows], pos[half_rows:]) * inv_ref[...]
    cos, sin = jnp.cos(ang), jnp.sin(ang)
    tab = jnp.concatenate([jnp.where(lower, cos, pltpu.roll(sin, half_lanes, 1)),
                           jnp.where(lower, pltpu.roll(cos, half_lanes, 1), sin)], axis=0)
    tab_ref[0] = tab
    d1, d2 = ATTN_PATTERNS[1][1], ATTN_PATTERNS[2][1]
    assert d2 == d1 * d1
    n1, n2 = ROW_TILE // d1, ROW_TILE // d2
    stage_scr[...] = tab
    for r in range(d1):
        part = stage_scr[pl.ds(r, n1, stride=d1), :]
        tab1_ref[0, r] = part
        stage2_scr[r] = part
    for r in range(d1):
        for a in range(d1):
            tab2_ref[0, a * d1 + r] = stage2_scr[r, pl.ds(a, n2, stride=d1), :]


def _rope_tables(positions):
    bsz, seq = positions.shape
    inv = ROPE_THETA ** (-jnp.arange(0, HEAD_DIM, 2, dtype=F32) / HEAD_DIM)
    inv2 = jnp.concatenate([inv, inv]).reshape(1, HEAD_DIM)
    d1, d2 = ATTN_PATTERNS[1][1], ATTN_PATTERNS[2][1]
    perm = lambda d: (pl.BlockSpec((1, d, ROW_TILE // d, HEAD_DIM), lambda b, i: (b, 0, i, 0)),
                      jax.ShapeDtypeStruct((bsz, d, seq // d, HEAD_DIM), F32))
    specs, shapes = zip((pl.BlockSpec((1, ROW_TILE, HEAD_DIM), lambda b, i: (b, i, 0)),
                         jax.ShapeDtypeStruct((bsz, seq, HEAD_DIM), F32)), perm(d1), perm(d2))
    tabs = pl.pallas_call(
        _rope_kernel,
        grid=(bsz, seq // ROW_TILE),
        in_specs=[pl.BlockSpec((1, ROW_TILE, 1), lambda b, i: (b, i, 0)),
                  pl.BlockSpec((1, HEAD_DIM), lambda b, i: (0, 0))],
        out_specs=list(specs),
        out_shape=list(shapes),
        scratch_shapes=[pltpu.VMEM((ROW_TILE, HEAD_DIM), F32), pltpu.VMEM((d1, ROW_TILE // d1, HEAD_DIM), F32)],
        compiler_params=_params(2),
        name="rope_tables",
    )(positions.reshape(bsz, seq, 1), inv2)
    return [t.reshape(bsz, seq, HEAD_DIM) for t in tabs]


def _even_in_kernel(*refs, from_x):
    if from_x:
        x_ref, nw_ref, sc_ref, sh_ref = refs[:4]
        h = _adaln(x_ref[0], nw_ref[...], sc_ref[0], sh_ref[0]).astype(BF16)
        refs = refs[4:]
    else:
        h = refs[0][0]
        refs = refs[1:]
    w_ref, dww_ref, dwb_ref, lnw_ref, lnb_ref, ya_ref, u_ref, bz_ref, conv_scr, y_scr, az_scr = refs
    i = pl.program_id(1)
    cw = CONV_WIDTH
    slabs = cw // LANES

    @pl.when(i == 0)
    def _():
        conv_scr[:, 0:CONV_HALO, :] = jnp.zeros((slabs, CONV_HALO, LANES), F32)

    a1 = _dot(h, w_ref[:, 0:cw])
    a2 = _dot(h, w_ref[:, cw:2 * cw])
    a = a1 * jax.nn.sigmoid(a2)
    for l in range(slabs):
        conv_scr[l, CONV_HALO:CONV_HALO + ROW_TILE, :] = a[:, l * LANES:(l + 1) * LANES]
    az_scr[...] = jax.nn.silu(_dot(h, w_ref[:, 2 * cw:3 * cw]))
    u_ref[0] = _dot(h, w_ref[:, 3 * cw:4 * cw])
    bz_ref[0] = _dot(h, w_ref[:, 4 * cw:5 * cw])

    first_tap = CONV_HALO - (CONV_KERNEL - 1)
    half = CONV_ROWS // 2

    def conv_block(r, carry):
        r0 = r * CONV_ROWS
        for l in range(slabs):
            lanes = slice(l * LANES, (l + 1) * LANES)
            for par in range(2):
                acc = jnp.broadcast_to(dwb_ref[:, lanes], (half, LANES))
                for k in range(CONV_KERNEL):
                    tap = conv_scr[l, pl.ds(r0 + first_tap + par + k, half, stride=2), :]
                    acc = acc + dww_ref[k:k + 1, lanes] * tap
                y_scr[l, pl.ds(r0 + par, half, stride=2), :] = acc
        return carry

    lax.fori_loop(0, ROW_TILE // CONV_ROWS, conv_block, 0)
    for l in range(slabs):
        conv_scr[l, 0:CONV_HALO, :] = conv_scr[l, ROW_TILE:ROW_TILE + CONV_HALO, :]

    def norm_block(r, carry):
        rows = pl.ds(pl.multiple_of(r * CONV_ROWS, CONV_ROWS), CONV_ROWS)
        acc = jnp.concatenate([y_scr[l, rows, :] for l in range(slabs)], axis=-1)
        mu = jnp.mean(acc, axis=-1, keepdims=True)
        xc = acc - mu
        y = xc * lax.rsqrt(jnp.mean(xc * xc, axis=-1, keepdims=True) + EPS)
        y = y * lnw_ref[...] + lnb_ref[...]
        ya_ref[0, rows, :] = (jax.nn.silu(y) * az_scr[rows, :]).astype(BF16)
        return carry

    lax.fori_loop(0, ROW_TILE // CONV_ROWS, norm_block, 0, unroll=4)


def _even_in(h, w_in, dw_w, dw_b, ln_w, ln_b, norm=None):
    bsz, seq, _ = h.shape
    cw = CONV_WIDTH
    const = lambda shape: pl.BlockSpec(shape, lambda b, i: (0,) * len(shape))
    row = lambda width: pl.BlockSpec((1, ROW_TILE, width), lambda b, i: (b, i, 0))
    vec = pl.BlockSpec((1, 1, D_MODEL), lambda b, i: (b, 0, 0))
    norm_args = [] if norm is None else list(norm)
    norm_specs = [] if norm is None else [const((1, D_MODEL)), vec, vec]
    return pl.pallas_call(
        functools.partial(_even_in_kernel, from_x=norm is not None),
        grid=(bsz, seq // ROW_TILE),
        in_specs=[row(D_MODEL), *norm_specs, const(w_in.shape), const((CONV_KERNEL, cw)),
                  const((1, cw)), const((1, cw)), const((1, cw))],
        out_specs=[row(cw), row(cw), row(cw)],
        out_shape=[jax.ShapeDtypeStruct((bsz, seq, cw), BF16),
                   jax.ShapeDtypeStruct((bsz, seq, cw), F32),
                   jax.ShapeDtypeStruct((bsz, seq, cw), F32)],
        scratch_shapes=[pltpu.VMEM((cw // LANES, CONV_HALO + ROW_TILE, LANES), F32),
                        pltpu.VMEM((cw // LANES, ROW_TILE, LANES), F32),
                        pltpu.VMEM((ROW_TILE, cw), F32)],
        compiler_params=_params(2),
        name="even_in",
    )(h, *norm_args, w_in, dw_w, dw_b.reshape(1, cw), ln_w.reshape(1, cw), ln_b.reshape(1, cw))


def _ssm_matrices(lam_re, lam_im, log_dt, b_re, b_im, c_re, c_im):
    hp = lax.Precision.HIGHEST
    t = SSM_CHUNK
    lr, li = lam_re.astype(F32), lam_im.astype(F32)
    dt = jnp.exp(log_dt.astype(F32))[:, None]

    def a_pow(k):
        kk = k.astype(F32)[:, None, None]
        mag = jnp.exp(kk * (lr * dt)[None])
        ang = kk * (li * dt)[None]
        return mag * jnp.cos(ang), mag * jnp.sin(ang)

    ar, ai = a_pow(jnp.ones((1,), F32))
    ar, ai = ar[0], ai[0]
    den = lr * lr + li * li
    nr = ar - 1.0
    kr = (nr * lr + ai * li) / den
    ki = (ai * lr - nr * li) / den
    br, bi = b_re.astype(F32), b_im.astype(F32)
    bbr = kr[..., None] * br - ki[..., None] * bi
    bbi = kr[..., None] * bi + ki[..., None] * br
    cr, ci = c_re.astype(F32), c_im.astype(F32)

    pr, pi = a_pow(jnp.arange(t + 1))
    wr = pr[:t, :, :, None] * bbr[None] - pi[:t, :, :, None] * bbi[None]
    wi = pr[:t, :, :, None] * bbi[None] + pi[:t, :, :, None] * bbr[None]
    kk = (jnp.einsum('gop,kgpi->kgoi', cr, wr, precision=hp)
          - jnp.einsum('gop,kgpi->kgoi', ci, wi, precision=hp))

    nb, gl, hh, pp = SSM_BLOCKS, SSM_LANE_GROUPS, SSM_GROUP, SSM_STATE
    grp = jnp.arange(gl)[:, None, None]
    row = jnp.arange(t * hh)[None, :, None]
    col = jnp.arange(t * LANES)[None, None, :]
    place = ((row // hh == col // LANES) & (row % hh == col % hh) & ((col % LANES) // hh == grp)).astype(BF16)
    prow = jnp.arange(pp)[None, :, None]
    pcol = jnp.arange(gl * pp)[None, None, :]
    spread = ((pcol % pp == prow) & (pcol // pp == grp)).astype(BF16)

    def rows_sgi(m):
        n = m.shape[-1]
        return m.reshape(nb, gl, t, hh, n).transpose(0, 2, 1, 3, 4).reshape(nb, t * LANES, n)

    lrow = jnp.arange(t * hh)[None, :, None]
    lcol = jnp.arange(t * hh)[None, None, :]
    s_idx = jnp.arange(t)[:, None, None]
    shift = ((lcol // hh == lrow // hh + s_idx) & (lcol % hh == lrow % hh)).astype(BF16)
    kk_i = kk.reshape(t, nb, gl, hh, hh).transpose(1, 2, 4, 0, 3).reshape(nb, gl, hh, t * hh)
    toe = jnp.einsum('cgil,sln->cgsin', kk_i.astype(BF16), shift,
                     preferred_element_type=BF16).reshape(nb, gl, t * hh, t * hh)
    m_intra = rows_sgi(jnp.einsum('cgab,gbn->cgan', toe, place, preferred_element_type=BF16))

    def state_cols(w):
        w = w[::-1].reshape(t, nb, gl, pp, hh).transpose(1, 2, 0, 4, 3).reshape(nb, gl, t * hh, pp)
        return rows_sgi(jnp.einsum('cgap,gpn->cgan', w.astype(BF16), spread, preferred_element_type=BF16))

    m_state = jnp.concatenate([state_cols(wr), state_cols(wi)], axis=-1)

    def in_rows(q):
        q = q.reshape(t, nb, gl, hh, pp).transpose(1, 2, 4, 0, 3).reshape(nb, gl, pp, t * hh)
        return jnp.einsum('cgpb,gbn->cgpn', q.astype(BF16), place,
                          preferred_element_type=BF16).reshape(nb, gl * pp, t * LANES)

    m_in_r = in_rows(pr[1:, :, None, :] * cr[None] - pi[1:, :, None, :] * ci[None])
    m_in_i = in_rows(-(pr[1:, :, None, :] * ci[None] + pi[1:, :, None, :] * cr[None]))

    n_lvl = int(math.log2(2048 // t))
    sr2, si2 = a_pow(t * (2 ** jnp.arange(n_lvl)))
    ap_r = sr2.reshape(n_lvl, nb, gl * pp).transpose(1, 0, 2)
    ap_i = si2.reshape(n_lvl, nb, gl * pp).transpose(1, 0, 2)
    return m_intra, m_state, m_in_r, m_in_i, ap_r, ap_i


def _ssm_kernel(u_ref, mintra_ref, mstate_ref, minr_ref, mini_ref, apr_ref, api_ref, d_ref, y_ref,
                x_scr, yi_scr, cr_scr, ci_scr, zr_scr, zi_scr, *, n_chunks, n_levels):
    t = SSM_CHUNK
    sc = SSM_STATE_COLS
    pair = 2 * LANES
    for bb in range(SSM_BATCH):
        for t0 in range(t):
            x_scr[bb * n_chunks:(bb + 1) * n_chunks, t0 * LANES:(t0 + 1) * LANES] = (
                u_ref[bb, pl.ds(t0, n_chunks, stride=t), :].astype(BF16))
    s_loc = _dot(x_scr[...], mstate_ref[0])
    for tp in range(t // 2):
        cols = slice(tp * pair, (tp + 1) * pair)
        yi_scr[:, cols] = _dot(x_scr[:, 0:(tp + 1) * pair], mintra_ref[0, 0:(tp + 1) * pair, cols])

    zero = jnp.zeros((n_chunks, sc), F32)
    for p in range(2):
        zr_scr[p, 0:n_chunks, :] = zero
        zi_scr[p, 0:n_chunks, :] = zero
    for bb in range(SSM_BATCH):
        rows = slice(bb * n_chunks, (bb + 1) * n_chunks)
        zr_scr[0, n_chunks:, :] = s_loc[rows, :sc]
        zi_scr[0, n_chunks:, :] = s_loc[rows, sc:]
        for k in range(n_levels):
            src, dst = k % 2, 1 - (k % 2)
            sh = n_chunks - (1 << k)
            zr = zr_scr[src, n_chunks:, :]
            zi = zi_scr[src, n_chunks:, :]
            pr = zr_scr[src, sh:sh + n_chunks, :]
            pi = zi_scr[src, sh:sh + n_chunks, :]
            ar = apr_ref[0, k:k + 1, :]
            ai = api_ref[0, k:k + 1, :]
            zr_scr[dst, n_chunks:, :] = zr + ar * pr - ai * pi
            zi_scr[dst, n_chunks:, :] = zi + ar * pi + ai * pr
        fin = n_levels % 2
        cr_scr[rows, :] = zr_scr[fin, n_chunks - 1:2 * n_chunks - 1, :].astype(BF16)
        ci_scr[rows, :] = zi_scr[fin, n_chunks - 1:2 * n_chunks - 1, :].astype(BF16)

    carry_r = cr_scr[...]
    carry_i = ci_scr[...]
    for tp in range(t // 2):
        cols = slice(tp * pair, (tp + 1) * pair)
        yc = yi_scr[:, cols] + _dot(carry_r, minr_ref[0, :, cols]) + _dot(carry_i, mini_ref[0, :, cols])
        for bb in range(SSM_BATCH):
            for j in range(2):
                y_ref[bb, pl.ds(2 * tp + j, n_chunks, stride=t), :] = (
                    yc[bb * n_chunks:(bb + 1) * n_chunks, j * LANES:(j + 1) * LANES])
    for bb in range(SSM_BATCH):
        y_ref[bb] = y_ref[bb] + d_ref[0] * u_ref[bb]


def _ssm(u, mats, layer, d_skip):
    m_intra, m_state, m_in_r, m_in_i, ap_r, ap_i = mats
    bsz, seq, _ = u.shape
    n_chunks = seq // SSM_CHUNK
    n_levels = ap_r.shape[2]
    nb = SSM_BATCH
    once = pl.Buffered(1)
    blk = pl.BlockSpec((nb, seq, LANES), lambda c, b: (b, 0, c))
    mat = lambda m: pl.BlockSpec((None, 1) + m.shape[2:], lambda c, b: (layer, c, 0, 0), pipeline_mode=once)
    lvl = pl.BlockSpec((None, 1, n_levels, SSM_STATE_COLS), lambda c, b: (layer, c, 0, 0))
    return pl.pallas_call(
        functools.partial(_ssm_kernel, n_chunks=n_chunks, n_levels=n_levels),
        grid=(SSM_BLOCKS, bsz // nb),
        in_specs=[blk, mat(m_intra), mat(m_state), mat(m_in_r), mat(m_in_i), lvl, lvl,
                  pl.BlockSpec((1, 1, LANES), lambda c, b: (c, 0, 0))],
        out_specs=blk,
        out_shape=jax.ShapeDtypeStruct(u.shape, F32),
        scratch_shapes=[pltpu.VMEM((nb * n_chunks, SSM_CHUNK * LANES), BF16),
                        pltpu.VMEM((nb * n_chunks, SSM_CHUNK * LANES), F32),
                        pltpu.VMEM((nb * n_chunks, SSM_STATE_COLS), BF16),
                        pltpu.VMEM((nb * n_chunks, SSM_STATE_COLS), BF16),
                        pltpu.VMEM((2, 2 * n_chunks, SSM_STATE_COLS), F32),
                        pltpu.VMEM((2, 2 * n_chunks, SSM_STATE_COLS), F32)],
        compiler_params=_params(2),
        name="ssm",
    )(u, m_intra, m_state, m_in_r, m_in_i, ap_r, ap_i, d_skip.reshape(SSM_BLOCKS, 1, LANES))


def _finish(x_ref, gate_ref, out, nxt, xo_ref, h_outs, stage_scr, stage2_scr):
    xn = x_ref[0] + gate_ref[0] * out
    xo_ref[0] = xn
    if nxt is None:
        return
    nw_ref, sc_ref, sh_ref = nxt
    hn = _adaln(xn, nw_ref[...], sc_ref[0], sh_ref[0])
    h_outs[0][0] = hn.astype(BF16)
    if len(h_outs) == 1:
        return
    slabs = D_MODEL // LANES
    d1, d2 = ATTN_PATTERNS[1][1], ATTN_PATTERNS[2][1]
    assert d2 == d1 * d1 and len(h_outs) == 3
    n1, n2 = ROW_TILE // d1, ROW_TILE // d2
    for l in range(slabs):
        stage_scr[l] = hn[:, l * LANES:(l + 1) * LANES]
    for r in range(d1):
        parts = [stage_scr[l, pl.ds(r, n1, stride=d1), :] for l in range(slabs)]
        h_outs[1][0, r] = jnp.concatenate(parts, axis=-1).astype(BF16)
        for l in range(slabs):
            stage2_scr[r * slabs + l] = parts[l]
    for r in range(d1):
        for a in range(d1):
            parts = [stage2_scr[r * slabs + l, pl.ds(a, n2, stride=d1), :] for l in range(slabs)]
            h_outs[2][0, a * d1 + r] = jnp.concatenate(parts, axis=-1).astype(BF16)


def _even_out_kernel(*refs, n_h):
    x_ref, ya_ref, ys_ref, bz_ref, gw_ref, gb_ref, wa_ref, ws_ref, gate_ref = refs[:9]
    nxt, xo_ref, h_outs, stages = _tail_refs(refs[9:], n_h)
    y = ys_ref[0]
    y = 0.5 * y * (1.0 + lax.erf(y * (2.0 ** -0.5)))
    y = y * jax.nn.sigmoid(_dot(y.astype(BF16), gw_ref[...]) + gb_ref[...])
    y = y * jax.nn.silu(bz_ref[0])
    out = _dot(ya_ref[0], wa_ref[...]) + _dot(y.astype(BF16), ws_ref[...])
    _finish(x_ref, gate_ref, out, nxt, xo_ref, h_outs, *stages)


def _odd_out_kernel(*refs, n_h):
    x_ref, h_ref, o_ref, wz_ref, wo_ref, gate_ref = refs[:6]
    nxt, xo_ref, h_outs, stages = _tail_refs(refs[6:], n_h)
    z = _dot(h_ref[0], wz_ref[...].astype(BF16))
    g = o_ref[0] * jax.nn.silu(z)
    out = _dot(g.astype(BF16), wo_ref[...])
    _finish(x_ref, gate_ref, out, nxt, xo_ref, h_outs, *stages)


def _tail_refs(refs, n_h):
    if n_h == 0:
        return None, refs[0], (), (None, None)
    nxt, xo_ref, h_outs = refs[:3], refs[3], refs[4:4 + n_h]
    return nxt, xo_ref, h_outs, (tuple(refs[4 + n_h:6 + n_h]) if n_h > 1 else (None, None))


def _tail_call(body, name, x, rows, consts, gate, nxt, permuted):
    bsz, seq, _ = x.shape
    n_h = 0 if nxt is None else (N_PATTERNS if permuted else 1)
    row = lambda a: pl.BlockSpec((1, ROW_TILE, a.shape[-1]), lambda b, i: (b, i, 0))
    const = lambda a: pl.BlockSpec(a.shape, lambda b, i: (0,) * a.ndim)
    vec = pl.BlockSpec((1, 1, D_MODEL), lambda b, i: (b, 0, 0))
    consts = [a if isinstance(a, tuple) else (a, const(a)) for a in consts]
    args = [x, *rows, *[a for a, _ in consts], gate]
    in_specs = [row(x)] + [row(a) for a in rows] + [spec for _, spec in consts] + [vec]
    out_specs = [row(x)]
    out_shape = [jax.ShapeDtypeStruct(x.shape, F32)]
    scratch = []
    if n_h:
        nw, sc, sh = nxt
        args += [nw, sc, sh]
        in_specs += [const(nw), vec, vec]
        out_specs.append(row(x))
        out_shape.append(jax.ShapeDtypeStruct(x.shape, BF16))
    if n_h > 1:
        for _, dil in ATTN_PATTERNS[1:]:
            out_specs.append(pl.BlockSpec((1, dil, ROW_TILE // dil, D_MODEL), lambda b, i: (b, 0, i, 0)))
            out_shape.append(jax.ShapeDtypeStruct((bsz, dil, seq // dil, D_MODEL), BF16))
        d1 = ATTN_PATTERNS[1][1]
        scratch.append(pltpu.VMEM((D_MODEL // LANES, ROW_TILE, LANES), F32))
        scratch.append(pltpu.VMEM((d1 * D_MODEL // LANES, ROW_TILE // d1, LANES), F32))
    res = pl.pallas_call(
        functools.partial(body, n_h=n_h),
        grid=(bsz, seq // ROW_TILE),
        in_specs=in_specs, out_specs=out_specs, out_shape=out_shape, scratch_shapes=scratch,
        compiler_params=_params(2),
        name=name,
    )(*args)
    return res[0], [h.reshape(x.shape) for h in res[1:]]


def _attn_kernel(h0_ref, h1_ref, h2_ref, *refs, seq):
    w_refs = refs[:N_PATTERNS]
    (t0_ref, t1_ref, t2_ref, qw_ref, kw_ref, o_ref,
     q_scr, k_scr, v_scr, on_scr, lse_scr, bias_scr) = refs[N_PATTERNS:]
    wk = WINDOW_KEYS
    n_chunks = seq // PROJ_ROWS
    chunk_blocks = PROJ_ROWS // wk
    h_refs = (h0_ref, h1_ref, h2_ref)
    tabs = (t0_ref, t1_ref, t2_ref)

    qi = lax.broadcasted_iota(jnp.int32, (wk, 2 * wk), 0)
    kj = lax.broadcasted_iota(jnp.int32, (wk, 2 * wk), 1)
    band = (kj >= qi) & (kj <= qi + wk)
    bias_scr[0] = jnp.where(band, 0.0, NEG_INF)
    bias_scr[1] = jnp.where(band & (kj >= wk), 0.0, NEG_INF)

    def chunk_rows(c):
        start = c * PROJ_ROWS
        if not isinstance(c, int):
            start = pl.multiple_of(start, PROJ_ROWS)
        return pl.ds(start, PROJ_ROWS)

    def proj_dots(c):
        rows = chunk_rows(c)
        return [_dot(h_refs[g][0, rows, :], w_refs[g][...]) for g in range(N_PATTERNS)]

    def proj_store(c, prs):
        rows = chunk_rows(c)
        for g, pr in enumerate(prs):
            tab = tabs[g][0, rows, :]
            swapped = pltpu.roll(tab, HEAD_DIM // 2, 1)
            lower = lax.broadcasted_iota(jnp.int32, tab.shape, 1) < HEAD_DIM // 2
            cos = jnp.where(lower, tab, swapped)
            sin = jnp.where(lower, -swapped, tab)

            def norm_rope(t, w):
                t = t * lax.rsqrt(jnp.mean(t * t, axis=-1, keepdims=True) + EPS) * w
                return t * cos + pltpu.roll(t, HEAD_DIM // 2, 1) * sin

            for hl in range(HEAD_PAIR):
                off = hl * 3 * HEAD_DIM
                slot = g * HEAD_PAIR + hl
                q_scr[slot, rows, :] = (norm_rope(pr[:, off:off + HEAD_DIM], qw_ref[...]) * ATTN_SCALE).astype(BF16)
                k_scr[slot, rows, :] = norm_rope(pr[:, off + HEAD_DIM:off + 2 * HEAD_DIM], kw_ref[...]).astype(BF16)
                v_scr[slot, rows, :] = pr[:, off + 2 * HEAD_DIM:off + 3 * HEAD_DIM].astype(BF16)

    def block_ids(c):
        return [(g, hl, chunk_blocks * c + u, u) for g in range(N_PATTERNS) for hl in range(HEAD_PAIR)
                for u in range(chunk_blocks)]

    def is_first(g, j, u):
        n_blk = seq // ATTN_PATTERNS[g][1] // wk
        if n_blk == 1:
            return True
        if u % 2 == 1:
            return False
        if isinstance(j, int):
            return j % n_blk == 0
        return (j % n_blk) == 0

    def block_start(j):
        start = j * wk
        return start if isinstance(start, int) else pl.multiple_of(start, wk)

    def scores(g, hl, j, u):
        first = is_first(g, j, u)
        slot = g * HEAD_PAIR + hl
        q = q_scr[slot, pl.ds(block_start(j), wk), :]
        if first is True:
            kk = k_scr[slot, pl.ds(block_start(j), wk), :]
            bias = bias_scr[1, :, wk:]
        else:
            kk = k_scr[slot, pl.ds(block_start(j - 1), 2 * wk), :]
            bias = bias_scr[0] if first is False else bias_scr[jnp.where(first, 1, 0)]
        return lax.dot_general(q, kk, (((1,), (1,)), ((), ())), preferred_element_type=F32) + bias

    def softmax(s):
        m = jnp.max(s, axis=-1, keepdims=True)
        p = jnp.exp(s - m)
        den = jnp.sum(p, axis=-1, keepdims=True)
        return p.astype(BF16), m, den

    def weighted(g, hl, j, u, p):
        slot = g * HEAD_PAIR + hl
        if is_first(g, j, u) is True:
            vv = v_scr[slot, pl.ds(block_start(j), wk), :]
        else:
            vv = v_scr[slot, pl.ds(block_start(j - 1), 2 * wk), :]
        return _dot(p, vv)

    def attn_store(g, hl, j, o, m, den):
        dil = ATTN_PATTERNS[g][1]
        n_blk = seq // dil // wk
        slot = g * HEAD_PAIR + hl
        if dil == 1:
            nat = pl.ds(block_start(j), wk)
        else:
            nat = pl.ds((j % n_blk) * (wk * dil) + j // n_blk, wk, stride=dil)
        on_scr[slot, nat, :] = o * (1.0 / den)
        lse_scr[slot, nat, :] = jnp.broadcast_to(m + jnp.log(den), (wk, HEAD_DIM))

    def step(c_attn, c_proj):
        ids = block_ids(c_attn) if c_attn is not None else []
        ss = [scores(g, hl, j, u) for g, hl, j, u in ids]
        prs = proj_dots(c_proj) if c_proj is not None else None
        sm = [softmax(s) for s in ss]
        os_ = [weighted(g, hl, j, u, p) for (g, hl, j, u), (p, _, _) in zip(ids, sm)]
        if prs is not None:
            proj_store(c_proj, prs)
        for (g, hl, j, u), o, (_, m, den) in zip(ids, os_, sm):
            attn_store(g, hl, j, o, m, den)

    step(None, 0)
    step(0, 1)

    def body(c, carry):
        step(c - 1, c)
        return carry

    lax.fori_loop(2, n_chunks, body, 0)
    step(n_chunks - 1, None)

    def merge_body(c, carry):
        rows = pl.ds(pl.multiple_of(c * PROJ_ROWS, PROJ_ROWS), PROJ_ROWS)
        for hl in range(HEAD_PAIR):
            ls = [lse_scr[g * HEAD_PAIR + hl, rows, :] for g in range(N_PATTERNS)]
            mx = jnp.maximum(jnp.maximum(ls[0], ls[1]), ls[2])
            num = jnp.zeros((PROJ_ROWS, HEAD_DIM), F32)
            den = jnp.zeros((PROJ_ROWS, HEAD_DIM), F32)
            for g in range(N_PATTERNS):
                wgt = jnp.exp(ls[g] - mx)
                num = num + wgt * on_scr[g * HEAD_PAIR + hl, rows, :]
                den = den + wgt
            o_ref[0, rows, hl * HEAD_DIM:(hl + 1) * HEAD_DIM] = (num / den).astype(BF16)
        return carry

    lax.fori_loop(0, seq // PROJ_ROWS, merge_body, 0)


def _attention(hs, w_qkv, tables, q_norm_w, k_norm_w):
    bsz, seq, _ = hs[0].shape
    pair_cols = HEAD_PAIR * 3 * HEAD_DIM
    wspec = lambda g: pl.BlockSpec((D_MODEL, pair_cols), lambda b, hp: (0, g * (HEADS // HEAD_PAIR) + hp))
    slots = N_PATTERNS * HEAD_PAIR
    stat = pltpu.VMEM((slots, seq, HEAD_DIM), F32)
    qkv = pltpu.VMEM((slots, seq, HEAD_DIM), BF16)
    once = pl.Buffered(1)
    hspec = pl.BlockSpec((1, seq, D_MODEL), lambda b, hp: (b, 0, 0), pipeline_mode=once)
    tspec = pl.BlockSpec((1, seq, HEAD_DIM), lambda b, hp: (b, 0, 0), pipeline_mode=once)
    vec = pl.BlockSpec((1, HEAD_DIM), lambda b, hp: (0, 0))
    return pl.pallas_call(
        functools.partial(_attn_kernel, seq=seq),
        grid=(bsz, HEADS // HEAD_PAIR),
        in_specs=[hspec, hspec, hspec, *[wspec(g) for g in range(N_PATTERNS)],
                  tspec, tspec, tspec, vec, vec],
        out_specs=pl.BlockSpec((1, seq, HEAD_PAIR * HEAD_DIM), lambda b, hp: (b, 0, hp)),
        out_shape=jax.ShapeDtypeStruct((bsz, seq, HEADS * HEAD_DIM), BF16),
        scratch_shapes=[qkv, qkv, qkv, stat, stat,
                        pltpu.VMEM((2, WINDOW_KEYS, 2 * WINDOW_KEYS), F32)],
        compiler_params=_params(2, ATTN_VMEM_LIMIT),
        name="attention",
    )(*hs, *[w_qkv] * N_PATTERNS, *tables, q_norm_w.reshape(1, HEAD_DIM), k_norm_w.reshape(1, HEAD_DIM))


def _qkv_cast_kernel(q_ref, k_ref, v_ref, o_ref):
    for part, ref in enumerate((q_ref, k_ref, v_ref)):
        o_ref[:, part * HEAD_DIM:(part + 1) * HEAD_DIM] = ref[...].astype(BF16)


def _pair_major_qkv(w_all, layer):
    n_heads = N_PATTERNS * HEADS
    part = lambda p: pl.BlockSpec((None, D_MODEL, HEAD_DIM), lambda j: (layer, 0, p * n_heads + j))
    return pl.pallas_call(
        _qkv_cast_kernel,
        grid=(n_heads,),
        in_specs=[part(0), part(1), part(2)],
        out_specs=pl.BlockSpec((D_MODEL, 3 * HEAD_DIM), lambda j: (0, j)),
        out_shape=jax.ShapeDtypeStruct((D_MODEL, 3 * ATTN_QKV), BF16),
        compiler_params=_params(1),
        name="qkv_weights",
    )(w_all, w_all, w_all)


def kernel(x, c, positions, mod_w, mod_b, norm_w, even_w_in, conv_dw_w, conv_dw_b, conv_ln_w, conv_ln_b, ssm_lam_re, ssm_lam_im, ssm_log_dt, ssm_b_re, ssm_b_im, ssm_c_re, ssm_c_im, ssm_d, ssm_glu_w, ssm_glu_b, even_w_out, attn_w_in, attn_q_norm_w, attn_k_norm_w, attn_w_out):
    bsz, seq, _ = x.shape
    assert seq == 2048 and x.shape[-1] == D_MODEL and bsz % SSM_BATCH == 0
    mod = _modulation(c, mod_w, mod_b)
    shift = [mod[l, :, None, :D_MODEL] for l in range(DEPTH)]
    scale = [mod[l, :, None, D_MODEL:2 * D_MODEL] for l in range(DEPTH)]
    gate = [mod[l, :, None, 2 * D_MODEL:] for l in range(DEPTH)]
    nw = [norm_w[l].reshape(1, D_MODEL) for l in range(DEPTH)]
    assert DEPTH % 2 == 0
    tables = _rope_tables(positions)
    ssm_mats = jax.vmap(_ssm_matrices)(ssm_lam_re, ssm_lam_im, ssm_log_dt, ssm_b_re, ssm_b_im, ssm_c_re, ssm_c_im)

    hs = None
    for layer in range(DEPTH):
        i = layer // 2
        nxt = (nw[layer + 1], scale[layer + 1], shift[layer + 1]) if layer + 1 < DEPTH else None
        if layer % 2 == 0:
            src, norm = (x, (nw[0], scale[0], shift[0])) if layer == 0 else (hs[0], None)
            ya, u, bz = _even_in(src, even_w_in[i].astype(BF16), conv_dw_w[i], conv_dw_b[i],
                                 conv_ln_w[i], conv_ln_b[i], norm)
            ys = _ssm(u, ssm_mats, i, ssm_d[i])
            w_out = even_w_out[i].astype(BF16)
            x, hs = _tail_call(_even_out_kernel, "even_out", x, [ya, ys, bz],
                               [ssm_glu_w[i].astype(BF16), ssm_glu_b[i].reshape(1, SSM_WIDTH),
                                w_out[:CONV_WIDTH], w_out[CONV_WIDTH:]], gate[layer], nxt, permuted=True)
        else:
            o = _attention(hs, _pair_major_qkv(attn_w_in, i), tables, attn_q_norm_w[i], attn_k_norm_w[i])
            width = HEADS * HEAD_DIM
            gate_cols = pl.BlockSpec((None, D_MODEL, width), lambda b, t, i=i: (i, 0, 3 * ATTN_QKV // width))
            x, hs = _tail_call(_odd_out_kernel, "odd_out", x, [hs[0], o],
                               [(attn_w_in, gate_cols), attn_w_out[i].astype(BF16)],
                               gate[layer], nxt, permuted=False)
    return x
```

```python
import functools
import math

import jax
import jax.numpy as jnp
from jax import lax
from jax.experimental import pallas as pl
from jax.experimental.pallas import tpu as pltpu

F32 = jnp.float32
BF16 = jnp.bfloat16

D_MODEL = 1024
DEPTH = 4
CONV_WIDTH = 512
CONV_KERNEL = 31
SSM_WIDTH = 512
SSM_GROUP = 16
SSM_GROUPS = 32
SSM_STATE = 64
HEAD_DIM = 128
ATTN_PATTERNS = ((128, 1), (512, 4), (2048, 16))
N_PATTERNS = 3
HEADS = 8
ATTN_QKV = N_PATTERNS * HEADS * HEAD_DIM
ATTN_SCALE = HEAD_DIM ** -0.5
ROPE_THETA = 10000.0
EPS = 1e-6
NEG_INF = -1e30

LANES = 128
WINDOW_KEYS = 128
SSM_CHUNK = 16
SSM_LANE_GROUPS = LANES // SSM_GROUP
SSM_BLOCKS = SSM_WIDTH // LANES
SSM_STATE_COLS = SSM_LANE_GROUPS * SSM_STATE
SSM_BATCH = 2
ROW_TILE = 512
CONV_HALO = 32
CONV_ROWS = 64
PROJ_ROWS = 256
VMEM_LIMIT = 56 * 1024 * 1024
ATTN_VMEM_LIMIT = 58 * 1024 * 1024
HEAD_PAIR = 2


def _params(n_axes, vmem=VMEM_LIMIT):
    return pltpu.CompilerParams(dimension_semantics=("arbitrary",) * n_axes, vmem_limit_bytes=vmem)


def _adaln(x, nw, scale, shift):
    ms = jnp.mean(x * x, axis=-1, keepdims=True)
    return (x * lax.rsqrt(ms + EPS) * nw) * (1.0 + scale) + shift


def _dot(a, b):
    return jnp.dot(a, b, preferred_element_type=F32)


def _mod_kernel(c_ref, w_ref, b_ref, o_ref):
    o_ref[0] = _dot(c_ref[...].astype(BF16), w_ref[0].astype(BF16)) + b_ref[0]


def _modulation(c, mod_w, mod_b):
    bsz = c.shape[0]
    nblk = 3 * D_MODEL // D_MODEL
    return pl.pallas_call(
        _mod_kernel,
        grid=(DEPTH, nblk),
        in_specs=[pl.BlockSpec((bsz, D_MODEL), lambda l, j: (0, 0)),
                  pl.BlockSpec((1, D_MODEL, D_MODEL), lambda l, j: (l, 0, j)),
                  pl.BlockSpec((1, 1, D_MODEL), lambda l, j: (l, 0, j))],
        out_specs=pl.BlockSpec((1, bsz, D_MODEL), lambda l, j: (l, 0, j)),
        out_shape=jax.ShapeDtypeStruct((DEPTH, bsz, 3 * D_MODEL), F32),
        compiler_params=_params(2),
        name="modulation",
    )(c, mod_w, mod_b.reshape(DEPTH, 1, 3 * D_MODEL))


def _rope_kernel(pos_ref, inv_ref, tab_ref, tab1_ref, tab2_ref, stage_scr, stage2_scr):
    half_rows, half_lanes = ROW_TILE // 2, HEAD_DIM // 2
    pos = pos_ref[0].astype(F32)
    lower = lax.broadcasted_iota(jnp.int32, (half_rows, HEAD_DIM), 1) < half_lanes
    ang = jnp.where(lower, pos[:half_rows], pos[half_rows:]) * inv_ref[...]
    cos, sin = jnp.cos(ang), jnp.sin(ang)
    tab = jnp.concatenate([jnp.where(lower, cos, pltpu.roll(sin, half_lanes, 1)),
                           jnp.where(lower, pltpu.roll(cos, half_lanes, 1), sin)], axis=0)
    tab_ref[0] = tab
    d1, d2 = ATTN_PATTERNS[1][1], ATTN_PATTERNS[2][1]
    assert d2 == d1 * d1
    n1, n2 = ROW_TILE // d1, ROW_TILE // d2
    stage_scr[...] = tab
    for r in range(d1):
        part = stage_scr[pl.ds(r, n1, stride=d1), :]
        tab1_ref[0, r] = part
        stage2_scr[r] = part
    for r in range(d1):
        for a in range(d1):
            tab2_ref[0, a * d1 + r] = stage2_scr[r, pl.ds(a, n2, stride=d1), :]


def _rope_tables(positions):
    bsz, seq = positions.shape
    inv = ROPE_THETA ** (-jnp.arange(0, HEAD_DIM, 2, dtype=F32) / HEAD_DIM)
    inv2 = jnp.concatenate([inv, inv]).reshape(1, HEAD_DIM)
    d1, d2 = ATTN_PATTERNS[1][1], ATTN_PATTERNS[2][1]
    perm = lambda d: (pl.BlockSpec((1, d, ROW_TILE // d, HEAD_DIM), lambda b, i: (b, 0, i, 0)),
                      jax.ShapeDtypeStruct((bsz, d, seq // d, HEAD_DIM), F32))
    specs, shapes = zip((pl.BlockSpec((1, ROW_TILE, HEAD_DIM), lambda b, i: (b, i, 0)),
                         jax.ShapeDtypeStruct((bsz, seq, HEAD_DIM), F32)), perm(d1), perm(d2))
    tabs = pl.pallas_call(
        _rope_kernel,
        grid=(bsz, seq // ROW_TILE),
        in_specs=[pl.BlockSpec((1, ROW_TILE, 1), lambda b, i: (b, i, 0)),
                  pl.BlockSpec((1, HEAD_DIM), lambda b, i: (0, 0))],
        out_specs=list(specs),
        out_shape=list(shapes),
        scratch_shapes=[pltpu.VMEM((ROW_TILE, HEAD_DIM), F32), pltpu.VMEM((d1, ROW_TILE // d1, HEAD_DIM), F32)],
        compiler_params=_params(2),
        name="rope_tables",
    )(positions.reshape(bsz, seq, 1), inv2)
    return [t.reshape(bsz, seq, HEAD_DIM) for t in tabs]


def _even_in_kernel(*refs, from_x):
    if from_x:
        x_ref, nw_ref, sc_ref, sh_ref = refs[:4]
        h = _adaln(x_ref[0], nw_ref[...], sc_ref[0], sh_ref[0]).astype(BF16)
        refs = refs[4:]
    else:
        h = refs[0][0]
        refs = refs[1:]
    w_ref, dww_ref, dwb_ref, lnw_ref, lnb_ref, ya_ref, u_ref, bz_ref, conv_scr, y_scr, az_scr = refs
    i = pl.program_id(1)
    cw = CONV_WIDTH
    slabs = cw // LANES

    @pl.when(i == 0)
    def _():
        conv_scr[:, 0:CONV_HALO, :] = jnp.zeros((slabs, CONV_HALO, LANES), F32)

    a1 = _dot(h, w_ref[:, 0:cw])
    a2 = _dot(h, w_ref[:, cw:2 * cw])
    a = a1 * jax.nn.sigmoid(a2)
    for l in range(slabs):
        conv_scr[l, CONV_HALO:CONV_HALO + ROW_TILE, :] = a[:, l * LANES:(l + 1) * LANES]
    az_scr[...] = jax.nn.silu(_dot(h, w_ref[:, 2 * cw:3 * cw]))
    u_ref[0] = _dot(h, w_ref[:, 3 * cw:4 * cw])
    bz_ref[0] = _dot(h, w_ref[:, 4 * cw:5 * cw])

    first_tap = CONV_HALO - (CONV_KERNEL - 1)
    half = CONV_ROWS // 2

    def conv_block(r, carry):
        r0 = r * CONV_ROWS
        for l in range(slabs):
            lanes = slice(l * LANES, (l + 1) * LANES)
            for par in range(2):
                acc = jnp.broadcast_to(dwb_ref[:, lanes], (half, LANES))
                for k in range(CONV_KERNEL):
                    tap = conv_scr[l, pl.ds(r0 + first_tap + par + k, half, stride=2), :]
                    acc = acc + dww_ref[k:k + 1, lanes] * tap
                y_scr[l, pl.ds(r0 + par, half, stride=2), :] = acc
        return carry

    lax.fori_loop(0, ROW_TILE // CONV_ROWS, conv_block, 0)
    for l in range(slabs):
        conv_scr[l, 0:CONV_HALO, :] = conv_scr[l, ROW_TILE:ROW_TILE + CONV_HALO, :]

    def norm_block(r, carry):
        rows = pl.ds(pl.multiple_of(r * CONV_ROWS, CONV_ROWS), CONV_ROWS)
        acc = jnp.concatenate([y_scr[l, rows, :] for l in range(slabs)], axis=-1)
        mu = jnp.mean(acc, axis=-1, keepdims=True)
        xc = acc - mu
        y = xc * lax.rsqrt(jnp.mean(xc * xc, axis=-1, keepdims=True) + EPS)
        y = y * lnw_ref[...] + lnb_ref[...]
        ya_ref[0, rows, :] = (jax.nn.silu(y) * az_scr[rows, :]).astype(BF16)
        return carry

    lax.fori_loop(0, ROW_TILE // CONV_ROWS, norm_block, 0, unroll=4)


def _even_in(h, w_in, dw_w, dw_b, ln_w, ln_b, norm=None):
    bsz, seq, _ = h.shape
    cw = CONV_WIDTH
    const = lambda shape: pl.BlockSpec(shape, lambda b, i: (0,) * len(shape))
    row = lambda width: pl.BlockSpec((1, ROW_TILE, width), lambda b, i: (b, i, 0))
    vec = pl.BlockSpec((1, 1, D_MODEL), lambda b, i: (b, 0, 0))
    norm_args = [] if norm is None else list(norm)
    norm_specs = [] if norm is None else [const((1, D_MODEL)), vec, vec]
    return pl.pallas_call(
        functools.partial(_even_in_kernel, from_x=norm is not None),
        grid=(bsz, seq // ROW_TILE),
        in_specs=[row(D_MODEL), *norm_specs, const(w_in.shape), const((CONV_KERNEL, cw)),
                  const((1, cw)), const((1, cw)), const((1, cw))],
        out_specs=[row(cw), row(cw), row(cw)],
        out_shape=[jax.ShapeDtypeStruct((bsz, seq, cw), BF16),
                   jax.ShapeDtypeStruct((bsz, seq, cw), F32),
                   jax.ShapeDtypeStruct((bsz, seq, cw), F32)],
        scratch_shapes=[pltpu.VMEM((cw // LANES, CONV_HALO + ROW_TILE, LANES), F32),
                        pltpu.VMEM((cw // LANES, ROW_TILE, LANES), F32),
                        pltpu.VMEM((ROW_TILE, cw), F32)],
        compiler_params=_params(2),
        name="even_in",
    )(h, *norm_args, w_in, dw_w, dw_b.reshape(1, cw), ln_w.reshape(1, cw), ln_b.reshape(1, cw))


def _ssm_matrices(lam_re, lam_im, log_dt, b_re, b_im, c_re, c_im):
    hp = lax.Precision.HIGHEST
    t = SSM_CHUNK
    lr, li = lam_re.astype(F32), lam_im.astype(F32)
    dt = jnp.exp(log_dt.astype(F32))[:, None]

    def a_pow(k):
        kk = k.astype(F32)[:, None, None]
        mag = jnp.exp(kk * (lr * dt)[None])
        ang = kk * (li * dt)[None]
        return mag * jnp.cos(ang), mag * jnp.sin(ang)

    ar, ai = a_pow(jnp.ones((1,), F32))
    ar, ai = ar[0], ai[0]
    den = lr * lr + li * li
    nr = ar - 1.0
    kr = (nr * lr + ai * li) / den
    ki = (ai * lr - nr * li) / den
    br, bi = b_re.astype(F32), b_im.astype(F32)
    bbr = kr[..., None] * br - ki[..., None] * bi
    bbi = kr[..., None] * bi + ki[..., None] * br
    cr, ci = c_re.astype(F32), c_im.astype(F32)

    pr, pi = a_pow(jnp.arange(t + 1))
    wr = pr[:t, :, :, None] * bbr[None] - pi[:t, :, :, None] * bbi[None]
    wi = pr[:t, :, :, None] * bbi[None] + pi[:t, :, :, None] * bbr[None]
    kk = (jnp.einsum('gop,kgpi->kgoi', cr, wr, precision=hp)
          - jnp.einsum('gop,kgpi->kgoi', ci, wi, precision=hp))

    nb, gl, hh, pp = SSM_BLOCKS, SSM_LANE_GROUPS, SSM_GROUP, SSM_STATE
    grp = jnp.arange(gl)[:, None, None]
    row = jnp.arange(t * hh)[None, :, None]
    col = jnp.arange(t * LANES)[None, None, :]
    place = ((row // hh == col // LANES) & (row % hh == col % hh) & ((col % LANES) // hh == grp)).astype(BF16)
    prow = jnp.arange(pp)[None, :, None]
    pcol = jnp.arange(gl * pp)[None, None, :]
    spread = ((pcol % pp == prow) & (pcol // pp == grp)).astype(BF16)

    def rows_sgi(m):
        n = m.shape[-1]
        return m.reshape(nb, gl, t, hh, n).transpose(0, 2, 1, 3, 4).reshape(nb, t * LANES, n)

    lrow = jnp.arange(t * hh)[None, :, None]
    lcol = jnp.arange(t * hh)[None, None, :]
    s_idx = jnp.arange(t)[:, None, None]
    shift = ((lcol // hh == lrow // hh + s_idx) & (lcol % hh == lrow % hh)).astype(BF16)
    kk_i = kk.reshape(t, nb, gl, hh, hh).transpose(1, 2, 4, 0, 3).reshape(nb, gl, hh, t * hh)
    toe = jnp.einsum('cgil,sln->cgsin', kk_i.astype(BF16), shift,
                     preferred_element_type=BF16).reshape(nb, gl, t * hh, t * hh)
    m_intra = rows_sgi(jnp.einsum('cgab,gbn->cgan', toe, place, preferred_element_type=BF16))

    def state_cols(w):
        w = w[::-1].reshape(t, nb, gl, pp, hh).transpose(1, 2, 0, 4, 3).reshape(nb, gl, t * hh, pp)
        return rows_sgi(jnp.einsum('cgap,gpn->cgan', w.astype(BF16), spread, preferred_element_type=BF16))

    m_state_r, m_state_i = state_cols(wr), state_cols(wi)

    def in_rows(q):
        q = q.reshape(t, nb, gl, hh, pp).transpose(1, 2, 4, 0, 3).reshape(nb, gl, pp, t * hh)
        return jnp.einsum('cgpb,gbn->cgpn', q.astype(BF16), place,
                          preferred_element_type=BF16).reshape(nb, gl * pp, t * LANES)

    m_in_r = in_rows(pr[1:, :, None, :] * cr[None] - pi[1:, :, None, :] * ci[None])
    m_in_i = in_rows(-(pr[1:, :, None, :] * ci[None] + pi[1:, :, None, :] * cr[None]))

    n_lvl = int(math.log2(2048 // t))
    sr2, si2 = a_pow(t * (2 ** jnp.arange(n_lvl)))
    ap_r = sr2.reshape(n_lvl, nb, gl * pp).transpose(1, 0, 2)
    ap_i = si2.reshape(n_lvl, nb, gl * pp).transpose(1, 0, 2)
    return m_intra, m_state_r, m_state_i, m_in_r, m_in_i, ap_r, ap_i


def _ssm_kernel(u_ref, mintra_ref, msr_ref, msi_ref, minr_ref, mini_ref, apr_ref, api_ref, d_ref, y_ref,
                x_scr, yi_scr, cr_scr, ci_scr, zr_scr, zi_scr, *, n_chunks, n_levels):
    t = SSM_CHUNK
    sc = SSM_STATE_COLS
    pair = 2 * LANES
    for bb in range(SSM_BATCH):
        for t0 in range(t):
            x_scr[bb * n_chunks:(bb + 1) * n_chunks, t0 * LANES:(t0 + 1) * LANES] = (
                u_ref[bb, pl.ds(t0, n_chunks, stride=t), :].astype(BF16))
    s_re = _dot(x_scr[...], msr_ref[0])
    s_im = _dot(x_scr[...], msi_ref[0])
    for tp in range(t // 2):
        cols = slice(tp * pair, (tp + 1) * pair)
        yi_scr[:, cols] = _dot(x_scr[:, 0:(tp + 1) * pair], mintra_ref[0, 0:(tp + 1) * pair, cols])

    zero = jnp.zeros((n_chunks, sc), F32)
    for p in range(2):
        zr_scr[p, 0:n_chunks, :] = zero
        zi_scr[p, 0:n_chunks, :] = zero
    for bb in range(SSM_BATCH):
        rows = slice(bb * n_chunks, (bb + 1) * n_chunks)
        zr_scr[0, n_chunks:, :] = s_re[rows]
        zi_scr[0, n_chunks:, :] = s_im[rows]
        for k in range(n_levels):
            src, dst = k % 2, 1 - (k % 2)
            sh = n_chunks - (1 << k)
            zr = zr_scr[src, n_chunks:, :]
            zi = zi_scr[src, n_chunks:, :]
            pr = zr_scr[src, sh:sh + n_chunks, :]
            pi = zi_scr[src, sh:sh + n_chunks, :]
            ar = apr_ref[0, k:k + 1, :]
            ai = api_ref[0, k:k + 1, :]
            zr_scr[dst, n_chunks:, :] = zr + ar * pr - ai * pi
            zi_scr[dst, n_chunks:, :] = zi + ar * pi + ai * pr
        fin = n_levels % 2
        cr_scr[rows, :] = zr_scr[fin, n_chunks - 1:2 * n_chunks - 1, :].astype(BF16)
        ci_scr[rows, :] = zi_scr[fin, n_chunks - 1:2 * n_chunks - 1, :].astype(BF16)

    carry_r = cr_scr[...]
    carry_i = ci_scr[...]
    for tp in range(t // 2):
        cols = slice(tp * pair, (tp + 1) * pair)
        yc = yi_scr[:, cols] + _dot(carry_r, minr_ref[0, :, cols]) + _dot(carry_i, mini_ref[0, :, cols])
        for bb in range(SSM_BATCH):
            for j in range(2):
                y_ref[bb, pl.ds(2 * tp + j, n_chunks, stride=t), :] = (
                    yc[bb * n_chunks:(bb + 1) * n_chunks, j * LANES:(j + 1) * LANES])
    for bb in range(SSM_BATCH):
        y_ref[bb] = y_ref[bb] + d_ref[0] * u_ref[bb]


def _ssm(u, mats, layer, d_skip):
    m_intra, m_state_r, m_state_i, m_in_r, m_in_i, ap_r, ap_i = mats
    bsz, seq, _ = u.shape
    n_chunks = seq // SSM_CHUNK
    n_levels = ap_r.shape[2]
    nb = SSM_BATCH
    once = pl.Buffered(1)
    blk = pl.BlockSpec((nb, seq, LANES), lambda c, b: (b, 0, c))
    mat = lambda m: pl.BlockSpec((None, 1) + m.shape[2:], lambda c, b: (layer, c, 0, 0), pipeline_mode=once)
    lvl = pl.BlockSpec((None, 1, n_levels, SSM_STATE_COLS), lambda c, b: (layer, c, 0, 0))
    return pl.pallas_call(
        functools.partial(_ssm_kernel, n_chunks=n_chunks, n_levels=n_levels),
        grid=(SSM_BLOCKS, bsz // nb),
        in_specs=[blk, mat(m_intra), mat(m_state_r), mat(m_state_i), mat(m_in_r), mat(m_in_i), lvl, lvl,
                  pl.BlockSpec((1, 1, LANES), lambda c, b: (c, 0, 0))],
        out_specs=blk,
        out_shape=jax.ShapeDtypeStruct(u.shape, F32),
        scratch_shapes=[pltpu.VMEM((nb * n_chunks, SSM_CHUNK * LANES), BF16),
                        pltpu.VMEM((nb * n_chunks, SSM_CHUNK * LANES), F32),
                        pltpu.VMEM((nb * n_chunks, SSM_STATE_COLS), BF16),
                        pltpu.VMEM((nb * n_chunks, SSM_STATE_COLS), BF16),
                        pltpu.VMEM((2, 2 * n_chunks, SSM_STATE_COLS), F32),
                        pltpu.VMEM((2, 2 * n_chunks, SSM_STATE_COLS), F32)],
        compiler_params=_params(2),
        name="ssm",
    )(u, m_intra, m_state_r, m_state_i, m_in_r, m_in_i, ap_r, ap_i, d_skip.reshape(SSM_BLOCKS, 1, LANES))


def _finish(x_ref, gate_ref, out, nxt, xo_ref, h_outs, stage_scr, stage2_scr):
    xn = x_ref[0] + gate_ref[0] * out
    xo_ref[0] = xn
    if nxt is None:
        return
    nw_ref, sc_ref, sh_ref = nxt
    hn = _adaln(xn, nw_ref[...], sc_ref[0], sh_ref[0])
    h_outs[0][0] = hn.astype(BF16)
    if len(h_outs) == 1:
        return
    slabs = D_MODEL // LANES
    d1, d2 = ATTN_PATTERNS[1][1], ATTN_PATTERNS[2][1]
    assert d2 == d1 * d1 and len(h_outs) == 3
    n1, n2 = ROW_TILE // d1, ROW_TILE // d2
    for l in range(slabs):
        stage_scr[l] = hn[:, l * LANES:(l + 1) * LANES]
    for r in range(d1):
        parts = [stage_scr[l, pl.ds(r, n1, stride=d1), :] for l in range(slabs)]
        h_outs[1][0, r] = jnp.concatenate(parts, axis=-1).astype(BF16)
        for l in range(slabs):
            stage2_scr[r * slabs + l] = parts[l]
    for r in range(d1):
        for a in range(d1):
            parts = [stage2_scr[r * slabs + l, pl.ds(a, n2, stride=d1), :] for l in range(slabs)]
            h_outs[2][0, a * d1 + r] = jnp.concatenate(parts, axis=-1).astype(BF16)


def _even_out_kernel(*refs, n_h):
    x_ref, ya_ref, ys_ref, bz_ref, gw_ref, gb_ref, wa_ref, ws_ref, gate_ref = refs[:9]
    nxt, xo_ref, h_outs, stages = _tail_refs(refs[9:], n_h)
    y = ys_ref[0]
    y = 0.5 * y * (1.0 + lax.erf(y * (2.0 ** -0.5)))
    y = y * jax.nn.sigmoid(_dot(y.astype(BF16), gw_ref[...]) + gb_ref[...])
    y = y * jax.nn.silu(bz_ref[0])
    out = _dot(ya_ref[0], wa_ref[...]) + _dot(y.astype(BF16), ws_ref[...])
    _finish(x_ref, gate_ref, out, nxt, xo_ref, h_outs, *stages)


def _odd_out_kernel(*refs, n_h):
    x_ref, h_ref, o_ref, wz_ref, wo_ref, gate_ref = refs[:6]
    nxt, xo_ref, h_outs, stages = _tail_refs(refs[6:], n_h)
    z = _dot(h_ref[0], wz_ref[...].astype(BF16))
    g = o_ref[0] * jax.nn.silu(z)
    out = _dot(g.astype(BF16), wo_ref[...])
    _finish(x_ref, gate_ref, out, nxt, xo_ref, h_outs, *stages)


def _tail_refs(refs, n_h):
    if n_h == 0:
        return None, refs[0], (), (None, None)
    nxt, xo_ref, h_outs = refs[:3], refs[3], refs[4:4 + n_h]
    return nxt, xo_ref, h_outs, (tuple(refs[4 + n_h:6 + n_h]) if n_h > 1 else (None, None))


def _tail_call(body, name, x, rows, consts, gate, nxt, permuted):
    bsz, seq, _ = x.shape
    n_h = 0 if nxt is None else (N_PATTERNS if permuted else 1)
    row = lambda a: pl.BlockSpec((1, ROW_TILE, a.shape[-1]), lambda b, i: (b, i, 0))
    const = lambda a: pl.BlockSpec(a.shape, lambda b, i: (0,) * a.ndim)
    vec = pl.BlockSpec((1, 1, D_MODEL), lambda b, i: (b, 0, 0))
    consts = [a if isinstance(a, tuple) else (a, const(a)) for a in consts]
    args = [x, *rows, *[a for a, _ in consts], gate]
    in_specs = [row(x)] + [row(a) for a in rows] + [spec for _, spec in consts] + [vec]
    out_specs = [row(x)]
    out_shape = [jax.ShapeDtypeStruct(x.shape, F32)]
    scratch = []
    if n_h:
        nw, sc, sh = nxt
        args += [nw, sc, sh]
        in_specs += [const(nw), vec, vec]
        out_specs.append(row(x))
        out_shape.append(jax.ShapeDtypeStruct(x.shape, BF16))
    if n_h > 1:
        for _, dil in ATTN_PATTERNS[1:]:
            out_specs.append(pl.BlockSpec((1, dil, ROW_TILE // dil, D_MODEL), lambda b, i: (b, 0, i, 0)))
            out_shape.append(jax.ShapeDtypeStruct((bsz, dil, seq // dil, D_MODEL), BF16))
        d1 = ATTN_PATTERNS[1][1]
        scratch.append(pltpu.VMEM((D_MODEL // LANES, ROW_TILE, LANES), F32))
        scratch.append(pltpu.VMEM((d1 * D_MODEL // LANES, ROW_TILE // d1, LANES), F32))
    res = pl.pallas_call(
        functools.partial(body, n_h=n_h),
        grid=(bsz, seq // ROW_TILE),
        in_specs=in_specs, out_specs=out_specs, out_shape=out_shape, scratch_shapes=scratch,
        compiler_params=_params(2),
        name=name,
    )(*args)
    return res[0], [h.reshape(x.shape) for h in res[1:]]


def _attn_kernel(h0_ref, h1_ref, h2_ref, *refs, seq):
    w_refs = refs[:N_PATTERNS]
    (t0_ref, t1_ref, t2_ref, qw_ref, kw_ref, o_ref,
     q_scr, k_scr, v_scr, on_scr, lse_scr, bias_scr) = refs[N_PATTERNS:]
    wk = WINDOW_KEYS
    n_chunks = seq // PROJ_ROWS
    chunk_blocks = PROJ_ROWS // wk
    h_refs = (h0_ref, h1_ref, h2_ref)
    tabs = (t0_ref, t1_ref, t2_ref)

    qi = lax.broadcasted_iota(jnp.int32, (wk, 2 * wk), 0)
    kj = lax.broadcasted_iota(jnp.int32, (wk, 2 * wk), 1)
    band = (kj >= qi) & (kj <= qi + wk)
    bias_scr[0] = jnp.where(band, 0.0, NEG_INF)
    bias_scr[1] = jnp.where(band & (kj >= wk), 0.0, NEG_INF)

    def chunk_rows(c):
        start = c * PROJ_ROWS
        if not isinstance(c, int):
            start = pl.multiple_of(start, PROJ_ROWS)
        return pl.ds(start, PROJ_ROWS)

    def proj_dots(c):
        rows = chunk_rows(c)
        return [_dot(h_refs[g][0, rows, :], w_refs[g][...]) for g in range(N_PATTERNS)]

    def proj_store(c, prs):
        rows = chunk_rows(c)
        for g, pr in enumerate(prs):
            tab = tabs[g][0, rows, :]
            swapped = pltpu.roll(tab, HEAD_DIM // 2, 1)
            lower = lax.broadcasted_iota(jnp.int32, tab.shape, 1) < HEAD_DIM // 2
            cos = jnp.where(lower, tab, swapped)
            sin = jnp.where(lower, -swapped, tab)

            def norm_rope(t, w):
                t = t * lax.rsqrt(jnp.mean(t * t, axis=-1, keepdims=True) + EPS) * w
                return t * cos + pltpu.roll(t, HEAD_DIM // 2, 1) * sin

            for hl in range(HEAD_PAIR):
                off = hl * 3 * HEAD_DIM
                slot = g * HEAD_PAIR + hl
                q_scr[slot, rows, :] = (norm_rope(pr[:, off:off + HEAD_DIM], qw_ref[...]) * ATTN_SCALE).astype(BF16)
                k_scr[slot, rows, :] = norm_rope(pr[:, off + HEAD_DIM:off + 2 * HEAD_DIM], kw_ref[...]).astype(BF16)
                v_scr[slot, rows, :] = pr[:, off + 2 * HEAD_DIM:off + 3 * HEAD_DIM].astype(BF16)

    def block_ids(c):
        return [(g, hl, chunk_blocks * c + u, u) for g in range(N_PATTERNS) for hl in range(HEAD_PAIR)
                for u in range(chunk_blocks)]

    def is_first(g, j, u):
        n_blk = seq // ATTN_PATTERNS[g][1] // wk
        if n_blk == 1:
            return True
        if u % 2 == 1:
            return False
        if isinstance(j, int):
            return j % n_blk == 0
        return (j % n_blk) == 0

    def block_start(j):
        start = j * wk
        return start if isinstance(start, int) else pl.multiple_of(start, wk)

    def scores(g, hl, j, u):
        first = is_first(g, j, u)
        slot = g * HEAD_PAIR + hl
        q = q_scr[slot, pl.ds(block_start(j), wk), :]
        if first is True:
            kk = k_scr[slot, pl.ds(block_start(j), wk), :]
            bias = bias_scr[1, :, wk:]
        else:
            kk = k_scr[slot, pl.ds(block_start(j - 1), 2 * wk), :]
            bias = bias_scr[0] if first is False else bias_scr[jnp.where(first, 1, 0)]
        return lax.dot_general(q, kk, (((1,), (1,)), ((), ())), preferred_element_type=F32) + bias

    def softmax(s):
        m = jnp.max(s, axis=-1, keepdims=True)
        return jnp.exp(s - m).astype(BF16), m

    def weighted(g, hl, j, u, p):
        slot = g * HEAD_PAIR + hl
        if is_first(g, j, u) is True:
            vv = v_scr[slot, pl.ds(block_start(j), wk), :]
        else:
            vv = v_scr[slot, pl.ds(block_start(j - 1), 2 * wk), :]
        ov = _dot(p, jnp.concatenate([vv, jnp.ones_like(vv)], axis=1))
        return ov[:, :HEAD_DIM], ov[:, HEAD_DIM:]

    def attn_store(g, hl, j, o, m, den):
        dil = ATTN_PATTERNS[g][1]
        n_blk = seq // dil // wk
        slot = g * HEAD_PAIR + hl
        if dil == 1:
            nat = pl.ds(block_start(j), wk)
        else:
            nat = pl.ds((j % n_blk) * (wk * dil) + j // n_blk, wk, stride=dil)
        on_scr[slot, nat, :] = o * (1.0 / den)
        lse_scr[slot, nat, :] = m + jnp.log(den)

    def step(c_attn, c_proj):
        ids = block_ids(c_attn) if c_attn is not None else []
        ss = [scores(g, hl, j, u) for g, hl, j, u in ids]
        prs = proj_dots(c_proj) if c_proj is not None else None
        sm = [softmax(s) for s in ss]
        os_ = [weighted(g, hl, j, u, p) for (g, hl, j, u), (p, _) in zip(ids, sm)]
        if prs is not None:
            proj_store(c_proj, prs)
        for (g, hl, j, u), (o, den), (_, m) in zip(ids, os_, sm):
            attn_store(g, hl, j, o, m, den)

    step(None, 0)
    step(0, 1)

    def body(c, carry):
        step(c - 1, c)
        return carry

    lax.fori_loop(2, n_chunks, body, 0)
    step(n_chunks - 1, None)

    def merge_body(c, carry):
        rows = pl.ds(pl.multiple_of(c * PROJ_ROWS, PROJ_ROWS), PROJ_ROWS)
        for hl in range(HEAD_PAIR):
            ls = [lse_scr[g * HEAD_PAIR + hl, rows, :] for g in range(N_PATTERNS)]
            mx = jnp.maximum(jnp.maximum(ls[0], ls[1]), ls[2])
            num = jnp.zeros((PROJ_ROWS, HEAD_DIM), F32)
            den = jnp.zeros((PROJ_ROWS, HEAD_DIM), F32)
            for g in range(N_PATTERNS):
                wgt = jnp.exp(ls[g] - mx)
                num = num + wgt * on_scr[g * HEAD_PAIR + hl, rows, :]
                den = den + wgt
            o_ref[0, rows, hl * HEAD_DIM:(hl + 1) * HEAD_DIM] = (num / den).astype(BF16)
        return carry

    lax.fori_loop(0, seq // PROJ_ROWS, merge_body, 0)


def _attention(hs, w_qkv, tables, q_norm_w, k_norm_w):
    bsz, seq, _ = hs[0].shape
    pair_cols = HEAD_PAIR * 3 * HEAD_DIM
    wspec = lambda g: pl.BlockSpec((D_MODEL, pair_cols), lambda b, hp: (0, g * (HEADS // HEAD_PAIR) + hp))
    slots = N_PATTERNS * HEAD_PAIR
    stat = pltpu.VMEM((slots, seq, HEAD_DIM), F32)
    qkv = pltpu.VMEM((slots, seq, HEAD_DIM), BF16)
    once = pl.Buffered(1)
    hspec = pl.BlockSpec((1, seq, D_MODEL), lambda b, hp: (b, 0, 0), pipeline_mode=once)
    tspec = pl.BlockSpec((1, seq, HEAD_DIM), lambda b, hp: (b, 0, 0), pipeline_mode=once)
    vec = pl.BlockSpec((1, HEAD_DIM), lambda b, hp: (0, 0))
    return pl.pallas_call(
        functools.partial(_attn_kernel, seq=seq),
        grid=(bsz, HEADS // HEAD_PAIR),
        in_specs=[hspec, hspec, hspec, *[wspec(g) for g in range(N_PATTERNS)],
                  tspec, tspec, tspec, vec, vec],
        out_specs=pl.BlockSpec((1, seq, HEAD_PAIR * HEAD_DIM), lambda b, hp: (b, 0, hp)),
        out_shape=jax.ShapeDtypeStruct((bsz, seq, HEADS * HEAD_DIM), BF16),
        scratch_shapes=[qkv, qkv, qkv, stat, stat,
                        pltpu.VMEM((2, WINDOW_KEYS, 2 * WINDOW_KEYS), F32)],
        compiler_params=_params(2, ATTN_VMEM_LIMIT),
        name="attention",
    )(*hs, *[w_qkv] * N_PATTERNS, *tables, q_norm_w.reshape(1, HEAD_DIM), k_norm_w.reshape(1, HEAD_DIM))


def _qkv_cast_kernel(*refs):
    o_ref = refs[-1]
    for slot, ref in enumerate(refs[:-1]):
        o_ref[:, slot * HEAD_DIM:(slot + 1) * HEAD_DIM] = ref[...].astype(BF16)


def _pair_major_qkv(w_all, layer):
    n_heads = N_PATTERNS * HEADS
    part = lambda hl, p: pl.BlockSpec((None, D_MODEL, HEAD_DIM),
                                      lambda j: (layer, 0, p * n_heads + HEAD_PAIR * j + hl))
    specs = [part(hl, p) for hl in range(HEAD_PAIR) for p in range(3)]
    return pl.pallas_call(
        _qkv_cast_kernel,
        grid=(n_heads // HEAD_PAIR,),
        in_specs=specs,
        out_specs=pl.BlockSpec((D_MODEL, HEAD_PAIR * 3 * HEAD_DIM), lambda j: (0, j)),
        out_shape=jax.ShapeDtypeStruct((D_MODEL, 3 * ATTN_QKV), BF16),
        compiler_params=_params(1),
        name="qkv_weights",
    )(*[w_all] * len(specs))


def kernel(x, c, positions, mod_w, mod_b, norm_w, even_w_in, conv_dw_w, conv_dw_b, conv_ln_w, conv_ln_b, ssm_lam_re, ssm_lam_im, ssm_log_dt, ssm_b_re, ssm_b_im, ssm_c_re, ssm_c_im, ssm_d, ssm_glu_w, ssm_glu_b, even_w_out, attn_w_in, attn_q_norm_w, attn_k_norm_w, attn_w_out):
    bsz, seq, _ = x.shape
    assert seq == 2048 and x.shape[-1] == D_MODEL and bsz % SSM_BATCH == 0
    mod = _modulation(c, mod_w, mod_b)
    shift = [mod[l, :, None, :D_MODEL] for l in range(DEPTH)]
    scale = [mod[l, :, None, D_MODEL:2 * D_MODEL] for l in range(DEPTH)]
    gate = [mod[l, :, None, 2 * D_MODEL:] for l in range(DEPTH)]
    nw = [norm_w[l].reshape(1, D_MODEL) for l in range(DEPTH)]
    assert DEPTH % 2 == 0
    tables = _rope_tables(positions)
    ssm_mats = jax.vmap(_ssm_matrices)(ssm_lam_re, ssm_lam_im, ssm_log_dt, ssm_b_re, ssm_b_im, ssm_c_re, ssm_c_im)

    hs = None
    for layer in range(DEPTH):
        i = layer // 2
        nxt = (nw[layer + 1], scale[layer + 1], shift[layer + 1]) if layer + 1 < DEPTH else None
        if layer % 2 == 0:
            src, norm = (x, (nw[0], scale[0], shift[0])) if layer == 0 else (hs[0], None)
            ya, u, bz = _even_in(src, even_w_in[i].astype(BF16), conv_dw_w[i], conv_dw_b[i],
                                 conv_ln_w[i], conv_ln_b[i], norm)
            ys = _ssm(u, ssm_mats, i, ssm_d[i])
            w_out = even_w_out[i].astype(BF16)
            x, hs = _tail_call(_even_out_kernel, "even_out", x, [ya, ys, bz],
                               [ssm_glu_w[i].astype(BF16), ssm_glu_b[i].reshape(1, SSM_WIDTH),
                                w_out[:CONV_WIDTH], w_out[CONV_WIDTH:]], gate[layer], nxt, permuted=True)
        else:
            o = _attention(hs, _pair_major_qkv(attn_w_in, i), tables, attn_q_norm_w[i], attn_k_norm_w[i])
            width = HEADS * HEAD_DIM
            gate_cols = pl.BlockSpec((None, D_MODEL, width), lambda b, t, i=i: (i, 0, 3 * ATTN_QKV // width))
            x, hs = _tail_call(_odd_out_kernel, "odd_out", x, [hs[0], o],
                               [(attn_w_in, gate_cols), attn_w_out[i].astype(BF16)],
                               gate[layer], nxt, permuted=False)
    return x
```

```python
import functools
import math

import jax
import jax.numpy as jnp
from jax import lax
from jax.experimental import pallas as pl
from jax.experimental.pallas import tpu as pltpu

F32 = jnp.float32
BF16 = jnp.bfloat16

D_MODEL = 1024
DEPTH = 4
CONV_WIDTH = 512
CONV_KERNEL = 31
SSM_WIDTH = 512
SSM_GROUP = 16
SSM_STATE = 64
HEAD_DIM = 128
SEQ_LEN = 2048
ATTN_PATTERNS = ((128, 1), (512, 4), (SEQ_LEN, 16))
N_PATTERNS = 3
HEADS = 8
ATTN_QKV = N_PATTERNS * HEADS * HEAD_DIM
ATTN_SCALE = HEAD_DIM ** -0.5
ROPE_THETA = 10000.0
EPS = 1e-6
NEG_INF = -1e30

LANES = 128
WINDOW_KEYS = 128
SSM_CHUNK = 16
SSM_LANE_GROUPS = LANES // SSM_GROUP
SSM_BLOCKS = SSM_WIDTH // LANES
SSM_STATE_COLS = SSM_LANE_GROUPS * SSM_STATE
SSM_BATCH = 2
ROW_TILE = 512
CONV_HALO = 32
CONV_ROWS = 64
PROJ_ROWS = 256
VMEM_LIMIT = 56 * 1024 * 1024
ATTN_VMEM_LIMIT = 58 * 1024 * 1024
HEAD_PAIR = 2


def _params(n_axes, vmem=VMEM_LIMIT):
    return pltpu.CompilerParams(dimension_semantics=("arbitrary",) * n_axes, vmem_limit_bytes=vmem)


def _adaln(x, nw, scale, shift):
    ms = jnp.mean(x * x, axis=-1, keepdims=True)
    return (x * lax.rsqrt(ms + EPS) * nw) * (1.0 + scale) + shift


def _dot(a, b):
    return jnp.dot(a, b, preferred_element_type=F32)


def _mod_kernel(c_ref, w_ref, b_ref, o_ref):
    o_ref[0] = _dot(c_ref[...].astype(BF16), w_ref[0].astype(BF16)) + b_ref[0]


def _modulation(c, mod_w, mod_b):
    bsz = c.shape[0]
    nblk = 3
    return pl.pallas_call(
        _mod_kernel,
        grid=(DEPTH, nblk),
        in_specs=[pl.BlockSpec((bsz, D_MODEL), lambda l, j: (0, 0)),
                  pl.BlockSpec((1, D_MODEL, D_MODEL), lambda l, j: (l, 0, j)),
                  pl.BlockSpec((1, 1, D_MODEL), lambda l, j: (l, 0, j))],
        out_specs=pl.BlockSpec((1, bsz, D_MODEL), lambda l, j: (l, 0, j)),
        out_shape=jax.ShapeDtypeStruct((DEPTH, bsz, 3 * D_MODEL), F32),
        compiler_params=_params(2),
        name="modulation",
    )(c, mod_w, mod_b.reshape(DEPTH, 1, 3 * D_MODEL))


def _rope_kernel(pos_ref, inv_ref, tab_ref, tab1_ref, tab2_ref, stage_scr, stage2_scr):
    half_rows, half_lanes = ROW_TILE // 2, HEAD_DIM // 2
    pos = pos_ref[0].astype(F32)
    lower = lax.broadcasted_iota(jnp.int32, (half_rows, HEAD_DIM), 1) < half_lanes
    ang = jnp.where(lower, pos[:half_rows], pos[half_rows:]) * inv_ref[...]
    cos, sin = jnp.cos(ang), jnp.sin(ang)
    tab = jnp.concatenate([jnp.where(lower, cos, pltpu.roll(sin, half_lanes, 1)),
                           jnp.where(lower, pltpu.roll(cos, half_lanes, 1), sin)], axis=0)
    tab_ref[0] = tab
    d1, d2 = ATTN_PATTERNS[1][1], ATTN_PATTERNS[2][1]
    assert d2 == d1 * d1
    n1, n2 = ROW_TILE // d1, ROW_TILE // d2
    stage_scr[...] = tab
    for r in range(d1):
        part = stage_scr[pl.ds(r, n1, stride=d1), :]
        tab1_ref[0, r] = part
        stage2_scr[r] = part
    for r in range(d1):
        for a in range(d1):
            tab2_ref[0, a * d1 + r] = stage2_scr[r, pl.ds(a, n2, stride=d1), :]


def _rope_tables(positions):
    bsz, seq = positions.shape
    inv = ROPE_THETA ** (-jnp.arange(0, HEAD_DIM, 2, dtype=F32) / HEAD_DIM)
    inv2 = jnp.concatenate([inv, inv]).reshape(1, HEAD_DIM)
    d1, d2 = ATTN_PATTERNS[1][1], ATTN_PATTERNS[2][1]
    perm = lambda d: (pl.BlockSpec((1, d, ROW_TILE // d, HEAD_DIM), lambda b, i: (b, 0, i, 0)),
                      jax.ShapeDtypeStruct((bsz, d, seq // d, HEAD_DIM), F32))
    specs, shapes = zip((pl.BlockSpec((1, ROW_TILE, HEAD_DIM), lambda b, i: (b, i, 0)),
                         jax.ShapeDtypeStruct((bsz, seq, HEAD_DIM), F32)), perm(d1), perm(d2))
    tabs = pl.pallas_call(
        _rope_kernel,
        grid=(bsz, seq // ROW_TILE),
        in_specs=[pl.BlockSpec((1, ROW_TILE, 1), lambda b, i: (b, i, 0)),
                  pl.BlockSpec((1, HEAD_DIM), lambda b, i: (0, 0))],
        out_specs=list(specs),
        out_shape=list(shapes),
        scratch_shapes=[pltpu.VMEM((ROW_TILE, HEAD_DIM), F32), pltpu.VMEM((d1, ROW_TILE // d1, HEAD_DIM), F32)],
        compiler_params=_params(2),
        name="rope_tables",
    )(positions.reshape(bsz, seq, 1), inv2)
    return [t.reshape(bsz, seq, HEAD_DIM) for t in tabs]


def _even_in_kernel(*refs, from_x):
    if from_x:
        x_ref, nw_ref, sc_ref, sh_ref = refs[:4]
        h = _adaln(x_ref[0], nw_ref[...], sc_ref[0], sh_ref[0]).astype(BF16)
        refs = refs[4:]
    else:
        h = refs[0][0]
        refs = refs[1:]
    w_ref, dww_ref, dwb_ref, lnw_ref, lnb_ref, ya_ref, u_ref, bz_ref, conv_scr, y_scr, az_scr = refs
    i = pl.program_id(1)
    cw = CONV_WIDTH
    slabs = cw // LANES

    @pl.when(i == 0)
    def _():
        conv_scr[:, 0:CONV_HALO, :] = jnp.zeros((slabs, CONV_HALO, LANES), F32)

    a1 = _dot(h, w_ref[:, 0:cw])
    a2 = _dot(h, w_ref[:, cw:2 * cw])
    a = a1 * jax.nn.sigmoid(a2)
    for l in range(slabs):
        conv_scr[l, CONV_HALO:CONV_HALO + ROW_TILE, :] = a[:, l * LANES:(l + 1) * LANES]
    az_scr[...] = jax.nn.silu(_dot(h, w_ref[:, 2 * cw:3 * cw]))
    u_ref[0] = _dot(h, w_ref[:, 3 * cw:4 * cw])
    bz_ref[0] = _dot(h, w_ref[:, 4 * cw:5 * cw])

    first_tap = CONV_HALO - (CONV_KERNEL - 1)
    half = CONV_ROWS // 2

    def conv_block(r, carry):
        r0 = r * CONV_ROWS
        for l in range(slabs):
            lanes = slice(l * LANES, (l + 1) * LANES)
            for par in range(2):
                acc = jnp.broadcast_to(dwb_ref[:, lanes], (half, LANES))
                for k in range(CONV_KERNEL):
                    tap = conv_scr[l, pl.ds(r0 + first_tap + par + k, half, stride=2), :]
                    acc = acc + dww_ref[k:k + 1, lanes] * tap
                y_scr[l, pl.ds(r0 + par, half, stride=2), :] = acc
        return carry

    lax.fori_loop(0, ROW_TILE // CONV_ROWS, conv_block, 0)
    for l in range(slabs):
        conv_scr[l, 0:CONV_HALO, :] = conv_scr[l, ROW_TILE:ROW_TILE + CONV_HALO, :]

    def norm_block(r, carry):
        rows = pl.ds(pl.multiple_of(r * CONV_ROWS, CONV_ROWS), CONV_ROWS)
        acc = jnp.concatenate([y_scr[l, rows, :] for l in range(slabs)], axis=-1)
        mu = jnp.mean(acc, axis=-1, keepdims=True)
        xc = acc - mu
        y = xc * lax.rsqrt(jnp.mean(xc * xc, axis=-1, keepdims=True) + EPS)
        y = y * lnw_ref[...] + lnb_ref[...]
        ya_ref[0, rows, :] = (jax.nn.silu(y) * az_scr[rows, :]).astype(BF16)
        return carry

    lax.fori_loop(0, ROW_TILE // CONV_ROWS, norm_block, 0, unroll=4)


def _even_in(h, w_in, dw_w, dw_b, ln_w, ln_b, norm=None):
    bsz, seq, _ = h.shape
    cw = CONV_WIDTH
    const = lambda shape: pl.BlockSpec(shape, lambda b, i: (0,) * len(shape))
    row = lambda width: pl.BlockSpec((1, ROW_TILE, width), lambda b, i: (b, i, 0))
    vec = pl.BlockSpec((1, 1, D_MODEL), lambda b, i: (b, 0, 0))
    norm_args = [] if norm is None else list(norm)
    norm_specs = [] if norm is None else [const((1, D_MODEL)), vec, vec]
    return pl.pallas_call(
        functools.partial(_even_in_kernel, from_x=norm is not None),
        grid=(bsz, seq // ROW_TILE),
        in_specs=[row(D_MODEL), *norm_specs, const(w_in.shape), const((CONV_KERNEL, cw)),
                  const((1, cw)), const((1, cw)), const((1, cw))],
        out_specs=[row(cw), row(cw), row(cw)],
        out_shape=[jax.ShapeDtypeStruct((bsz, seq, cw), BF16),
                   jax.ShapeDtypeStruct((bsz, seq, cw), F32),
                   jax.ShapeDtypeStruct((bsz, seq, cw), F32)],
        scratch_shapes=[pltpu.VMEM((cw // LANES, CONV_HALO + ROW_TILE, LANES), F32),
                        pltpu.VMEM((cw // LANES, ROW_TILE, LANES), F32),
                        pltpu.VMEM((ROW_TILE, cw), F32)],
        compiler_params=_params(2),
        name="even_in",
    )(h, *norm_args, w_in, dw_w, dw_b.reshape(1, cw), ln_w.reshape(1, cw), ln_b.reshape(1, cw))


def _ssm_matrices(lam_re, lam_im, log_dt, b_re, b_im, c_re, c_im):
    hp = lax.Precision.HIGHEST
    t = SSM_CHUNK
    lr, li = lam_re.astype(F32), lam_im.astype(F32)
    dt = jnp.exp(log_dt.astype(F32))[:, None]

    def a_pow(k):
        kk = k.astype(F32)[:, None, None]
        mag = jnp.exp(kk * (lr * dt)[None])
        ang = kk * (li * dt)[None]
        return mag * jnp.cos(ang), mag * jnp.sin(ang)

    ar, ai = a_pow(jnp.ones((1,), F32))
    ar, ai = ar[0], ai[0]
    den = lr * lr + li * li
    nr = ar - 1.0
    kr = (nr * lr + ai * li) / den
    ki = (ai * lr - nr * li) / den
    br, bi = b_re.astype(F32), b_im.astype(F32)
    bbr = kr[..., None] * br - ki[..., None] * bi
    bbi = kr[..., None] * bi + ki[..., None] * br
    cr, ci = c_re.astype(F32), c_im.astype(F32)

    pr, pi = a_pow(jnp.arange(t + 1))
    wr = pr[:t, :, :, None] * bbr[None] - pi[:t, :, :, None] * bbi[None]
    wi = pr[:t, :, :, None] * bbi[None] + pi[:t, :, :, None] * bbr[None]
    kk = (jnp.einsum('gop,kgpi->kgoi', cr, wr, precision=hp)
          - jnp.einsum('gop,kgpi->kgoi', ci, wi, precision=hp))

    nb, gl, hh, pp = SSM_BLOCKS, SSM_LANE_GROUPS, SSM_GROUP, SSM_STATE
    grp = jnp.arange(gl)[:, None, None]
    row = jnp.arange(t * hh)[None, :, None]
    col = jnp.arange(t * LANES)[None, None, :]
    place = ((row // hh == col // LANES) & (row % hh == col % hh) & ((col % LANES) // hh == grp)).astype(BF16)
    prow = jnp.arange(pp)[None, :, None]
    pcol = jnp.arange(gl * pp)[None, None, :]
    spread = ((pcol % pp == prow) & (pcol // pp == grp)).astype(BF16)

    def rows_sgi(m):
        n = m.shape[-1]
        return m.reshape(nb, gl, t, hh, n).transpose(0, 2, 1, 3, 4).reshape(nb, t * LANES, n)

    lrow = jnp.arange(t * hh)[None, :, None]
    lcol = jnp.arange(t * hh)[None, None, :]
    s_idx = jnp.arange(t)[:, None, None]
    shift = ((lcol // hh == lrow // hh + s_idx) & (lcol % hh == lrow % hh)).astype(BF16)
    kk_i = kk.reshape(t, nb, gl, hh, hh).transpose(1, 2, 4, 0, 3).reshape(nb, gl, hh, t * hh)
    toe = jnp.einsum('cgil,sln->cgsin', kk_i.astype(BF16), shift,
                     preferred_element_type=BF16).reshape(nb, gl, t * hh, t * hh)
    m_intra = rows_sgi(jnp.einsum('cgab,gbn->cgan', toe, place, preferred_element_type=BF16))

    def state_cols(w):
        w = w[::-1].reshape(t, nb, gl, pp, hh).transpose(1, 2, 0, 4, 3).reshape(nb, gl, t * hh, pp)
        return rows_sgi(jnp.einsum('cgap,gpn->cgan', w.astype(BF16), spread, preferred_element_type=BF16))

    m_state_r, m_state_i = state_cols(wr), state_cols(wi)

    def in_rows(q):
        q = q.reshape(t, nb, gl, hh, pp).transpose(1, 2, 4, 0, 3).reshape(nb, gl, pp, t * hh)
        return jnp.einsum('cgpb,gbn->cgpn', q.astype(BF16), place,
                          preferred_element_type=BF16).reshape(nb, gl * pp, t * LANES)

    m_in_r = in_rows(pr[1:, :, None, :] * cr[None] - pi[1:, :, None, :] * ci[None])
    m_in_i = in_rows(-(pr[1:, :, None, :] * ci[None] + pi[1:, :, None, :] * cr[None]))

    n_lvl = int(math.log2(SEQ_LEN // t))
    sr2, si2 = a_pow(t * (2 ** jnp.arange(n_lvl)))
    ap_r = sr2.reshape(n_lvl, nb, gl * pp).transpose(1, 0, 2)
    ap_i = si2.reshape(n_lvl, nb, gl * pp).transpose(1, 0, 2)
    return m_intra, m_state_r, m_state_i, m_in_r, m_in_i, ap_r, ap_i


def _ssm_kernel(u_ref, mintra_ref, msr_ref, msi_ref, minr_ref, mini_ref, apr_ref, api_ref, d_ref, y_ref,
                x_scr, yi_scr, cr_scr, ci_scr, zr_scr, zi_scr, *, n_chunks, n_levels):
    t = SSM_CHUNK
    sc = SSM_STATE_COLS
    pair = 2 * LANES
    for bb in range(SSM_BATCH):
        for t0 in range(t):
            x_scr[bb * n_chunks:(bb + 1) * n_chunks, t0 * LANES:(t0 + 1) * LANES] = (
                u_ref[bb, pl.ds(t0, n_chunks, stride=t), :].astype(BF16))
    s_re = _dot(x_scr[...], msr_ref[0])
    s_im = _dot(x_scr[...], msi_ref[0])
    for tp in range(t // 2):
        cols = slice(tp * pair, (tp + 1) * pair)
        yi_scr[:, cols] = _dot(x_scr[:, 0:(tp + 1) * pair], mintra_ref[0, 0:(tp + 1) * pair, cols])

    zero = jnp.zeros((n_chunks, sc), F32)
    for p in range(2):
        zr_scr[p, 0:n_chunks, :] = zero
        zi_scr[p, 0:n_chunks, :] = zero
    for bb in range(SSM_BATCH):
        rows = slice(bb * n_chunks, (bb + 1) * n_chunks)
        zr_scr[0, n_chunks:, :] = s_re[rows]
        zi_scr[0, n_chunks:, :] = s_im[rows]
        for k in range(n_levels):
            src, dst = k % 2, 1 - (k % 2)
            sh = n_chunks - (1 << k)
            zr = zr_scr[src, n_chunks:, :]
            zi = zi_scr[src, n_chunks:, :]
            pr = zr_scr[src, sh:sh + n_chunks, :]
            pi = zi_scr[src, sh:sh + n_chunks, :]
            ar = apr_ref[0, k:k + 1, :]
            ai = api_ref[0, k:k + 1, :]
            zr_scr[dst, n_chunks:, :] = zr + ar * pr - ai * pi
            zi_scr[dst, n_chunks:, :] = zi + ar * pi + ai * pr
        fin = n_levels % 2
        cr_scr[rows, :] = zr_scr[fin, n_chunks - 1:2 * n_chunks - 1, :].astype(BF16)
        ci_scr[rows, :] = zi_scr[fin, n_chunks - 1:2 * n_chunks - 1, :].astype(BF16)

    carry_r = cr_scr[...]
    carry_i = ci_scr[...]
    for tp in range(t // 2):
        cols = slice(tp * pair, (tp + 1) * pair)
        yc = yi_scr[:, cols] + _dot(carry_r, minr_ref[0, :, cols]) + _dot(carry_i, mini_ref[0, :, cols])
        for bb in range(SSM_BATCH):
            for j in range(2):
                y_ref[bb, pl.ds(2 * tp + j, n_chunks, stride=t), :] = (
                    yc[bb * n_chunks:(bb + 1) * n_chunks, j * LANES:(j + 1) * LANES])
    for bb in range(SSM_BATCH):
        y_ref[bb] = y_ref[bb] + d_ref[0] * u_ref[bb]


def _ssm(u, mats, layer, d_skip):
    m_intra, m_state_r, m_state_i, m_in_r, m_in_i, ap_r, ap_i = mats
    bsz, seq, _ = u.shape
    n_chunks = seq // SSM_CHUNK
    n_levels = ap_r.shape[2]
    nb = SSM_BATCH
    once = pl.Buffered(1)
    blk = pl.BlockSpec((nb, seq, LANES), lambda c, b: (b, 0, c))
    mat = lambda m: pl.BlockSpec((None, 1) + m.shape[2:], lambda c, b: (layer, c, 0, 0), pipeline_mode=once)
    lvl = pl.BlockSpec((None, 1, n_levels, SSM_STATE_COLS), lambda c, b: (layer, c, 0, 0))
    return pl.pallas_call(
        functools.partial(_ssm_kernel, n_chunks=n_chunks, n_levels=n_levels),
        grid=(SSM_BLOCKS, bsz // nb),
        in_specs=[blk, mat(m_intra), mat(m_state_r), mat(m_state_i), mat(m_in_r), mat(m_in_i), lvl, lvl,
                  pl.BlockSpec((1, 1, LANES), lambda c, b: (c, 0, 0))],
        out_specs=blk,
        out_shape=jax.ShapeDtypeStruct(u.shape, F32),
        scratch_shapes=[pltpu.VMEM((nb * n_chunks, SSM_CHUNK * LANES), BF16),
                        pltpu.VMEM((nb * n_chunks, SSM_CHUNK * LANES), F32),
                        pltpu.VMEM((nb * n_chunks, SSM_STATE_COLS), BF16),
                        pltpu.VMEM((nb * n_chunks, SSM_STATE_COLS), BF16),
                        pltpu.VMEM((2, 2 * n_chunks, SSM_STATE_COLS), F32),
                        pltpu.VMEM((2, 2 * n_chunks, SSM_STATE_COLS), F32)],
        compiler_params=_params(2),
        name="ssm",
    )(u, m_intra, m_state_r, m_state_i, m_in_r, m_in_i, ap_r, ap_i, d_skip.reshape(SSM_BLOCKS, 1, LANES))


def _finish(x_ref, gate_ref, out, nxt, xo_ref, h_outs, stage_scr, stage2_scr):
    xn = x_ref[0] + gate_ref[0] * out
    xo_ref[0] = xn
    if nxt is None:
        return
    nw_ref, sc_ref, sh_ref = nxt
    hn = _adaln(xn, nw_ref[...], sc_ref[0], sh_ref[0])
    h_outs[0][0] = hn.astype(BF16)
    if len(h_outs) == 1:
        return
    slabs = D_MODEL // LANES
    d1, d2 = ATTN_PATTERNS[1][1], ATTN_PATTERNS[2][1]
    assert d2 == d1 * d1 and len(h_outs) == 3
    n1, n2 = ROW_TILE // d1, ROW_TILE // d2
    for l in range(slabs):
        stage_scr[l] = hn[:, l * LANES:(l + 1) * LANES]
    for r in range(d1):
        parts = [stage_scr[l, pl.ds(r, n1, stride=d1), :] for l in range(slabs)]
        h_outs[1][0, r] = jnp.concatenate(parts, axis=-1).astype(BF16)
        for l in range(slabs):
            stage2_scr[r * slabs + l] = parts[l]
    for r in range(d1):
        for a in range(d1):
            parts = [stage2_scr[r * slabs + l, pl.ds(a, n2, stride=d1), :] for l in range(slabs)]
            h_outs[2][0, a * d1 + r] = jnp.concatenate(parts, axis=-1).astype(BF16)


def _even_out_kernel(*refs, n_h):
    x_ref, ya_ref, ys_ref, bz_ref, gw_ref, gb_ref, wa_ref, ws_ref, gate_ref = refs[:9]
    nxt, xo_ref, h_outs, stages = _tail_refs(refs[9:], n_h)
    y = ys_ref[0]
    y = 0.5 * y * (1.0 + lax.erf(y * (2.0 ** -0.5)))
    y = y * jax.nn.sigmoid(_dot(y.astype(BF16), gw_ref[...]) + gb_ref[...])
    y = y * jax.nn.silu(bz_ref[0])
    out = _dot(ya_ref[0], wa_ref[...]) + _dot(y.astype(BF16), ws_ref[...])
    _finish(x_ref, gate_ref, out, nxt, xo_ref, h_outs, *stages)


def _odd_out_kernel(*refs, n_h):
    x_ref, h_ref, o_ref, wz_ref, wo_ref, gate_ref = refs[:6]
    nxt, xo_ref, h_outs, stages = _tail_refs(refs[6:], n_h)
    z = _dot(h_ref[0], wz_ref[...].astype(BF16))
    g = o_ref[0] * jax.nn.silu(z)
    out = _dot(g.astype(BF16), wo_ref[...])
    _finish(x_ref, gate_ref, out, nxt, xo_ref, h_outs, *stages)


def _tail_refs(refs, n_h):
    if n_h == 0:
        return None, refs[0], (), (None, None)
    nxt, xo_ref, h_outs = refs[:3], refs[3], refs[4:4 + n_h]
    return nxt, xo_ref, h_outs, (tuple(refs[4 + n_h:6 + n_h]) if n_h > 1 else (None, None))


def _tail_call(body, name, x, rows, consts, gate, nxt, permuted):
    bsz, seq, _ = x.shape
    n_h = 0 if nxt is None else (N_PATTERNS if permuted else 1)
    row = lambda a: pl.BlockSpec((1, ROW_TILE, a.shape[-1]), lambda b, i: (b, i, 0))
    const = lambda a: pl.BlockSpec(a.shape, lambda b, i: (0,) * a.ndim)
    vec = pl.BlockSpec((1, 1, D_MODEL), lambda b, i: (b, 0, 0))
    consts = [a if isinstance(a, tuple) else (a, const(a)) for a in consts]
    args = [x, *rows, *[a for a, _ in consts], gate]
    in_specs = [row(x)] + [row(a) for a in rows] + [spec for _, spec in consts] + [vec]
    out_specs = [row(x)]
    out_shape = [jax.ShapeDtypeStruct(x.shape, F32)]
    scratch = []
    if n_h:
        nw, sc, sh = nxt
        args += [nw, sc, sh]
        in_specs += [const(nw), vec, vec]
        out_specs.append(row(x))
        out_shape.append(jax.ShapeDtypeStruct(x.shape, BF16))
    if n_h > 1:
        for _, dil in ATTN_PATTERNS[1:]:
            out_specs.append(pl.BlockSpec((1, dil, ROW_TILE // dil, D_MODEL), lambda b, i: (b, 0, i, 0)))
            out_shape.append(jax.ShapeDtypeStruct((bsz, dil, seq // dil, D_MODEL), BF16))
        d1 = ATTN_PATTERNS[1][1]
        scratch.append(pltpu.VMEM((D_MODEL // LANES, ROW_TILE, LANES), F32))
        scratch.append(pltpu.VMEM((d1 * D_MODEL // LANES, ROW_TILE // d1, LANES), F32))
    res = pl.pallas_call(
        functools.partial(body, n_h=n_h),
        grid=(bsz, seq // ROW_TILE),
        in_specs=in_specs, out_specs=out_specs, out_shape=out_shape, scratch_shapes=scratch,
        compiler_params=_params(2),
        name=name,
    )(*args)
    return res[0], [h.reshape(x.shape) for h in res[1:]]


def _attn_kernel(h0_ref, h1_ref, h2_ref, *refs, seq):
    w_refs = refs[:N_PATTERNS]
    (t0_ref, t1_ref, t2_ref, qw_ref, kw_ref, o_ref,
     q_scr, k_scr, v_scr, on_scr, lse_scr, bias_scr) = refs[N_PATTERNS:]
    wk = WINDOW_KEYS
    n_chunks = seq // PROJ_ROWS
    chunk_blocks = PROJ_ROWS // wk
    h_refs = (h0_ref, h1_ref, h2_ref)
    tabs = (t0_ref, t1_ref, t2_ref)

    qi = lax.broadcasted_iota(jnp.int32, (wk, 2 * wk), 0)
    kj = lax.broadcasted_iota(jnp.int32, (wk, 2 * wk), 1)
    band = (kj >= qi) & (kj <= qi + wk)
    bias_scr[0] = jnp.where(band, 0.0, NEG_INF)
    bias_scr[1] = jnp.where(band & (kj >= wk), 0.0, NEG_INF)

    def chunk_rows(c):
        start = c * PROJ_ROWS
        if not isinstance(c, int):
            start = pl.multiple_of(start, PROJ_ROWS)
        return pl.ds(start, PROJ_ROWS)

    def proj_dots(c):
        rows = chunk_rows(c)
        return [_dot(h_refs[g][0, rows, :], w_refs[g][...]) for g in range(N_PATTERNS)]

    def proj_store(c, prs):
        rows = chunk_rows(c)
        for g, pr in enumerate(prs):
            tab = tabs[g][0, rows, :]
            swapped = pltpu.roll(tab, HEAD_DIM // 2, 1)
            lower = lax.broadcasted_iota(jnp.int32, tab.shape, 1) < HEAD_DIM // 2
            cos = jnp.where(lower, tab, swapped)
            sin = jnp.where(lower, -swapped, tab)

            def norm_rope(t, w):
                t = t * lax.rsqrt(jnp.mean(t * t, axis=-1, keepdims=True) + EPS) * w
                return t * cos + pltpu.roll(t, HEAD_DIM // 2, 1) * sin

            for hl in range(HEAD_PAIR):
                off = hl * 3 * HEAD_DIM
                slot = g * HEAD_PAIR + hl
                q_scr[slot, rows, :] = (norm_rope(pr[:, off:off + HEAD_DIM], qw_ref[...]) * ATTN_SCALE).astype(BF16)
                k_scr[slot, rows, :] = norm_rope(pr[:, off + HEAD_DIM:off + 2 * HEAD_DIM], kw_ref[...]).astype(BF16)
                v_scr[slot, rows, :] = pr[:, off + 2 * HEAD_DIM:off + 3 * HEAD_DIM].astype(BF16)

    def block_ids(c):
        return [(g, hl, chunk_blocks * c + u, u) for g in range(N_PATTERNS) for hl in range(HEAD_PAIR)
                for u in range(chunk_blocks)]

    def is_first(g, j, u):
        n_blk = seq // ATTN_PATTERNS[g][1] // wk
        if n_blk == 1:
            return True
        if u % 2 == 1:
            return False
        if isinstance(j, int):
            return j % n_blk == 0
        return (j % n_blk) == 0

    def block_start(j):
        start = j * wk
        return start if isinstance(start, int) else pl.multiple_of(start, wk)

    def scores(g, hl, j, u):
        first = is_first(g, j, u)
        slot = g * HEAD_PAIR + hl
        q = q_scr[slot, pl.ds(block_start(j), wk), :]
        if first is True:
            kk = k_scr[slot, pl.ds(block_start(j), wk), :]
            bias = bias_scr[1, :, wk:]
        else:
            kk = k_scr[slot, pl.ds(block_start(j - 1), 2 * wk), :]
            bias = bias_scr[0] if first is False else bias_scr[jnp.where(first, 1, 0)]
        return lax.dot_general(q, kk, (((1,), (1,)), ((), ())), preferred_element_type=F32) + bias

    def softmax(s):
        m = jnp.max(s, axis=-1, keepdims=True)
        return jnp.exp(s - m).astype(BF16), m

    def weighted(g, hl, j, u, p):
        slot = g * HEAD_PAIR + hl
        if is_first(g, j, u) is True:
            vv = v_scr[slot, pl.ds(block_start(j), wk), :]
        else:
            vv = v_scr[slot, pl.ds(block_start(j - 1), 2 * wk), :]
        ov = _dot(p, jnp.concatenate([vv, jnp.ones_like(vv)], axis=1))
        return ov[:, :HEAD_DIM], ov[:, HEAD_DIM:]

    def attn_store(g, hl, j, o, m, den):
        dil = ATTN_PATTERNS[g][1]
        n_blk = seq // dil // wk
        slot = g * HEAD_PAIR + hl
        if dil == 1:
            nat = pl.ds(block_start(j), wk)
        else:
            nat = pl.ds((j % n_blk) * (wk * dil) + j // n_blk, wk, stride=dil)
        on_scr[slot, nat, :] = o * (1.0 / den)
        lse_scr[slot, nat, :] = m + jnp.log(den)

    def step(c_attn, c_proj):
        ids = block_ids(c_attn) if c_attn is not None else []
        ss = [scores(g, hl, j, u) for g, hl, j, u in ids]
        prs = proj_dots(c_proj) if c_proj is not None else None
        sm = [softmax(s) for s in ss]
        os_ = [weighted(g, hl, j, u, p) for (g, hl, j, u), (p, _) in zip(ids, sm)]
        if prs is not None:
            proj_store(c_proj, prs)
        for (g, hl, j, u), (o, den), (_, m) in zip(ids, os_, sm):
            attn_store(g, hl, j, o, m, den)

    step(None, 0)
    step(0, 1)

    def body(c, carry):
        step(c - 1, c)
        return carry

    lax.fori_loop(2, n_chunks, body, 0)
    step(n_chunks - 1, None)

    def merge_body(c, carry):
        rows = pl.ds(pl.multiple_of(c * PROJ_ROWS, PROJ_ROWS), PROJ_ROWS)
        for hl in range(HEAD_PAIR):
            ls = [lse_scr[g * HEAD_PAIR + hl, rows, :] for g in range(N_PATTERNS)]
            mx = jnp.maximum(jnp.maximum(ls[0], ls[1]), ls[2])
            num = jnp.zeros((PROJ_ROWS, HEAD_DIM), F32)
            den = jnp.zeros((PROJ_ROWS, HEAD_DIM), F32)
            for g in range(N_PATTERNS):
                wgt = jnp.exp(ls[g] - mx)
                num = num + wgt * on_scr[g * HEAD_PAIR + hl, rows, :]
                den = den + wgt
            o_ref[0, rows, hl * HEAD_DIM:(hl + 1) * HEAD_DIM] = (num / den).astype(BF16)
        return carry

    lax.fori_loop(0, seq // PROJ_ROWS, merge_body, 0)


def _attention(hs, w_qkv, tables, q_norm_w, k_norm_w):
    bsz, seq, _ = hs[0].shape
    pair_cols = HEAD_PAIR * 3 * HEAD_DIM
    wspec = lambda g: pl.BlockSpec((D_MODEL, pair_cols), lambda b, hp: (0, g * (HEADS // HEAD_PAIR) + hp))
    slots = N_PATTERNS * HEAD_PAIR
    stat = pltpu.VMEM((slots, seq, HEAD_DIM), F32)
    qkv = pltpu.VMEM((slots, seq, HEAD_DIM), BF16)
    once = pl.Buffered(1)
    hspec = pl.BlockSpec((1, seq, D_MODEL), lambda b, hp: (b, 0, 0), pipeline_mode=once)
    tspec = pl.BlockSpec((1, seq, HEAD_DIM), lambda b, hp: (b, 0, 0), pipeline_mode=once)
    vec = pl.BlockSpec((1, HEAD_DIM), lambda b, hp: (0, 0))
    return pl.pallas_call(
        functools.partial(_attn_kernel, seq=seq),
        grid=(bsz, HEADS // HEAD_PAIR),
        in_specs=[hspec, hspec, hspec, *[wspec(g) for g in range(N_PATTERNS)],
                  tspec, tspec, tspec, vec, vec],
        out_specs=pl.BlockSpec((1, seq, HEAD_PAIR * HEAD_DIM), lambda b, hp: (b, 0, hp)),
        out_shape=jax.ShapeDtypeStruct((bsz, seq, HEADS * HEAD_DIM), BF16),
        scratch_shapes=[qkv, qkv, qkv, stat, stat,
                        pltpu.VMEM((2, WINDOW_KEYS, 2 * WINDOW_KEYS), F32)],
        compiler_params=_params(2, ATTN_VMEM_LIMIT),
        name="attention",
    )(*hs, *[w_qkv] * N_PATTERNS, *tables, q_norm_w.reshape(1, HEAD_DIM), k_norm_w.reshape(1, HEAD_DIM))


def _qkv_cast_kernel(*refs):
    o_ref = refs[-1]
    for slot, ref in enumerate(refs[:-1]):
        o_ref[:, slot * HEAD_DIM:(slot + 1) * HEAD_DIM] = ref[...].astype(BF16)


def _pair_major_qkv(w_all, layer):
    n_heads = N_PATTERNS * HEADS
    part = lambda hl, p: pl.BlockSpec((None, D_MODEL, HEAD_DIM),
                                      lambda j: (layer, 0, p * n_heads + HEAD_PAIR * j + hl))
    specs = [part(hl, p) for hl in range(HEAD_PAIR) for p in range(3)]
    return pl.pallas_call(
        _qkv_cast_kernel,
        grid=(n_heads // HEAD_PAIR,),
        in_specs=specs,
        out_specs=pl.BlockSpec((D_MODEL, HEAD_PAIR * 3 * HEAD_DIM), lambda j: (0, j)),
        out_shape=jax.ShapeDtypeStruct((D_MODEL, 3 * ATTN_QKV), BF16),
        compiler_params=_params(1),
        name="qkv_weights",
    )(*[w_all] * len(specs))


def kernel(x, c, positions, mod_w, mod_b, norm_w, even_w_in, conv_dw_w, conv_dw_b, conv_ln_w, conv_ln_b, ssm_lam_re, ssm_lam_im, ssm_log_dt, ssm_b_re, ssm_b_im, ssm_c_re, ssm_c_im, ssm_d, ssm_glu_w, ssm_glu_b, even_w_out, attn_w_in, attn_q_norm_w, attn_k_norm_w, attn_w_out):
    bsz, seq, _ = x.shape
    assert seq == SEQ_LEN and x.shape[-1] == D_MODEL and bsz % SSM_BATCH == 0
    mod = _modulation(c, mod_w, mod_b)
    shift = [mod[l, :, None, :D_MODEL] for l in range(DEPTH)]
    scale = [mod[l, :, None, D_MODEL:2 * D_MODEL] for l in range(DEPTH)]
    gate = [mod[l, :, None, 2 * D_MODEL:] for l in range(DEPTH)]
    nw = [norm_w[l].reshape(1, D_MODEL) for l in range(DEPTH)]
    assert DEPTH % 2 == 0
    tables = _rope_tables(positions)
    ssm_mats = jax.vmap(_ssm_matrices)(ssm_lam_re, ssm_lam_im, ssm_log_dt, ssm_b_re, ssm_b_im, ssm_c_re, ssm_c_im)

    hs = None
    for layer in range(DEPTH):
        i = layer // 2
        nxt = (nw[layer + 1], scale[layer + 1], shift[layer + 1]) if layer + 1 < DEPTH else None
        if layer % 2 == 0:
            src, norm = (x, (nw[0], scale[0], shift[0])) if layer == 0 else (hs[0], None)
            ya, u, bz = _even_in(src, even_w_in[i].astype(BF16), conv_dw_w[i], conv_dw_b[i],
                                 conv_ln_w[i], conv_ln_b[i], norm)
            ys = _ssm(u, ssm_mats, i, ssm_d[i])
            w_out = even_w_out[i].astype(BF16)
            x, hs = _tail_call(_even_out_kernel, "even_out", x, [ya, ys, bz],
                               [ssm_glu_w[i].astype(BF16), ssm_glu_b[i].reshape(1, SSM_WIDTH),
                                w_out[:CONV_WIDTH], w_out[CONV_WIDTH:]], gate[layer], nxt, permuted=True)
        else:
            o = _attention(hs, _pair_major_qkv(attn_w_in, i), tables, attn_q_norm_w[i], attn_k_norm_w[i])
            width = HEADS * HEAD_DIM
            gate_cols = pl.BlockSpec((None, D_MODEL, width), lambda b, t, i=i: (i, 0, 3 * ATTN_QKV // width))
            x, hs = _tail_call(_odd_out_kernel, "odd_out", x, [hs[0], o],
                               [(attn_w_in, gate_cols), attn_w_out[i].astype(BF16)],
                               gate[layer], nxt, permuted=False)
    return x
```

```python
import functools
import math

import jax
import jax.numpy as jnp
from jax import lax
from jax.experimental import pallas as pl
from jax.experimental.pallas import tpu as pltpu

F32 = jnp.float32
BF16 = jnp.bfloat16

D_MODEL = 1024
DEPTH = 4
CONV_WIDTH = 512
CONV_KERNEL = 31
SSM_WIDTH = 512
SSM_GROUP = 16
SSM_STATE = 64
HEAD_DIM = 128
SEQ_LEN = 2048
ATTN_PATTERNS = ((128, 1), (512, 4), (SEQ_LEN, 16))
N_PATTERNS = 3
HEADS = 8
ATTN_QKV = N_PATTERNS * HEADS * HEAD_DIM
ATTN_SCALE = HEAD_DIM ** -0.5
ROPE_THETA = 10000.0
EPS = 1e-6
NEG_INF = -1e30

LANES = 128
WINDOW_KEYS = 128
SSM_CHUNK = 16
SSM_LANE_GROUPS = LANES // SSM_GROUP
SSM_BLOCKS = SSM_WIDTH // LANES
SSM_STATE_COLS = SSM_LANE_GROUPS * SSM_STATE
SSM_BATCH = 2
ROW_TILE = 1024
CONV_HALO = 32
CONV_ROWS = 64
PROJ_ROWS = 256
VMEM_LIMIT = 56 * 1024 * 1024
ATTN_VMEM_LIMIT = 58 * 1024 * 1024
HEAD_PAIR = 2


def _params(n_axes, vmem=VMEM_LIMIT):
    return pltpu.CompilerParams(dimension_semantics=("arbitrary",) * n_axes, vmem_limit_bytes=vmem)


def _adaln(x, nw, scale, shift):
    ms = jnp.mean(x * x, axis=-1, keepdims=True)
    return (x * lax.rsqrt(ms + EPS) * nw) * (1.0 + scale) + shift


def _dot(a, b):
    return jnp.dot(a, b, preferred_element_type=F32)


def _mod_kernel(c_ref, w_ref, b_ref, o_ref):
    o_ref[0] = _dot(c_ref[...].astype(BF16), w_ref[0].astype(BF16)) + b_ref[0]


def _modulation(c, mod_w, mod_b):
    bsz = c.shape[0]
    nblk = 3
    return pl.pallas_call(
        _mod_kernel,
        grid=(DEPTH, nblk),
        in_specs=[pl.BlockSpec((bsz, D_MODEL), lambda l, j: (0, 0)),
                  pl.BlockSpec((1, D_MODEL, D_MODEL), lambda l, j: (l, 0, j)),
                  pl.BlockSpec((1, 1, D_MODEL), lambda l, j: (l, 0, j))],
        out_specs=pl.BlockSpec((1, bsz, D_MODEL), lambda l, j: (l, 0, j)),
        out_shape=jax.ShapeDtypeStruct((DEPTH, bsz, 3 * D_MODEL), F32),
        compiler_params=_params(2),
        name="modulation",
    )(c, mod_w, mod_b.reshape(DEPTH, 1, 3 * D_MODEL))


def _rope_kernel(pos_ref, inv_ref, tab_ref, tab1_ref, tab2_ref, stage_scr, stage2_scr):
    half_rows, half_lanes = ROW_TILE // 2, HEAD_DIM // 2
    pos = pos_ref[0].astype(F32)
    lower = lax.broadcasted_iota(jnp.int32, (half_rows, HEAD_DIM), 1) < half_lanes
    ang = jnp.where(lower, pos[:half_rows], pos[half_rows:]) * inv_ref[...]
    cos, sin = jnp.cos(ang), jnp.sin(ang)
    tab = jnp.concatenate([jnp.where(lower, cos, pltpu.roll(sin, half_lanes, 1)),
                           jnp.where(lower, pltpu.roll(cos, half_lanes, 1), sin)], axis=0)
    tab_ref[0] = tab
    d1, d2 = ATTN_PATTERNS[1][1], ATTN_PATTERNS[2][1]
    assert d2 == d1 * d1
    n1, n2 = ROW_TILE // d1, ROW_TILE // d2
    stage_scr[...] = tab
    for r in range(d1):
        part = stage_scr[pl.ds(r, n1, stride=d1), :]
        tab1_ref[0, r] = part
        stage2_scr[r] = part
    for r in range(d1):
        for a in range(d1):
            tab2_ref[0, a * d1 + r] = stage2_scr[r, pl.ds(a, n2, stride=d1), :]


def _rope_tables(positions):
    bsz, seq = positions.shape
    inv = ROPE_THETA ** (-jnp.arange(0, HEAD_DIM, 2, dtype=F32) / HEAD_DIM)
    inv2 = jnp.concatenate([inv, inv]).reshape(1, HEAD_DIM)
    d1, d2 = ATTN_PATTERNS[1][1], ATTN_PATTERNS[2][1]
    perm = lambda d: (pl.BlockSpec((1, d, ROW_TILE // d, HEAD_DIM), lambda b, i: (b, 0, i, 0)),
                      jax.ShapeDtypeStruct((bsz, d, seq // d, HEAD_DIM), F32))
    specs, shapes = zip((pl.BlockSpec((1, ROW_TILE, HEAD_DIM), lambda b, i: (b, i, 0)),
                         jax.ShapeDtypeStruct((bsz, seq, HEAD_DIM), F32)), perm(d1), perm(d2))
    tabs = pl.pallas_call(
        _rope_kernel,
        grid=(bsz, seq // ROW_TILE),
        in_specs=[pl.BlockSpec((1, ROW_TILE, 1), lambda b, i: (b, i, 0)),
                  pl.BlockSpec((1, HEAD_DIM), lambda b, i: (0, 0))],
        out_specs=list(specs),
        out_shape=list(shapes),
        scratch_shapes=[pltpu.VMEM((ROW_TILE, HEAD_DIM), F32), pltpu.VMEM((d1, ROW_TILE // d1, HEAD_DIM), F32)],
        compiler_params=_params(2),
        name="rope_tables",
    )(positions.reshape(bsz, seq, 1), inv2)
    return [t.reshape(bsz, seq, HEAD_DIM) for t in tabs]


def _even_in_kernel(*refs, from_x):
    if from_x:
        x_ref, nw_ref, sc_ref, sh_ref = refs[:4]
        h = _adaln(x_ref[0], nw_ref[...], sc_ref[0], sh_ref[0]).astype(BF16)
        refs = refs[4:]
    else:
        h = refs[0][0]
        refs = refs[1:]
    w_ref, dww_ref, dwb_ref, lnw_ref, lnb_ref, ya_ref, u_ref, bz_ref, conv_scr, y_scr, az_scr = refs
    i = pl.program_id(1)
    cw = CONV_WIDTH
    slabs = cw // LANES

    @pl.when(i == 0)
    def _():
        conv_scr[:, 0:CONV_HALO, :] = jnp.zeros((slabs, CONV_HALO, LANES), F32)

    a1 = _dot(h, w_ref[:, 0:cw])
    a2 = _dot(h, w_ref[:, cw:2 * cw])
    a = a1 * jax.nn.sigmoid(a2)
    for l in range(slabs):
        conv_scr[l, CONV_HALO:CONV_HALO + ROW_TILE, :] = a[:, l * LANES:(l + 1) * LANES]
    az_scr[...] = jax.nn.silu(_dot(h, w_ref[:, 2 * cw:3 * cw]))
    u_ref[0] = _dot(h, w_ref[:, 3 * cw:4 * cw])
    bz_ref[0] = _dot(h, w_ref[:, 4 * cw:5 * cw])

    first_tap = CONV_HALO - (CONV_KERNEL - 1)
    half = CONV_ROWS // 2

    def conv_block(r, carry):
        r0 = r * CONV_ROWS
        for l in range(slabs):
            lanes = slice(l * LANES, (l + 1) * LANES)
            for par in range(2):
                acc = jnp.broadcast_to(dwb_ref[:, lanes], (half, LANES))
                for k in range(CONV_KERNEL):
                    tap = conv_scr[l, pl.ds(r0 + first_tap + par + k, half, stride=2), :]
                    acc = acc + dww_ref[k:k + 1, lanes] * tap
                y_scr[l, pl.ds(r0 + par, half, stride=2), :] = acc
        return carry

    lax.fori_loop(0, ROW_TILE // CONV_ROWS, conv_block, 0)
    for l in range(slabs):
        conv_scr[l, 0:CONV_HALO, :] = conv_scr[l, ROW_TILE:ROW_TILE + CONV_HALO, :]

    def norm_block(r, carry):
        rows = pl.ds(pl.multiple_of(r * CONV_ROWS, CONV_ROWS), CONV_ROWS)
        acc = jnp.concatenate([y_scr[l, rows, :] for l in range(slabs)], axis=-1)
        mu = jnp.mean(acc, axis=-1, keepdims=True)
        xc = acc - mu
        y = xc * lax.rsqrt(jnp.mean(xc * xc, axis=-1, keepdims=True) + EPS)
        y = y * lnw_ref[...] + lnb_ref[...]
        ya_ref[0, rows, :] = (jax.nn.silu(y) * az_scr[rows, :]).astype(BF16)
        return carry

    lax.fori_loop(0, ROW_TILE // CONV_ROWS, norm_block, 0, unroll=4)


def _even_in(h, w_in, dw_w, dw_b, ln_w, ln_b, norm=None):
    bsz, seq, _ = h.shape
    cw = CONV_WIDTH
    const = lambda shape: pl.BlockSpec(shape, lambda b, i: (0,) * len(shape))
    row = lambda width: pl.BlockSpec((1, ROW_TILE, width), lambda b, i: (b, i, 0))
    vec = pl.BlockSpec((1, 1, D_MODEL), lambda b, i: (b, 0, 0))
    norm_args = [] if norm is None else list(norm)
    norm_specs = [] if norm is None else [const((1, D_MODEL)), vec, vec]
    return pl.pallas_call(
        functools.partial(_even_in_kernel, from_x=norm is not None),
        grid=(bsz, seq // ROW_TILE),
        in_specs=[row(D_MODEL), *norm_specs, const(w_in.shape), const((CONV_KERNEL, cw)),
                  const((1, cw)), const((1, cw)), const((1, cw))],
        out_specs=[row(cw), row(cw), row(cw)],
        out_shape=[jax.ShapeDtypeStruct((bsz, seq, cw), BF16),
                   jax.ShapeDtypeStruct((bsz, seq, cw), F32),
                   jax.ShapeDtypeStruct((bsz, seq, cw), F32)],
        scratch_shapes=[pltpu.VMEM((cw // LANES, CONV_HALO + ROW_TILE, LANES), F32),
                        pltpu.VMEM((cw // LANES, ROW_TILE, LANES), F32),
                        pltpu.VMEM((ROW_TILE, cw), F32)],
        compiler_params=_params(2),
        name="even_in",
    )(h, *norm_args, w_in, dw_w, dw_b.reshape(1, cw), ln_w.reshape(1, cw), ln_b.reshape(1, cw))


def _ssm_matrices(lam_re, lam_im, log_dt, b_re, b_im, c_re, c_im):
    hp = lax.Precision.HIGHEST
    t = SSM_CHUNK
    lr, li = lam_re.astype(F32), lam_im.astype(F32)
    dt = jnp.exp(log_dt.astype(F32))[:, None]

    def a_pow(k):
        kk = k.astype(F32)[:, None, None]
        mag = jnp.exp(kk * (lr * dt)[None])
        ang = kk * (li * dt)[None]
        return mag * jnp.cos(ang), mag * jnp.sin(ang)

    ar, ai = a_pow(jnp.ones((1,), F32))
    ar, ai = ar[0], ai[0]
    den = lr * lr + li * li
    nr = ar - 1.0
    kr = (nr * lr + ai * li) / den
    ki = (ai * lr - nr * li) / den
    br, bi = b_re.astype(F32), b_im.astype(F32)
    bbr = kr[..., None] * br - ki[..., None] * bi
    bbi = kr[..., None] * bi + ki[..., None] * br
    cr, ci = c_re.astype(F32), c_im.astype(F32)

    pr, pi = a_pow(jnp.arange(t + 1))
    wr = pr[:t, :, :, None] * bbr[None] - pi[:t, :, :, None] * bbi[None]
    wi = pr[:t, :, :, None] * bbi[None] + pi[:t, :, :, None] * bbr[None]
    kk = (jnp.einsum('gop,kgpi->kgoi', cr, wr, precision=hp)
          - jnp.einsum('gop,kgpi->kgoi', ci, wi, precision=hp))

    nb, gl, hh, pp = SSM_BLOCKS, SSM_LANE_GROUPS, SSM_GROUP, SSM_STATE
    grp = jnp.arange(gl)[:, None, None]
    row = jnp.arange(t * hh)[None, :, None]
    col = jnp.arange(t * LANES)[None, None, :]
    place = ((row // hh == col // LANES) & (row % hh == col % hh) & ((col % LANES) // hh == grp)).astype(BF16)
    prow = jnp.arange(pp)[None, :, None]
    pcol = jnp.arange(gl * pp)[None, None, :]
    spread = ((pcol % pp == prow) & (pcol // pp == grp)).astype(BF16)

    def rows_sgi(m):
        n = m.shape[-1]
        return m.reshape(nb, gl, t, hh, n).transpose(0, 2, 1, 3, 4).reshape(nb, t * LANES, n)

    lrow = jnp.arange(t * hh)[None, :, None]
    lcol = jnp.arange(t * hh)[None, None, :]
    s_idx = jnp.arange(t)[:, None, None]
    shift = ((lcol // hh == lrow // hh + s_idx) & (lcol % hh == lrow % hh)).astype(BF16)
    kk_i = kk.reshape(t, nb, gl, hh, hh).transpose(1, 2, 4, 0, 3).reshape(nb, gl, hh, t * hh)
    toe = jnp.einsum('cgil,sln->cgsin', kk_i.astype(BF16), shift,
                     preferred_element_type=BF16).reshape(nb, gl, t * hh, t * hh)
    m_intra = rows_sgi(jnp.einsum('cgab,gbn->cgan', toe, place, preferred_element_type=BF16))

    def state_cols(w):
        w = w[::-1].reshape(t, nb, gl, pp, hh).transpose(1, 2, 0, 4, 3).reshape(nb, gl, t * hh, pp)
        return rows_sgi(jnp.einsum('cgap,gpn->cgan', w.astype(BF16), spread, preferred_element_type=BF16))

    m_state_r, m_state_i = state_cols(wr), state_cols(wi)

    def in_rows(q):
        q = q.reshape(t, nb, gl, hh, pp).transpose(1, 2, 4, 0, 3).reshape(nb, gl, pp, t * hh)
        return jnp.einsum('cgpb,gbn->cgpn', q.astype(BF16), place,
                          preferred_element_type=BF16).reshape(nb, gl * pp, t * LANES)

    m_in_r = in_rows(pr[1:, :, None, :] * cr[None] - pi[1:, :, None, :] * ci[None])
    m_in_i = in_rows(-(pr[1:, :, None, :] * ci[None] + pi[1:, :, None, :] * cr[None]))

    n_lvl = int(math.log2(SEQ_LEN // t))
    sr2, si2 = a_pow(t * (2 ** jnp.arange(n_lvl)))
    ap_r = sr2.reshape(n_lvl, nb, gl * pp).transpose(1, 0, 2)
    ap_i = si2.reshape(n_lvl, nb, gl * pp).transpose(1, 0, 2)
    return m_intra, m_state_r, m_state_i, m_in_r, m_in_i, ap_r, ap_i


def _ssm_kernel(u_ref, mintra_ref, msr_ref, msi_ref, minr_ref, mini_ref, apr_ref, api_ref, d_ref, y_ref,
                x_scr, yi_scr, cr_scr, ci_scr, zr_scr, zi_scr, *, n_chunks, n_levels):
    t = SSM_CHUNK
    sc = SSM_STATE_COLS
    pair = 2 * LANES
    for bb in range(SSM_BATCH):
        for t0 in range(t):
            x_scr[bb * n_chunks:(bb + 1) * n_chunks, t0 * LANES:(t0 + 1) * LANES] = (
                u_ref[bb, pl.ds(t0, n_chunks, stride=t), :].astype(BF16))
    s_re = _dot(x_scr[...], msr_ref[0])
    s_im = _dot(x_scr[...], msi_ref[0])
    for tp in range(t // 2):
        cols = slice(tp * pair, (tp + 1) * pair)
        yi_scr[:, cols] = _dot(x_scr[:, 0:(tp + 1) * pair], mintra_ref[0, 0:(tp + 1) * pair, cols])

    zero = jnp.zeros((n_chunks, sc), F32)
    for p in range(2):
        zr_scr[p, 0:n_chunks, :] = zero
        zi_scr[p, 0:n_chunks, :] = zero
    for bb in range(SSM_BATCH):
        rows = slice(bb * n_chunks, (bb + 1) * n_chunks)
        zr_scr[0, n_chunks:, :] = s_re[rows]
        zi_scr[0, n_chunks:, :] = s_im[rows]
        for k in range(n_levels):
            src, dst = k % 2, 1 - (k % 2)
            sh = n_chunks - (1 << k)
            zr = zr_scr[src, n_chunks:, :]
            zi = zi_scr[src, n_chunks:, :]
            pr = zr_scr[src, sh:sh + n_chunks, :]
            pi = zi_scr[src, sh:sh + n_chunks, :]
            ar = apr_ref[0, k:k + 1, :]
            ai = api_ref[0, k:k + 1, :]
            zr_scr[dst, n_chunks:, :] = zr + ar * pr - ai * pi
            zi_scr[dst, n_chunks:, :] = zi + ar * pi + ai * pr
        fin = n_levels % 2
        cr_scr[rows, :] = zr_scr[fin, n_chunks - 1:2 * n_chunks - 1, :].astype(BF16)
        ci_scr[rows, :] = zi_scr[fin, n_chunks - 1:2 * n_chunks - 1, :].astype(BF16)

    carry_r = cr_scr[...]
    carry_i = ci_scr[...]
    for tp in range(t // 2):
        cols = slice(tp * pair, (tp + 1) * pair)
        yc = yi_scr[:, cols] + _dot(carry_r, minr_ref[0, :, cols]) + _dot(carry_i, mini_ref[0, :, cols])
        for bb in range(SSM_BATCH):
            for j in range(2):
                y_ref[bb, pl.ds(2 * tp + j, n_chunks, stride=t), :] = (
                    yc[bb * n_chunks:(bb + 1) * n_chunks, j * LANES:(j + 1) * LANES])
    for bb in range(SSM_BATCH):
        y_ref[bb] = y_ref[bb] + d_ref[0] * u_ref[bb]


def _ssm(u, mats, layer, d_skip):
    m_intra, m_state_r, m_state_i, m_in_r, m_in_i, ap_r, ap_i = mats
    bsz, seq, _ = u.shape
    n_chunks = seq // SSM_CHUNK
    n_levels = ap_r.shape[2]
    nb = SSM_BATCH
    once = pl.Buffered(1)
    blk = pl.BlockSpec((nb, seq, LANES), lambda c, b: (b, 0, c))
    mat = lambda m: pl.BlockSpec((None, 1) + m.shape[2:], lambda c, b: (layer, c, 0, 0), pipeline_mode=once)
    lvl = pl.BlockSpec((None, 1, n_levels, SSM_STATE_COLS), lambda c, b: (layer, c, 0, 0))
    return pl.pallas_call(
        functools.partial(_ssm_kernel, n_chunks=n_chunks, n_levels=n_levels),
        grid=(SSM_BLOCKS, bsz // nb),
        in_specs=[blk, mat(m_intra), mat(m_state_r), mat(m_state_i), mat(m_in_r), mat(m_in_i), lvl, lvl,
                  pl.BlockSpec((1, 1, LANES), lambda c, b: (c, 0, 0))],
        out_specs=blk,
        out_shape=jax.ShapeDtypeStruct(u.shape, F32),
        scratch_shapes=[pltpu.VMEM((nb * n_chunks, SSM_CHUNK * LANES), BF16),
                        pltpu.VMEM((nb * n_chunks, SSM_CHUNK * LANES), F32),
                        pltpu.VMEM((nb * n_chunks, SSM_STATE_COLS), BF16),
                        pltpu.VMEM((nb * n_chunks, SSM_STATE_COLS), BF16),
                        pltpu.VMEM((2, 2 * n_chunks, SSM_STATE_COLS), F32),
                        pltpu.VMEM((2, 2 * n_chunks, SSM_STATE_COLS), F32)],
        compiler_params=_params(2),
        name="ssm",
    )(u, m_intra, m_state_r, m_state_i, m_in_r, m_in_i, ap_r, ap_i, d_skip.reshape(SSM_BLOCKS, 1, LANES))


def _finish(x_ref, gate_ref, out, nxt, xo_ref, h_outs, stage_scr, stage2_scr):
    xn = x_ref[0] + gate_ref[0] * out
    xo_ref[0] = xn
    if nxt is None:
        return
    nw_ref, sc_ref, sh_ref = nxt
    hn = _adaln(xn, nw_ref[...], sc_ref[0], sh_ref[0])
    h_outs[0][0] = hn.astype(BF16)
    if len(h_outs) == 1:
        return
    slabs = D_MODEL // LANES
    d1, d2 = ATTN_PATTERNS[1][1], ATTN_PATTERNS[2][1]
    assert d2 == d1 * d1 and len(h_outs) == 3
    n1, n2 = ROW_TILE // d1, ROW_TILE // d2
    for l in range(slabs):
        stage_scr[l] = hn[:, l * LANES:(l + 1) * LANES]
    for r in range(d1):
        parts = [stage_scr[l, pl.ds(r, n1, stride=d1), :] for l in range(slabs)]
        h_outs[1][0, r] = jnp.concatenate(parts, axis=-1).astype(BF16)
        for l in range(slabs):
            stage2_scr[r * slabs + l] = parts[l]
    for r in range(d1):
        for a in range(d1):
            parts = [stage2_scr[r * slabs + l, pl.ds(a, n2, stride=d1), :] for l in range(slabs)]
            h_outs[2][0, a * d1 + r] = jnp.concatenate(parts, axis=-1).astype(BF16)


def _even_out_kernel(*refs, n_h):
    x_ref, ya_ref, ys_ref, bz_ref, gw_ref, gb_ref, wa_ref, ws_ref, gate_ref = refs[:9]
    nxt, xo_ref, h_outs, stages = _tail_refs(refs[9:], n_h)
    y = ys_ref[0]
    y = 0.5 * y * (1.0 + lax.erf(y * (2.0 ** -0.5)))
    y = y * jax.nn.sigmoid(_dot(y.astype(BF16), gw_ref[...]) + gb_ref[...])
    y = y * jax.nn.silu(bz_ref[0])
    out = _dot(ya_ref[0], wa_ref[...]) + _dot(y.astype(BF16), ws_ref[...])
    _finish(x_ref, gate_ref, out, nxt, xo_ref, h_outs, *stages)


def _odd_out_kernel(*refs, n_h):
    x_ref, h_ref, o_ref, wz_ref, wo_ref, gate_ref = refs[:6]
    nxt, xo_ref, h_outs, stages = _tail_refs(refs[6:], n_h)
    z = _dot(h_ref[0], wz_ref[...].astype(BF16))
    g = o_ref[0] * jax.nn.silu(z)
    out = _dot(g.astype(BF16), wo_ref[...])
    _finish(x_ref, gate_ref, out, nxt, xo_ref, h_outs, *stages)


def _tail_refs(refs, n_h):
    if n_h == 0:
        return None, refs[0], (), (None, None)
    nxt, xo_ref, h_outs = refs[:3], refs[3], refs[4:4 + n_h]
    return nxt, xo_ref, h_outs, (tuple(refs[4 + n_h:6 + n_h]) if n_h > 1 else (None, None))


def _tail_call(body, name, x, rows, consts, gate, nxt, permuted):
    bsz, seq, _ = x.shape
    n_h = 0 if nxt is None else (N_PATTERNS if permuted else 1)
    row = lambda a: pl.BlockSpec((1, ROW_TILE, a.shape[-1]), lambda b, i: (b, i, 0))
    const = lambda a: pl.BlockSpec(a.shape, lambda b, i: (0,) * a.ndim)
    vec = pl.BlockSpec((1, 1, D_MODEL), lambda b, i: (b, 0, 0))
    consts = [a if isinstance(a, tuple) else (a, const(a)) for a in consts]
    args = [x, *rows, *[a for a, _ in consts], gate]
    in_specs = [row(x)] + [row(a) for a in rows] + [spec for _, spec in consts] + [vec]
    out_specs = [row(x)]
    out_shape = [jax.ShapeDtypeStruct(x.shape, F32)]
    scratch = []
    if n_h:
        nw, sc, sh = nxt
        args += [nw, sc, sh]
        in_specs += [const(nw), vec, vec]
        out_specs.append(row(x))
        out_shape.append(jax.ShapeDtypeStruct(x.shape, BF16))
    if n_h > 1:
        for _, dil in ATTN_PATTERNS[1:]:
            out_specs.append(pl.BlockSpec((1, dil, ROW_TILE // dil, D_MODEL), lambda b, i: (b, 0, i, 0)))
            out_shape.append(jax.ShapeDtypeStruct((bsz, dil, seq // dil, D_MODEL), BF16))
        d1 = ATTN_PATTERNS[1][1]
        scratch.append(pltpu.VMEM((D_MODEL // LANES, ROW_TILE, LANES), F32))
        scratch.append(pltpu.VMEM((d1 * D_MODEL // LANES, ROW_TILE // d1, LANES), F32))
    res = pl.pallas_call(
        functools.partial(body, n_h=n_h),
        grid=(bsz, seq // ROW_TILE),
        in_specs=in_specs, out_specs=out_specs, out_shape=out_shape, scratch_shapes=scratch,
        compiler_params=_params(2),
        name=name,
    )(*args)
    return res[0], [h.reshape(x.shape) for h in res[1:]]


def _attn_kernel(h0_ref, h1_ref, h2_ref, *refs, seq):
    w_refs = refs[:N_PATTERNS]
    (t0_ref, t1_ref, t2_ref, qw_ref, kw_ref, o_ref,
     q_scr, k_scr, v_scr, on_scr, lse_scr, bias_scr) = refs[N_PATTERNS:]
    wk = WINDOW_KEYS
    n_chunks = seq // PROJ_ROWS
    chunk_blocks = PROJ_ROWS // wk
    h_refs = (h0_ref, h1_ref, h2_ref)
    tabs = (t0_ref, t1_ref, t2_ref)

    qi = lax.broadcasted_iota(jnp.int32, (wk, 2 * wk), 0)
    kj = lax.broadcasted_iota(jnp.int32, (wk, 2 * wk), 1)
    band = (kj >= qi) & (kj <= qi + wk)
    bias_scr[0] = jnp.where(band, 0.0, NEG_INF)
    bias_scr[1] = jnp.where(band & (kj >= wk), 0.0, NEG_INF)

    def chunk_rows(c):
        start = c * PROJ_ROWS
        if not isinstance(c, int):
            start = pl.multiple_of(start, PROJ_ROWS)
        return pl.ds(start, PROJ_ROWS)

    def proj_dots(c):
        rows = chunk_rows(c)
        return [_dot(h_refs[g][0, rows, :], w_refs[g][...]) for g in range(N_PATTERNS)]

    def proj_store(c, prs):
        rows = chunk_rows(c)
        for g, pr in enumerate(prs):
            tab = tabs[g][0, rows, :]
            swapped = pltpu.roll(tab, HEAD_DIM // 2, 1)
            lower = lax.broadcasted_iota(jnp.int32, tab.shape, 1) < HEAD_DIM // 2
            cos = jnp.where(lower, tab, swapped)
            sin = jnp.where(lower, -swapped, tab)

            def norm_rope(t, w):
                t = t * lax.rsqrt(jnp.mean(t * t, axis=-1, keepdims=True) + EPS) * w
                return t * cos + pltpu.roll(t, HEAD_DIM // 2, 1) * sin

            for hl in range(HEAD_PAIR):
                off = hl * 3 * HEAD_DIM
                slot = g * HEAD_PAIR + hl
                q_scr[slot, rows, :] = (norm_rope(pr[:, off:off + HEAD_DIM], qw_ref[...]) * ATTN_SCALE).astype(BF16)
                k_scr[slot, rows, :] = norm_rope(pr[:, off + HEAD_DIM:off + 2 * HEAD_DIM], kw_ref[...]).astype(BF16)
                v_scr[slot, rows, :] = pr[:, off + 2 * HEAD_DIM:off + 3 * HEAD_DIM].astype(BF16)

    def block_ids(c):
        return [(g, hl, chunk_blocks * c + u, u) for g in range(N_PATTERNS) for hl in range(HEAD_PAIR)
                for u in range(chunk_blocks)]

    def is_first(g, j, u):
        n_blk = seq // ATTN_PATTERNS[g][1] // wk
        if n_blk == 1:
            return True
        if u % 2 == 1:
            return False
        if isinstance(j, int):
            return j % n_blk == 0
        return (j % n_blk) == 0

    def block_start(j):
        start = j * wk
        return start if isinstance(start, int) else pl.multiple_of(start, wk)

    def scores(g, hl, j, u):
        first = is_first(g, j, u)
        slot = g * HEAD_PAIR + hl
        q = q_scr[slot, pl.ds(block_start(j), wk), :]
        if first is True:
            kk = k_scr[slot, pl.ds(block_start(j), wk), :]
            bias = bias_scr[1, :, wk:]
        else:
            kk = k_scr[slot, pl.ds(block_start(j - 1), 2 * wk), :]
            bias = bias_scr[0] if first is False else bias_scr[jnp.where(first, 1, 0)]
        return lax.dot_general(q, kk, (((1,), (1,)), ((), ())), preferred_element_type=F32) + bias

    def softmax(s):
        m = jnp.max(s, axis=-1, keepdims=True)
        return jnp.exp(s - m).astype(BF16), m

    def weighted(g, hl, j, u, p):
        slot = g * HEAD_PAIR + hl
        if is_first(g, j, u) is True:
            vv = v_scr[slot, pl.ds(block_start(j), wk), :]
        else:
            vv = v_scr[slot, pl.ds(block_start(j - 1), 2 * wk), :]
        ov = _dot(p, jnp.concatenate([vv, jnp.ones_like(vv)], axis=1))
        return ov[:, :HEAD_DIM], ov[:, HEAD_DIM:]

    def attn_store(g, hl, j, o, m, den):
        dil = ATTN_PATTERNS[g][1]
        n_blk = seq // dil // wk
        slot = g * HEAD_PAIR + hl
        if dil == 1:
            nat = pl.ds(block_start(j), wk)
        else:
            nat = pl.ds((j % n_blk) * (wk * dil) + j // n_blk, wk, stride=dil)
        on_scr[slot, nat, :] = o * (1.0 / den)
        lse_scr[slot, nat, :] = m + jnp.log(den)

    def step(c_attn, c_proj):
        ids = block_ids(c_attn) if c_attn is not None else []
        ss = [scores(g, hl, j, u) for g, hl, j, u in ids]
        prs = proj_dots(c_proj) if c_proj is not None else None
        sm = [softmax(s) for s in ss]
        os_ = [weighted(g, hl, j, u, p) for (g, hl, j, u), (p, _) in zip(ids, sm)]
        if prs is not None:
            proj_store(c_proj, prs)
        for (g, hl, j, u), (o, den), (_, m) in zip(ids, os_, sm):
            attn_store(g, hl, j, o, m, den)

    step(None, 0)
    step(0, 1)

    def body(c, carry):
        step(c - 1, c)
        return carry

    lax.fori_loop(2, n_chunks, body, 0)
    step(n_chunks - 1, None)

    def merge_body(c, carry):
        rows = pl.ds(pl.multiple_of(c * PROJ_ROWS, PROJ_ROWS), PROJ_ROWS)
        for hl in range(HEAD_PAIR):
            ls = [lse_scr[g * HEAD_PAIR + hl, rows, :] for g in range(N_PATTERNS)]
            mx = jnp.maximum(jnp.maximum(ls[0], ls[1]), ls[2])
            num = jnp.zeros((PROJ_ROWS, HEAD_DIM), F32)
            den = jnp.zeros((PROJ_ROWS, HEAD_DIM), F32)
            for g in range(N_PATTERNS):
                wgt = jnp.exp(ls[g] - mx)
                num = num + wgt * on_scr[g * HEAD_PAIR + hl, rows, :]
                den = den + wgt
            o_ref[0, rows, hl * HEAD_DIM:(hl + 1) * HEAD_DIM] = (num / den).astype(BF16)
        return carry

    lax.fori_loop(0, seq // PROJ_ROWS, merge_body, 0)


def _attention(hs, w_qkv, tables, q_norm_w, k_norm_w):
    bsz, seq, _ = hs[0].shape
    pair_cols = HEAD_PAIR * 3 * HEAD_DIM
    wspec = lambda g: pl.BlockSpec((D_MODEL, pair_cols), lambda b, hp: (0, g * (HEADS // HEAD_PAIR) + hp))
    slots = N_PATTERNS * HEAD_PAIR
    stat = pltpu.VMEM((slots, seq, HEAD_DIM), F32)
    qkv = pltpu.VMEM((slots, seq, HEAD_DIM), BF16)
    once = pl.Buffered(1)
    hspec = pl.BlockSpec((1, seq, D_MODEL), lambda b, hp: (b, 0, 0), pipeline_mode=once)
    tspec = pl.BlockSpec((1, seq, HEAD_DIM), lambda b, hp: (b, 0, 0), pipeline_mode=once)
    vec = pl.BlockSpec((1, HEAD_DIM), lambda b, hp: (0, 0))
    return pl.pallas_call(
        functools.partial(_attn_kernel, seq=seq),
        grid=(bsz, HEADS // HEAD_PAIR),
        in_specs=[hspec, hspec, hspec, *[wspec(g) for g in range(N_PATTERNS)],
                  tspec, tspec, tspec, vec, vec],
        out_specs=pl.BlockSpec((1, seq, HEAD_PAIR * HEAD_DIM), lambda b, hp: (b, 0, hp)),
        out_shape=jax.ShapeDtypeStruct((bsz, seq, HEADS * HEAD_DIM), BF16),
        scratch_shapes=[qkv, qkv, qkv, stat, stat,
                        pltpu.VMEM((2, WINDOW_KEYS, 2 * WINDOW_KEYS), F32)],
        compiler_params=_params(2, ATTN_VMEM_LIMIT),
        name="attention",
    )(*hs, *[w_qkv] * N_PATTERNS, *tables, q_norm_w.reshape(1, HEAD_DIM), k_norm_w.reshape(1, HEAD_DIM))


def _qkv_cast_kernel(*refs):
    o_ref = refs[-1]
    for slot, ref in enumerate(refs[:-1]):
        o_ref[:, slot * HEAD_DIM:(slot + 1) * HEAD_DIM] = ref[...].astype(BF16)


def _pair_major_qkv(w_all, layer):
    n_heads = N_PATTERNS * HEADS
    part = lambda hl, p: pl.BlockSpec((None, D_MODEL, HEAD_DIM),
                                      lambda j: (layer, 0, p * n_heads + HEAD_PAIR * j + hl))
    specs = [part(hl, p) for hl in range(HEAD_PAIR) for p in range(3)]
    return pl.pallas_call(
        _qkv_cast_kernel,
        grid=(n_heads // HEAD_PAIR,),
        in_specs=specs,
        out_specs=pl.BlockSpec((D_MODEL, HEAD_PAIR * 3 * HEAD_DIM), lambda j: (0, j)),
        out_shape=jax.ShapeDtypeStruct((D_MODEL, 3 * ATTN_QKV), BF16),
        compiler_params=_params(1),
        name="qkv_weights",
    )(*[w_all] * len(specs))


def kernel(x, c, positions, mod_w, mod_b, norm_w, even_w_in, conv_dw_w, conv_dw_b, conv_ln_w, conv_ln_b, ssm_lam_re, ssm_lam_im, ssm_log_dt, ssm_b_re, ssm_b_im, ssm_c_re, ssm_c_im, ssm_d, ssm_glu_w, ssm_glu_b, even_w_out, attn_w_in, attn_q_norm_w, attn_k_norm_w, attn_w_out):
    bsz, seq, _ = x.shape
    assert seq == SEQ_LEN and x.shape[-1] == D_MODEL and bsz % SSM_BATCH == 0
    mod = _modulation(c, mod_w, mod_b)
    shift = [mod[l, :, None, :D_MODEL] for l in range(DEPTH)]
    scale = [mod[l, :, None, D_MODEL:2 * D_MODEL] for l in range(DEPTH)]
    gate = [mod[l, :, None, 2 * D_MODEL:] for l in range(DEPTH)]
    nw = [norm_w[l].reshape(1, D_MODEL) for l in range(DEPTH)]
    assert DEPTH % 2 == 0
    tables = _rope_tables(positions)
    ssm_mats = jax.vmap(_ssm_matrices)(ssm_lam_re, ssm_lam_im, ssm_log_dt, ssm_b_re, ssm_b_im, ssm_c_re, ssm_c_im)

    hs = None
    for layer in range(DEPTH):
        i = layer // 2
        nxt = (nw[layer + 1], scale[layer + 1], shift[layer + 1]) if layer + 1 < DEPTH else None
        if layer % 2 == 0:
            src, norm = (x, (nw[0], scale[0], shift[0])) if layer == 0 else (hs[0], None)
            ya, u, bz = _even_in(src, even_w_in[i].astype(BF16), conv_dw_w[i], conv_dw_b[i],
                                 conv_ln_w[i], conv_ln_b[i], norm)
            ys = _ssm(u, ssm_mats, i, ssm_d[i])
            w_out = even_w_out[i].astype(BF16)
            x, hs = _tail_call(_even_out_kernel, "even_out", x, [ya, ys, bz],
                               [ssm_glu_w[i].astype(BF16), ssm_glu_b[i].reshape(1, SSM_WIDTH),
                                w_out[:CONV_WIDTH], w_out[CONV_WIDTH:]], gate[layer], nxt, permuted=True)
        else:
            o = _attention(hs, _pair_major_qkv(attn_w_in, i), tables, attn_q_norm_w[i], attn_k_norm_w[i])
            width = HEADS * HEAD_DIM
            gate_cols = pl.BlockSpec((None, D_MODEL, width), lambda b, t, i=i: (i, 0, 3 * ATTN_QKV // width))
            x, hs = _tail_call(_odd_out_kernel, "odd_out", x, [hs[0], o],
                               [(attn_w_in, gate_cols), attn_w_out[i].astype(BF16)],
                               gate[layer], nxt, permuted=False)
    return x
```

```python
import functools
import math

import jax
import jax.numpy as jnp
from jax import lax
from jax.experimental import pallas as pl
from jax.experimental.pallas import tpu as pltpu

F32 = jnp.float32
BF16 = jnp.bfloat16

D_MODEL = 1024
DEPTH = 4
CONV_WIDTH = 512
CONV_KERNEL = 31
SSM_WIDTH = 512
SSM_GROUP = 16
SSM_STATE = 64
HEAD_DIM = 128
SEQ_LEN = 2048
ATTN_PATTERNS = ((128, 1), (512, 4), (SEQ_LEN, 16))
N_PATTERNS = 3
HEADS = 8
ATTN_QKV = N_PATTERNS * HEADS * HEAD_DIM
ATTN_SCALE = HEAD_DIM ** -0.5
ROPE_THETA = 10000.0
EPS = 1e-6
NEG_INF = -1e30

LANES = 128
WINDOW_KEYS = 128
SSM_CHUNK = 16
SSM_LANE_GROUPS = LANES // SSM_GROUP
SSM_BLOCKS = SSM_WIDTH // LANES
SSM_STATE_COLS = SSM_LANE_GROUPS * SSM_STATE
SSM_BATCH = 2
ROW_TILE = 1024
CONV_HALO = 32
CONV_ROWS = 64
PROJ_ROWS = 256
VMEM_LIMIT = 56 * 1024 * 1024
ATTN_VMEM_LIMIT = 58 * 1024 * 1024
HEAD_PAIR = 2
COPY_HEADS = 4


def _params(n_axes, vmem=VMEM_LIMIT):
    return pltpu.CompilerParams(dimension_semantics=("arbitrary",) * n_axes, vmem_limit_bytes=vmem)


def _adaln(x, nw, scale, shift):
    ms = jnp.mean(x * x, axis=-1, keepdims=True)
    return (x * lax.rsqrt(ms + EPS) * nw) * (1.0 + scale) + shift


def _dot(a, b):
    return jnp.dot(a, b, preferred_element_type=F32)


def _mod_kernel(c_ref, w_ref, b_ref, o_ref):
    o_ref[0] = _dot(c_ref[...].astype(BF16), w_ref[0].astype(BF16)) + b_ref[0]


def _modulation(c, mod_w, mod_b):
    bsz = c.shape[0]
    nblk = 3
    return pl.pallas_call(
        _mod_kernel,
        grid=(DEPTH, nblk),
        in_specs=[pl.BlockSpec((bsz, D_MODEL), lambda l, j: (0, 0)),
                  pl.BlockSpec((1, D_MODEL, D_MODEL), lambda l, j: (l, 0, j)),
                  pl.BlockSpec((1, 1, D_MODEL), lambda l, j: (l, 0, j))],
        out_specs=pl.BlockSpec((1, bsz, D_MODEL), lambda l, j: (l, 0, j)),
        out_shape=jax.ShapeDtypeStruct((DEPTH, bsz, 3 * D_MODEL), F32),
        compiler_params=_params(2),
        name="modulation",
    )(c, mod_w, mod_b.reshape(DEPTH, 1, 3 * D_MODEL))


def _rope_kernel(pos_ref, inv_ref, tab_ref, tab1_ref, tab2_ref, stage_scr, stage2_scr):
    half_rows, half_lanes = ROW_TILE // 2, HEAD_DIM // 2
    pos = pos_ref[0].astype(F32)
    lower = lax.broadcasted_iota(jnp.int32, (half_rows, HEAD_DIM), 1) < half_lanes
    ang = jnp.where(lower, pos[:half_rows], pos[half_rows:]) * inv_ref[...]
    cos, sin = jnp.cos(ang), jnp.sin(ang)
    tab = jnp.concatenate([jnp.where(lower, cos, pltpu.roll(sin, half_lanes, 1)),
                           jnp.where(lower, pltpu.roll(cos, half_lanes, 1), sin)], axis=0)
    tab_ref[0] = tab
    d1, d2 = ATTN_PATTERNS[1][1], ATTN_PATTERNS[2][1]
    assert d2 == d1 * d1
    n1, n2 = ROW_TILE // d1, ROW_TILE // d2
    stage_scr[...] = tab
    for r in range(d1):
        part = stage_scr[pl.ds(r, n1, stride=d1), :]
        tab1_ref[0, r] = part
        stage2_scr[r] = part
    for r in range(d1):
        for a in range(d1):
            tab2_ref[0, a * d1 + r] = stage2_scr[r, pl.ds(a, n2, stride=d1), :]


def _rope_tables(positions):
    bsz, seq = positions.shape
    inv = ROPE_THETA ** (-jnp.arange(0, HEAD_DIM, 2, dtype=F32) / HEAD_DIM)
    inv2 = jnp.concatenate([inv, inv]).reshape(1, HEAD_DIM)
    d1, d2 = ATTN_PATTERNS[1][1], ATTN_PATTERNS[2][1]
    perm = lambda d: (pl.BlockSpec((1, d, ROW_TILE // d, HEAD_DIM), lambda b, i: (b, 0, i, 0)),
                      jax.ShapeDtypeStruct((bsz, d, seq // d, HEAD_DIM), F32))
    specs, shapes = zip((pl.BlockSpec((1, ROW_TILE, HEAD_DIM), lambda b, i: (b, i, 0)),
                         jax.ShapeDtypeStruct((bsz, seq, HEAD_DIM), F32)), perm(d1), perm(d2))
    tabs = pl.pallas_call(
        _rope_kernel,
        grid=(bsz, seq // ROW_TILE),
        in_specs=[pl.BlockSpec((1, ROW_TILE, 1), lambda b, i: (b, i, 0)),
                  pl.BlockSpec((1, HEAD_DIM), lambda b, i: (0, 0))],
        out_specs=list(specs),
        out_shape=list(shapes),
        scratch_shapes=[pltpu.VMEM((ROW_TILE, HEAD_DIM), F32), pltpu.VMEM((d1, ROW_TILE // d1, HEAD_DIM), F32)],
        compiler_params=_params(2),
        name="rope_tables",
    )(positions.reshape(bsz, seq, 1), inv2)
    return [t.reshape(bsz, seq, HEAD_DIM) for t in tabs]


def _even_in_kernel(*refs, from_x):
    if from_x:
        x_ref, nw_ref, sc_ref, sh_ref = refs[:4]
        h = _adaln(x_ref[0], nw_ref[...], sc_ref[0], sh_ref[0]).astype(BF16)
        refs = refs[4:]
    else:
        h = refs[0][0]
        refs = refs[1:]
    w_ref, dww_ref, dwb_ref, lnw_ref, lnb_ref, ya_ref, u_ref, bz_ref, conv_scr, y_scr, az_scr = refs
    i = pl.program_id(1)
    cw = CONV_WIDTH
    slabs = cw // LANES

    @pl.when(i == 0)
    def _():
        conv_scr[:, 0:CONV_HALO, :] = jnp.zeros((slabs, CONV_HALO, LANES), F32)

    a1 = _dot(h, w_ref[:, 0:cw])
    a2 = _dot(h, w_ref[:, cw:2 * cw])
    a = a1 * jax.nn.sigmoid(a2)
    for l in range(slabs):
        conv_scr[l, CONV_HALO:CONV_HALO + ROW_TILE, :] = a[:, l * LANES:(l + 1) * LANES]
    az_scr[...] = jax.nn.silu(_dot(h, w_ref[:, 2 * cw:3 * cw]))
    u_ref[0] = _dot(h, w_ref[:, 3 * cw:4 * cw])
    bz_ref[0] = _dot(h, w_ref[:, 4 * cw:5 * cw])

    first_tap = CONV_HALO - (CONV_KERNEL - 1)
    half = CONV_ROWS // 2

    def conv_block(r, carry):
        r0 = r * CONV_ROWS
        for l in range(slabs):
            lanes = slice(l * LANES, (l + 1) * LANES)
            for par in range(2):
                acc = jnp.broadcast_to(dwb_ref[:, lanes], (half, LANES))
                for k in range(CONV_KERNEL):
                    tap = conv_scr[l, pl.ds(r0 + first_tap + par + k, half, stride=2), :]
                    acc = acc + dww_ref[k:k + 1, lanes] * tap
                y_scr[l, pl.ds(r0 + par, half, stride=2), :] = acc
        return carry

    lax.fori_loop(0, ROW_TILE // CONV_ROWS, conv_block, 0)
    for l in range(slabs):
        conv_scr[l, 0:CONV_HALO, :] = conv_scr[l, ROW_TILE:ROW_TILE + CONV_HALO, :]

    def norm_block(r, carry):
        rows = pl.ds(pl.multiple_of(r * CONV_ROWS, CONV_ROWS), CONV_ROWS)
        acc = jnp.concatenate([y_scr[l, rows, :] for l in range(slabs)], axis=-1)
        mu = jnp.mean(acc, axis=-1, keepdims=True)
        xc = acc - mu
        y = xc * lax.rsqrt(jnp.mean(xc * xc, axis=-1, keepdims=True) + EPS)
        y = y * lnw_ref[...] + lnb_ref[...]
        ya_ref[0, rows, :] = (jax.nn.silu(y) * az_scr[rows, :]).astype(BF16)
        return carry

    lax.fori_loop(0, ROW_TILE // CONV_ROWS, norm_block, 0, unroll=4)


def _even_in(h, w_in, dw_w, dw_b, ln_w, ln_b, norm=None):
    bsz, seq, _ = h.shape
    cw = CONV_WIDTH
    const = lambda shape: pl.BlockSpec(shape, lambda b, i: (0,) * len(shape))
    row = lambda width: pl.BlockSpec((1, ROW_TILE, width), lambda b, i: (b, i, 0))
    vec = pl.BlockSpec((1, 1, D_MODEL), lambda b, i: (b, 0, 0))
    norm_args = [] if norm is None else list(norm)
    norm_specs = [] if norm is None else [const((1, D_MODEL)), vec, vec]
    return pl.pallas_call(
        functools.partial(_even_in_kernel, from_x=norm is not None),
        grid=(bsz, seq // ROW_TILE),
        in_specs=[row(D_MODEL), *norm_specs, const(w_in.shape), const((CONV_KERNEL, cw)),
                  const((1, cw)), const((1, cw)), const((1, cw))],
        out_specs=[row(cw), row(cw), row(cw)],
        out_shape=[jax.ShapeDtypeStruct((bsz, seq, cw), BF16),
                   jax.ShapeDtypeStruct((bsz, seq, cw), F32),
                   jax.ShapeDtypeStruct((bsz, seq, cw), F32)],
        scratch_shapes=[pltpu.VMEM((cw // LANES, CONV_HALO + ROW_TILE, LANES), F32),
                        pltpu.VMEM((cw // LANES, ROW_TILE, LANES), F32),
                        pltpu.VMEM((ROW_TILE, cw), F32)],
        compiler_params=_params(2),
        name="even_in",
    )(h, *norm_args, w_in, dw_w, dw_b.reshape(1, cw), ln_w.reshape(1, cw), ln_b.reshape(1, cw))


def _ssm_matrices(lam_re, lam_im, log_dt, b_re, b_im, c_re, c_im):
    hp = lax.Precision.HIGHEST
    t = SSM_CHUNK
    lr, li = lam_re.astype(F32), lam_im.astype(F32)
    dt = jnp.exp(log_dt.astype(F32))[:, None]

    def a_pow(k):
        kk = k.astype(F32)[:, None, None]
        mag = jnp.exp(kk * (lr * dt)[None])
        ang = kk * (li * dt)[None]
        return mag * jnp.cos(ang), mag * jnp.sin(ang)

    ar, ai = a_pow(jnp.ones((1,), F32))
    ar, ai = ar[0], ai[0]
    den = lr * lr + li * li
    nr = ar - 1.0
    kr = (nr * lr + ai * li) / den
    ki = (ai * lr - nr * li) / den
    br, bi = b_re.astype(F32), b_im.astype(F32)
    bbr = kr[..., None] * br - ki[..., None] * bi
    bbi = kr[..., None] * bi + ki[..., None] * br
    cr, ci = c_re.astype(F32), c_im.astype(F32)

    pr, pi = a_pow(jnp.arange(t + 1))
    wr = pr[:t, :, :, None] * bbr[None] - pi[:t, :, :, None] * bbi[None]
    wi = pr[:t, :, :, None] * bbi[None] + pi[:t, :, :, None] * bbr[None]
    kk = (jnp.einsum('gop,kgpi->kgoi', cr, wr, precision=hp)
          - jnp.einsum('gop,kgpi->kgoi', ci, wi, precision=hp))

    nb, gl, hh, pp = SSM_BLOCKS, SSM_LANE_GROUPS, SSM_GROUP, SSM_STATE
    grp = jnp.arange(gl)[:, None, None]
    row = jnp.arange(t * hh)[None, :, None]
    col = jnp.arange(t * LANES)[None, None, :]
    place = ((row // hh == col // LANES) & (row % hh == col % hh) & ((col % LANES) // hh == grp)).astype(BF16)
    prow = jnp.arange(pp)[None, :, None]
    pcol = jnp.arange(gl * pp)[None, None, :]
    spread = ((pcol % pp == prow) & (pcol // pp == grp)).astype(BF16)

    def rows_sgi(m):
        n = m.shape[-1]
        return m.reshape(nb, gl, t, hh, n).transpose(0, 2, 1, 3, 4).reshape(nb, t * LANES, n)

    lrow = jnp.arange(t * hh)[None, :, None]
    lcol = jnp.arange(t * hh)[None, None, :]
    s_idx = jnp.arange(t)[:, None, None]
    shift = ((lcol // hh == lrow // hh + s_idx) & (lcol % hh == lrow % hh)).astype(BF16)
    kk_i = kk.reshape(t, nb, gl, hh, hh).transpose(1, 2, 4, 0, 3).reshape(nb, gl, hh, t * hh)
    toe = jnp.einsum('cgil,sln->cgsin', kk_i.astype(BF16), shift,
                     preferred_element_type=BF16).reshape(nb, gl, t * hh, t * hh)
    m_intra = rows_sgi(jnp.einsum('cgab,gbn->cgan', toe, place, preferred_element_type=BF16))

    def state_cols(w):
        w = w[::-1].reshape(t, nb, gl, pp, hh).transpose(1, 2, 0, 4, 3).reshape(nb, gl, t * hh, pp)
        return rows_sgi(jnp.einsum('cgap,gpn->cgan', w.astype(BF16), spread, preferred_element_type=BF16))

    m_state_r, m_state_i = state_cols(wr), state_cols(wi)

    def in_rows(q):
        q = q.reshape(t, nb, gl, hh, pp).transpose(1, 2, 4, 0, 3).reshape(nb, gl, pp, t * hh)
        return jnp.einsum('cgpb,gbn->cgpn', q.astype(BF16), place,
                          preferred_element_type=BF16).reshape(nb, gl * pp, t * LANES)

    m_in_r = in_rows(pr[1:, :, None, :] * cr[None] - pi[1:, :, None, :] * ci[None])
    m_in_i = in_rows(-(pr[1:, :, None, :] * ci[None] + pi[1:, :, None, :] * cr[None]))

    n_lvl = int(math.log2(SEQ_LEN // t))
    sr2, si2 = a_pow(t * (2 ** jnp.arange(n_lvl)))
    ap_r = sr2.reshape(n_lvl, nb, gl * pp).transpose(1, 0, 2)
    ap_i = si2.reshape(n_lvl, nb, gl * pp).transpose(1, 0, 2)
    return m_intra, m_state_r, m_state_i, m_in_r, m_in_i, ap_r, ap_i


def _ssm_kernel(u_ref, mintra_ref, msr_ref, msi_ref, minr_ref, mini_ref, apr_ref, api_ref, d_ref, y_ref,
                x_scr, yi_scr, cr_scr, ci_scr, zr_scr, zi_scr, *, n_chunks, n_levels):
    t = SSM_CHUNK
    sc = SSM_STATE_COLS
    pair = 2 * LANES
    for bb in range(SSM_BATCH):
        for t0 in range(t):
            x_scr[bb * n_chunks:(bb + 1) * n_chunks, t0 * LANES:(t0 + 1) * LANES] = (
                u_ref[bb, pl.ds(t0, n_chunks, stride=t), :].astype(BF16))
    s_re = _dot(x_scr[...], msr_ref[0])
    s_im = _dot(x_scr[...], msi_ref[0])
    for tp in range(t // 2):
        cols = slice(tp * pair, (tp + 1) * pair)
        yi_scr[:, cols] = _dot(x_scr[:, 0:(tp + 1) * pair], mintra_ref[0, 0:(tp + 1) * pair, cols])

    zero = jnp.zeros((n_chunks, sc), F32)
    for p in range(2):
        zr_scr[p, 0:n_chunks, :] = zero
        zi_scr[p, 0:n_chunks, :] = zero
    for bb in range(SSM_BATCH):
        rows = slice(bb * n_chunks, (bb + 1) * n_chunks)
        zr_scr[0, n_chunks:, :] = s_re[rows]
        zi_scr[0, n_chunks:, :] = s_im[rows]
        for k in range(n_levels):
            src, dst = k % 2, 1 - (k % 2)
            sh = n_chunks - (1 << k)
            zr = zr_scr[src, n_chunks:, :]
            zi = zi_scr[src, n_chunks:, :]
            pr = zr_scr[src, sh:sh + n_chunks, :]
            pi = zi_scr[src, sh:sh + n_chunks, :]
            ar = apr_ref[0, k:k + 1, :]
            ai = api_ref[0, k:k + 1, :]
            zr_scr[dst, n_chunks:, :] = zr + ar * pr - ai * pi
            zi_scr[dst, n_chunks:, :] = zi + ar * pi + ai * pr
        fin = n_levels % 2
        cr_scr[rows, :] = zr_scr[fin, n_chunks - 1:2 * n_chunks - 1, :].astype(BF16)
        ci_scr[rows, :] = zi_scr[fin, n_chunks - 1:2 * n_chunks - 1, :].astype(BF16)

    carry_r = cr_scr[...]
    carry_i = ci_scr[...]
    for tp in range(t // 2):
        cols = slice(tp * pair, (tp + 1) * pair)
        yc = yi_scr[:, cols] + _dot(carry_r, minr_ref[0, :, cols]) + _dot(carry_i, mini_ref[0, :, cols])
        for bb in range(SSM_BATCH):
            for j in range(2):
                y_ref[bb, pl.ds(2 * tp + j, n_chunks, stride=t), :] = (
                    yc[bb * n_chunks:(bb + 1) * n_chunks, j * LANES:(j + 1) * LANES])
    for bb in range(SSM_BATCH):
        y_ref[bb] = y_ref[bb] + d_ref[0] * u_ref[bb]


def _ssm(u, mats, layer, d_skip):
    m_intra, m_state_r, m_state_i, m_in_r, m_in_i, ap_r, ap_i = mats
    bsz, seq, _ = u.shape
    n_chunks = seq // SSM_CHUNK
    n_levels = ap_r.shape[2]
    nb = SSM_BATCH
    once = pl.Buffered(1)
    blk = pl.BlockSpec((nb, seq, LANES), lambda c, b: (b, 0, c))
    mat = lambda m: pl.BlockSpec((None, 1) + m.shape[2:], lambda c, b: (layer, c, 0, 0), pipeline_mode=once)
    lvl = pl.BlockSpec((None, 1, n_levels, SSM_STATE_COLS), lambda c, b: (layer, c, 0, 0))
    return pl.pallas_call(
        functools.partial(_ssm_kernel, n_chunks=n_chunks, n_levels=n_levels),
        grid=(SSM_BLOCKS, bsz // nb),
        in_specs=[blk, mat(m_intra), mat(m_state_r), mat(m_state_i), mat(m_in_r), mat(m_in_i), lvl, lvl,
                  pl.BlockSpec((1, 1, LANES), lambda c, b: (c, 0, 0))],
        out_specs=blk,
        out_shape=jax.ShapeDtypeStruct(u.shape, F32),
        scratch_shapes=[pltpu.VMEM((nb * n_chunks, SSM_CHUNK * LANES), BF16),
                        pltpu.VMEM((nb * n_chunks, SSM_CHUNK * LANES), F32),
                        pltpu.VMEM((nb * n_chunks, SSM_STATE_COLS), BF16),
                        pltpu.VMEM((nb * n_chunks, SSM_STATE_COLS), BF16),
                        pltpu.VMEM((2, 2 * n_chunks, SSM_STATE_COLS), F32),
                        pltpu.VMEM((2, 2 * n_chunks, SSM_STATE_COLS), F32)],
        compiler_params=_params(2),
        name="ssm",
    )(u, m_intra, m_state_r, m_state_i, m_in_r, m_in_i, ap_r, ap_i, d_skip.reshape(SSM_BLOCKS, 1, LANES))


def _finish(x_ref, gate_ref, out, nxt, xo_ref, h_outs, stage_scr, stage2_scr):
    xn = x_ref[0] + gate_ref[0] * out
    xo_ref[0] = xn
    if nxt is None:
        return
    nw_ref, sc_ref, sh_ref = nxt
    hn = _adaln(xn, nw_ref[...], sc_ref[0], sh_ref[0])
    h_outs[0][0] = hn.astype(BF16)
    if len(h_outs) == 1:
        return
    slabs = D_MODEL // LANES
    d1, d2 = ATTN_PATTERNS[1][1], ATTN_PATTERNS[2][1]
    assert d2 == d1 * d1 and len(h_outs) == 3
    n1, n2 = ROW_TILE // d1, ROW_TILE // d2
    for l in range(slabs):
        stage_scr[l] = hn[:, l * LANES:(l + 1) * LANES]
    for r in range(d1):
        parts = [stage_scr[l, pl.ds(r, n1, stride=d1), :] for l in range(slabs)]
        h_outs[1][0, r] = jnp.concatenate(parts, axis=-1).astype(BF16)
        for l in range(slabs):
            stage2_scr[r * slabs + l] = parts[l]
    for r in range(d1):
        for a in range(d1):
            parts = [stage2_scr[r * slabs + l, pl.ds(a, n2, stride=d1), :] for l in range(slabs)]
            h_outs[2][0, a * d1 + r] = jnp.concatenate(parts, axis=-1).astype(BF16)


def _even_out_kernel(*refs, n_h):
    x_ref, ya_ref, ys_ref, bz_ref, gw_ref, gb_ref, wa_ref, ws_ref, gate_ref = refs[:9]
    nxt, xo_ref, h_outs, stages = _tail_refs(refs[9:], n_h)
    y = ys_ref[0]
    y = 0.5 * y * (1.0 + lax.erf(y * (2.0 ** -0.5)))
    y = y * jax.nn.sigmoid(_dot(y.astype(BF16), gw_ref[...]) + gb_ref[...])
    y = y * jax.nn.silu(bz_ref[0])
    out = _dot(ya_ref[0], wa_ref[...]) + _dot(y.astype(BF16), ws_ref[...])
    _finish(x_ref, gate_ref, out, nxt, xo_ref, h_outs, *stages)


def _odd_out_kernel(*refs, n_h):
    x_ref, h_ref, o_ref, wz_ref, wo_ref, gate_ref = refs[:6]
    nxt, xo_ref, h_outs, stages = _tail_refs(refs[6:], n_h)
    z = _dot(h_ref[0], wz_ref[...].astype(BF16))
    g = o_ref[0] * jax.nn.silu(z)
    out = _dot(g.astype(BF16), wo_ref[...])
    _finish(x_ref, gate_ref, out, nxt, xo_ref, h_outs, *stages)


def _tail_refs(refs, n_h):
    if n_h == 0:
        return None, refs[0], (), (None, None)
    nxt, xo_ref, h_outs = refs[:3], refs[3], refs[4:4 + n_h]
    return nxt, xo_ref, h_outs, (tuple(refs[4 + n_h:6 + n_h]) if n_h > 1 else (None, None))


def _tail_call(body, name, x, rows, consts, gate, nxt, permuted):
    bsz, seq, _ = x.shape
    n_h = 0 if nxt is None else (N_PATTERNS if permuted else 1)
    row = lambda a: pl.BlockSpec((1, ROW_TILE, a.shape[-1]), lambda b, i: (b, i, 0))
    const = lambda a: pl.BlockSpec(a.shape, lambda b, i: (0,) * a.ndim)
    vec = pl.BlockSpec((1, 1, D_MODEL), lambda b, i: (b, 0, 0))
    consts = [a if isinstance(a, tuple) else (a, const(a)) for a in consts]
    args = [x, *rows, *[a for a, _ in consts], gate]
    in_specs = [row(x)] + [row(a) for a in rows] + [spec for _, spec in consts] + [vec]
    out_specs = [row(x)]
    out_shape = [jax.ShapeDtypeStruct(x.shape, F32)]
    scratch = []
    if n_h:
        nw, sc, sh = nxt
        args += [nw, sc, sh]
        in_specs += [const(nw), vec, vec]
        out_specs.append(row(x))
        out_shape.append(jax.ShapeDtypeStruct(x.shape, BF16))
    if n_h > 1:
        for _, dil in ATTN_PATTERNS[1:]:
            out_specs.append(pl.BlockSpec((1, dil, ROW_TILE // dil, D_MODEL), lambda b, i: (b, 0, i, 0)))
            out_shape.append(jax.ShapeDtypeStruct((bsz, dil, seq // dil, D_MODEL), BF16))
        d1 = ATTN_PATTERNS[1][1]
        scratch.append(pltpu.VMEM((D_MODEL // LANES, ROW_TILE, LANES), F32))
        scratch.append(pltpu.VMEM((d1 * D_MODEL // LANES, ROW_TILE // d1, LANES), F32))
    res = pl.pallas_call(
        functools.partial(body, n_h=n_h),
        grid=(bsz, seq // ROW_TILE),
        in_specs=in_specs, out_specs=out_specs, out_shape=out_shape, scratch_shapes=scratch,
        compiler_params=_params(2),
        name=name,
    )(*args)
    return res[0], [h.reshape(x.shape) for h in res[1:]]


def _attn_kernel(h0_ref, h1_ref, h2_ref, *refs, seq):
    w_refs = refs[:N_PATTERNS]
    (t0_ref, t1_ref, t2_ref, qw_ref, kw_ref, o_ref,
     q_scr, k_scr, v_scr, on_scr, lse_scr, bias_scr) = refs[N_PATTERNS:]
    wk = WINDOW_KEYS
    n_chunks = seq // PROJ_ROWS
    chunk_blocks = PROJ_ROWS // wk
    h_refs = (h0_ref, h1_ref, h2_ref)
    tabs = (t0_ref, t1_ref, t2_ref)

    qi = lax.broadcasted_iota(jnp.int32, (wk, 2 * wk), 0)
    kj = lax.broadcasted_iota(jnp.int32, (wk, 2 * wk), 1)
    band = (kj >= qi) & (kj <= qi + wk)
    bias_scr[0] = jnp.where(band, 0.0, NEG_INF)
    bias_scr[1] = jnp.where(band & (kj >= wk), 0.0, NEG_INF)

    def chunk_rows(c):
        start = c * PROJ_ROWS
        if not isinstance(c, int):
            start = pl.multiple_of(start, PROJ_ROWS)
        return pl.ds(start, PROJ_ROWS)

    def proj_dots(c):
        rows = chunk_rows(c)
        return [_dot(h_refs[g][0, rows, :], w_refs[g][...]) for g in range(N_PATTERNS)]

    def proj_store(c, prs):
        rows = chunk_rows(c)
        for g, pr in enumerate(prs):
            tab = tabs[g][0, rows, :]
            swapped = pltpu.roll(tab, HEAD_DIM // 2, 1)
            lower = lax.broadcasted_iota(jnp.int32, tab.shape, 1) < HEAD_DIM // 2
            cos = jnp.where(lower, tab, swapped)
            sin = jnp.where(lower, -swapped, tab)

            def norm_rope(t, w):
                t = t * lax.rsqrt(jnp.mean(t * t, axis=-1, keepdims=True) + EPS) * w
                return t * cos + pltpu.roll(t, HEAD_DIM // 2, 1) * sin

            for hl in range(HEAD_PAIR):
                off = hl * 3 * HEAD_DIM
                slot = g * HEAD_PAIR + hl
                q_scr[slot, rows, :] = (norm_rope(pr[:, off:off + HEAD_DIM], qw_ref[...]) * ATTN_SCALE).astype(BF16)
                k_scr[slot, rows, :] = norm_rope(pr[:, off + HEAD_DIM:off + 2 * HEAD_DIM], kw_ref[...]).astype(BF16)
                v_scr[slot, rows, :] = pr[:, off + 2 * HEAD_DIM:off + 3 * HEAD_DIM].astype(BF16)

    def block_ids(c):
        return [(g, hl, chunk_blocks * c + u, u) for g in range(N_PATTERNS) for hl in range(HEAD_PAIR)
                for u in range(chunk_blocks)]

    def is_first(g, j, u):
        n_blk = seq // ATTN_PATTERNS[g][1] // wk
        if n_blk == 1:
            return True
        if u % 2 == 1:
            return False
        if isinstance(j, int):
            return j % n_blk == 0
        return (j % n_blk) == 0

    def block_start(j):
        start = j * wk
        return start if isinstance(start, int) else pl.multiple_of(start, wk)

    def scores(g, hl, j, u):
        first = is_first(g, j, u)
        slot = g * HEAD_PAIR + hl
        q = q_scr[slot, pl.ds(block_start(j), wk), :]
        if first is True:
            kk = k_scr[slot, pl.ds(block_start(j), wk), :]
            bias = bias_scr[1, :, wk:]
        else:
            kk = k_scr[slot, pl.ds(block_start(j - 1), 2 * wk), :]
            bias = bias_scr[0] if first is False else bias_scr[jnp.where(first, 1, 0)]
        return lax.dot_general(q, kk, (((1,), (1,)), ((), ())), preferred_element_type=F32) + bias

    def softmax(s):
        m = jnp.max(s, axis=-1, keepdims=True)
        return jnp.exp(s - m).astype(BF16), m

    def weighted(g, hl, j, u, p):
        slot = g * HEAD_PAIR + hl
        if is_first(g, j, u) is True:
            vv = v_scr[slot, pl.ds(block_start(j), wk), :]
        else:
            vv = v_scr[slot, pl.ds(block_start(j - 1), 2 * wk), :]
        ov = _dot(p, jnp.concatenate([vv, jnp.ones_like(vv)], axis=1))
        return ov[:, :HEAD_DIM], ov[:, HEAD_DIM:]

    def attn_store(g, hl, j, o, m, den):
        dil = ATTN_PATTERNS[g][1]
        n_blk = seq // dil // wk
        slot = g * HEAD_PAIR + hl
        if dil == 1:
            nat = pl.ds(block_start(j), wk)
        else:
            nat = pl.ds((j % n_blk) * (wk * dil) + j // n_blk, wk, stride=dil)
        on_scr[slot, nat, :] = o * (1.0 / den)
        lse_scr[slot, nat, :] = m + jnp.log(den)

    def step(c_attn, c_proj):
        ids = block_ids(c_attn) if c_attn is not None else []
        ss = [scores(g, hl, j, u) for g, hl, j, u in ids]
        prs = proj_dots(c_proj) if c_proj is not None else None
        sm = [softmax(s) for s in ss]
        os_ = [weighted(g, hl, j, u, p) for (g, hl, j, u), (p, _) in zip(ids, sm)]
        if prs is not None:
            proj_store(c_proj, prs)
        for (g, hl, j, u), (o, den), (_, m) in zip(ids, os_, sm):
            attn_store(g, hl, j, o, m, den)

    step(None, 0)
    step(0, 1)

    def body(c, carry):
        step(c - 1, c)
        return carry

    lax.fori_loop(2, n_chunks, body, 0)
    step(n_chunks - 1, None)

    def merge_body(c, carry):
        rows = pl.ds(pl.multiple_of(c * PROJ_ROWS, PROJ_ROWS), PROJ_ROWS)
        for hl in range(HEAD_PAIR):
            ls = [lse_scr[g * HEAD_PAIR + hl, rows, :] for g in range(N_PATTERNS)]
            mx = jnp.maximum(jnp.maximum(ls[0], ls[1]), ls[2])
            num = jnp.zeros((PROJ_ROWS, HEAD_DIM), F32)
            den = jnp.zeros((PROJ_ROWS, HEAD_DIM), F32)
            for g in range(N_PATTERNS):
                wgt = jnp.exp(ls[g] - mx)
                num = num + wgt * on_scr[g * HEAD_PAIR + hl, rows, :]
                den = den + wgt
            o_ref[0, rows, hl * HEAD_DIM:(hl + 1) * HEAD_DIM] = (num / den).astype(BF16)
        return carry

    lax.fori_loop(0, seq // PROJ_ROWS, merge_body, 0)


def _attention(hs, w_qkv, tables, q_norm_w, k_norm_w):
    bsz, seq, _ = hs[0].shape
    pair_cols = HEAD_PAIR * 3 * HEAD_DIM
    wspec = lambda g: pl.BlockSpec((D_MODEL, pair_cols), lambda b, hp: (0, g * (HEADS // HEAD_PAIR) + hp))
    slots = N_PATTERNS * HEAD_PAIR
    stat = pltpu.VMEM((slots, seq, HEAD_DIM), F32)
    qkv = pltpu.VMEM((slots, seq, HEAD_DIM), BF16)
    once = pl.Buffered(1)
    hspec = pl.BlockSpec((1, seq, D_MODEL), lambda b, hp: (b, 0, 0), pipeline_mode=once)
    tspec = pl.BlockSpec((1, seq, HEAD_DIM), lambda b, hp: (b, 0, 0), pipeline_mode=once)
    vec = pl.BlockSpec((1, HEAD_DIM), lambda b, hp: (0, 0))
    return pl.pallas_call(
        functools.partial(_attn_kernel, seq=seq),
        grid=(bsz, HEADS // HEAD_PAIR),
        in_specs=[hspec, hspec, hspec, *[wspec(g) for g in range(N_PATTERNS)],
                  tspec, tspec, tspec, vec, vec],
        out_specs=pl.BlockSpec((1, seq, HEAD_PAIR * HEAD_DIM), lambda b, hp: (b, 0, hp)),
        out_shape=jax.ShapeDtypeStruct((bsz, seq, HEADS * HEAD_DIM), BF16),
        scratch_shapes=[qkv, qkv, qkv, stat, stat,
                        pltpu.VMEM((2, WINDOW_KEYS, 2 * WINDOW_KEYS), F32)],
        compiler_params=_params(2, ATTN_VMEM_LIMIT),
        name="attention",
    )(*hs, *[w_qkv] * N_PATTERNS, *tables, q_norm_w.reshape(1, HEAD_DIM), k_norm_w.reshape(1, HEAD_DIM))


def _qkv_cast_kernel(*refs):
    o_ref = refs[-1]
    for slot, ref in enumerate(refs[:-1]):
        o_ref[:, slot * HEAD_DIM:(slot + 1) * HEAD_DIM] = ref[...].astype(BF16)


def _pair_major_qkv(w_all, layer):
    n_heads = N_PATTERNS * HEADS
    part = lambda hl, p: pl.BlockSpec((None, D_MODEL, HEAD_DIM),
                                      lambda j: (layer, 0, p * n_heads + COPY_HEADS * j + hl))
    specs = [part(hl, p) for hl in range(COPY_HEADS) for p in range(3)]
    return pl.pallas_call(
        _qkv_cast_kernel,
        grid=(n_heads // COPY_HEADS,),
        in_specs=specs,
        out_specs=pl.BlockSpec((D_MODEL, COPY_HEADS * 3 * HEAD_DIM), lambda j: (0, j)),
        out_shape=jax.ShapeDtypeStruct((D_MODEL, 3 * ATTN_QKV), BF16),
        compiler_params=_params(1),
        name="qkv_weights",
    )(*[w_all] * len(specs))


def kernel(x, c, positions, mod_w, mod_b, norm_w, even_w_in, conv_dw_w, conv_dw_b, conv_ln_w, conv_ln_b, ssm_lam_re, ssm_lam_im, ssm_log_dt, ssm_b_re, ssm_b_im, ssm_c_re, ssm_c_im, ssm_d, ssm_glu_w, ssm_glu_b, even_w_out, attn_w_in, attn_q_norm_w, attn_k_norm_w, attn_w_out):
    bsz, seq, _ = x.shape
    assert seq == SEQ_LEN and x.shape[-1] == D_MODEL and bsz % SSM_BATCH == 0
    mod = _modulation(c, mod_w, mod_b)
    shift = [mod[l, :, None, :D_MODEL] for l in range(DEPTH)]
    scale = [mod[l, :, None, D_MODEL:2 * D_MODEL] for l in range(DEPTH)]
    gate = [mod[l, :, None, 2 * D_MODEL:] for l in range(DEPTH)]
    nw = [norm_w[l].reshape(1, D_MODEL) for l in range(DEPTH)]
    assert DEPTH % 2 == 0
    tables = _rope_tables(positions)
    ssm_mats = jax.vmap(_ssm_matrices)(ssm_lam_re, ssm_lam_im, ssm_log_dt, ssm_b_re, ssm_b_im, ssm_c_re, ssm_c_im)

    hs = None
    for layer in range(DEPTH):
        i = layer // 2
        nxt = (nw[layer + 1], scale[layer + 1], shift[layer + 1]) if layer + 1 < DEPTH else None
        if layer % 2 == 0:
            src, norm = (x, (nw[0], scale[0], shift[0])) if layer == 0 else (hs[0], None)
            ya, u, bz = _even_in(src, even_w_in[i].astype(BF16), conv_dw_w[i], conv_dw_b[i],
                                 conv_ln_w[i], conv_ln_b[i], norm)
            ys = _ssm(u, ssm_mats, i, ssm_d[i])
            w_out = even_w_out[i].astype(BF16)
            x, hs = _tail_call(_even_out_kernel, "even_out", x, [ya, ys, bz],
                               [ssm_glu_w[i].astype(BF16), ssm_glu_b[i].reshape(1, SSM_WIDTH),
                                w_out[:CONV_WIDTH], w_out[CONV_WIDTH:]], gate[layer], nxt, permuted=True)
        else:
            o = _attention(hs, _pair_major_qkv(attn_w_in, i), tables, attn_q_norm_w[i], attn_k_norm_w[i])
            width = HEADS * HEAD_DIM
            gate_cols = pl.BlockSpec((None, D_MODEL, width), lambda b, t, i=i: (i, 0, 3 * ATTN_QKV // width))
            x, hs = _tail_call(_odd_out_kernel, "odd_out", x, [hs[0], o],
                               [(attn_w_in, gate_cols), attn_w_out[i].astype(BF16)],
                               gate[layer], nxt, permuted=False)
    return x
```

```python
import functools
import math

import jax
import jax.numpy as jnp
from jax import lax
from jax.experimental import pallas as pl
from jax.experimental.pallas import tpu as pltpu

F32 = jnp.float32
BF16 = jnp.bfloat16

D_MODEL = 1024
DEPTH = 4
CONV_WIDTH = 512
CONV_KERNEL = 31
SSM_WIDTH = 512
SSM_GROUP = 16
SSM_STATE = 64
HEAD_DIM = 128
SEQ_LEN = 2048
ATTN_PATTERNS = ((128, 1), (512, 4), (SEQ_LEN, 16))
N_PATTERNS = 3
HEADS = 8
ATTN_QKV = N_PATTERNS * HEADS * HEAD_DIM
ATTN_SCALE = HEAD_DIM ** -0.5
ROPE_THETA = 10000.0
EPS = 1e-6
NEG_INF = -1e30

LANES = 128
WINDOW_KEYS = 128
SSM_CHUNK = 16
SSM_LANE_GROUPS = LANES // SSM_GROUP
SSM_BLOCKS = SSM_WIDTH // LANES
SSM_STATE_COLS = SSM_LANE_GROUPS * SSM_STATE
SSM_BATCH = 2
ROW_TILE = 1024
CONV_HALO = 32
CONV_ROWS = 64
PROJ_ROWS = 256
VMEM_LIMIT = 56 * 1024 * 1024
ATTN_VMEM_LIMIT = 58 * 1024 * 1024
HEAD_PAIR = 2


def _params(n_axes, vmem=VMEM_LIMIT):
    return pltpu.CompilerParams(dimension_semantics=("arbitrary",) * n_axes, vmem_limit_bytes=vmem)


def _adaln(x, nw, scale, shift):
    ms = jnp.mean(x * x, axis=-1, keepdims=True)
    return (x * lax.rsqrt(ms + EPS) * nw) * (1.0 + scale) + shift


def _dot(a, b):
    return jnp.dot(a, b, preferred_element_type=F32)


def _mod_kernel(c_ref, w_ref, b_ref, o_ref):
    o_ref[0] = _dot(c_ref[...].astype(BF16), w_ref[0].astype(BF16)) + b_ref[0]


def _modulation(c, mod_w, mod_b):
    bsz = c.shape[0]
    nblk = 3
    return pl.pallas_call(
        _mod_kernel,
        grid=(DEPTH, nblk),
        in_specs=[pl.BlockSpec((bsz, D_MODEL), lambda l, j: (0, 0)),
                  pl.BlockSpec((1, D_MODEL, D_MODEL), lambda l, j: (l, 0, j)),
                  pl.BlockSpec((1, 1, D_MODEL), lambda l, j: (l, 0, j))],
        out_specs=pl.BlockSpec((1, bsz, D_MODEL), lambda l, j: (l, 0, j)),
        out_shape=jax.ShapeDtypeStruct((DEPTH, bsz, 3 * D_MODEL), F32),
        compiler_params=_params(2),
        name="modulation",
    )(c, mod_w, mod_b.reshape(DEPTH, 1, 3 * D_MODEL))


def _rope_kernel(pos_ref, inv_ref, tab_ref, tab1_ref, tab2_ref, stage_scr, stage2_scr):
    half_rows, half_lanes = ROW_TILE // 2, HEAD_DIM // 2
    pos = pos_ref[0].astype(F32)
    lower = lax.broadcasted_iota(jnp.int32, (half_rows, HEAD_DIM), 1) < half_lanes
    ang = jnp.where(lower, pos[:half_rows], pos[half_rows:]) * inv_ref[...]
    cos, sin = jnp.cos(ang), jnp.sin(ang)
    tab = jnp.concatenate([jnp.where(lower, cos, pltpu.roll(sin, half_lanes, 1)),
                           jnp.where(lower, pltpu.roll(cos, half_lanes, 1), sin)], axis=0)
    tab_ref[0] = tab
    d1, d2 = ATTN_PATTERNS[1][1], ATTN_PATTERNS[2][1]
    assert d2 == d1 * d1
    n1, n2 = ROW_TILE // d1, ROW_TILE // d2
    stage_scr[...] = tab
    for r in range(d1):
        part = stage_scr[pl.ds(r, n1, stride=d1), :]
        tab1_ref[0, r] = part
        stage2_scr[r] = part
    for r in range(d1):
        for a in range(d1):
            tab2_ref[0, a * d1 + r] = stage2_scr[r, pl.ds(a, n2, stride=d1), :]


def _rope_tables(positions):
    bsz, seq = positions.shape
    inv = ROPE_THETA ** (-jnp.arange(0, HEAD_DIM, 2, dtype=F32) / HEAD_DIM)
    inv2 = jnp.concatenate([inv, inv]).reshape(1, HEAD_DIM)
    d1, d2 = ATTN_PATTERNS[1][1], ATTN_PATTERNS[2][1]
    perm = lambda d: (pl.BlockSpec((1, d, ROW_TILE // d, HEAD_DIM), lambda b, i: (b, 0, i, 0)),
                      jax.ShapeDtypeStruct((bsz, d, seq // d, HEAD_DIM), F32))
    specs, shapes = zip((pl.BlockSpec((1, ROW_TILE, HEAD_DIM), lambda b, i: (b, i, 0)),
                         jax.ShapeDtypeStruct((bsz, seq, HEAD_DIM), F32)), perm(d1), perm(d2))
    tabs = pl.pallas_call(
        _rope_kernel,
        grid=(bsz, seq // ROW_TILE),
        in_specs=[pl.BlockSpec((1, ROW_TILE, 1), lambda b, i: (b, i, 0)),
                  pl.BlockSpec((1, HEAD_DIM), lambda b, i: (0, 0))],
        out_specs=list(specs),
        out_shape=list(shapes),
        scratch_shapes=[pltpu.VMEM((ROW_TILE, HEAD_DIM), F32), pltpu.VMEM((d1, ROW_TILE // d1, HEAD_DIM), F32)],
        compiler_params=_params(2),
        name="rope_tables",
    )(positions.reshape(bsz, seq, 1), inv2)
    return [t.reshape(bsz, seq, HEAD_DIM) for t in tabs]


def _even_in_kernel(*refs, from_x):
    if from_x:
        x_ref, nw_ref, sc_ref, sh_ref = refs[:4]
        h = _adaln(x_ref[0], nw_ref[...], sc_ref[0], sh_ref[0]).astype(BF16)
        refs = refs[4:]
    else:
        h = refs[0][0]
        refs = refs[1:]
    w_ref, dww_ref, dwb_ref, lnw_ref, lnb_ref, ya_ref, u_ref, bz_ref, conv_scr, y_scr, az_scr = refs
    i = pl.program_id(1)
    cw = CONV_WIDTH
    slabs = cw // LANES

    @pl.when(i == 0)
    def _():
        conv_scr[:, 0:CONV_HALO, :] = jnp.zeros((slabs, CONV_HALO, LANES), F32)

    a1 = _dot(h, w_ref[:, 0:cw])
    a2 = _dot(h, w_ref[:, cw:2 * cw])
    a = a1 * jax.nn.sigmoid(a2)
    for l in range(slabs):
        conv_scr[l, CONV_HALO:CONV_HALO + ROW_TILE, :] = a[:, l * LANES:(l + 1) * LANES]
    az_scr[...] = jax.nn.silu(_dot(h, w_ref[:, 2 * cw:3 * cw]))
    u_ref[0] = _dot(h, w_ref[:, 3 * cw:4 * cw])
    bz_ref[0] = _dot(h, w_ref[:, 4 * cw:5 * cw])

    first_tap = CONV_HALO - (CONV_KERNEL - 1)
    half = CONV_ROWS // 2

    def conv_block(r, carry):
        r0 = r * CONV_ROWS
        for l in range(slabs):
            lanes = slice(l * LANES, (l + 1) * LANES)
            for par in range(2):
                acc = jnp.broadcast_to(dwb_ref[:, lanes], (half, LANES))
                for k in range(CONV_KERNEL):
                    tap = conv_scr[l, pl.ds(r0 + first_tap + par + k, half, stride=2), :]
                    acc = acc + dww_ref[k:k + 1, lanes] * tap
                y_scr[l, pl.ds(r0 + par, half, stride=2), :] = acc
        return carry

    lax.fori_loop(0, ROW_TILE // CONV_ROWS, conv_block, 0)
    for l in range(slabs):
        conv_scr[l, 0:CONV_HALO, :] = conv_scr[l, ROW_TILE:ROW_TILE + CONV_HALO, :]

    def norm_block(r, carry):
        rows = pl.ds(pl.multiple_of(r * CONV_ROWS, CONV_ROWS), CONV_ROWS)
        acc = jnp.concatenate([y_scr[l, rows, :] for l in range(slabs)], axis=-1)
        mu = jnp.mean(acc, axis=-1, keepdims=True)
        xc = acc - mu
        y = xc * lax.rsqrt(jnp.mean(xc * xc, axis=-1, keepdims=True) + EPS)
        y = y * lnw_ref[...] + lnb_ref[...]
        ya_ref[0, rows, :] = (jax.nn.silu(y) * az_scr[rows, :]).astype(BF16)
        return carry

    lax.fori_loop(0, ROW_TILE // CONV_ROWS, norm_block, 0, unroll=4)


def _even_in(h, w_in, dw_w, dw_b, ln_w, ln_b, norm=None):
    bsz, seq, _ = h.shape
    cw = CONV_WIDTH
    const = lambda shape: pl.BlockSpec(shape, lambda b, i: (0,) * len(shape))
    row = lambda width: pl.BlockSpec((1, ROW_TILE, width), lambda b, i: (b, i, 0))
    vec = pl.BlockSpec((1, 1, D_MODEL), lambda b, i: (b, 0, 0))
    norm_args = [] if norm is None else list(norm)
    norm_specs = [] if norm is None else [const((1, D_MODEL)), vec, vec]
    return pl.pallas_call(
        functools.partial(_even_in_kernel, from_x=norm is not None),
        grid=(bsz, seq // ROW_TILE),
        in_specs=[row(D_MODEL), *norm_specs, const(w_in.shape), const((CONV_KERNEL, cw)),
                  const((1, cw)), const((1, cw)), const((1, cw))],
        out_specs=[row(cw), row(cw), row(cw)],
        out_shape=[jax.ShapeDtypeStruct((bsz, seq, cw), BF16),
                   jax.ShapeDtypeStruct((bsz, seq, cw), F32),
                   jax.ShapeDtypeStruct((bsz, seq, cw), F32)],
        scratch_shapes=[pltpu.VMEM((cw // LANES, CONV_HALO + ROW_TILE, LANES), F32),
                        pltpu.VMEM((cw // LANES, ROW_TILE, LANES), F32),
                        pltpu.VMEM((ROW_TILE, cw), F32)],
        compiler_params=_params(2),
        name="even_in",
    )(h, *norm_args, w_in, dw_w, dw_b.reshape(1, cw), ln_w.reshape(1, cw), ln_b.reshape(1, cw))


def _ssm_matrices(lam_re, lam_im, log_dt, b_re, b_im, c_re, c_im):
    hp = lax.Precision.HIGHEST
    t = SSM_CHUNK
    lr, li = lam_re.astype(F32), lam_im.astype(F32)
    dt = jnp.exp(log_dt.astype(F32))[:, None]

    def a_pow(k):
        kk = k.astype(F32)[:, None, None]
        mag = jnp.exp(kk * (lr * dt)[None])
        ang = kk * (li * dt)[None]
        return mag * jnp.cos(ang), mag * jnp.sin(ang)

    ar, ai = a_pow(jnp.ones((1,), F32))
    ar, ai = ar[0], ai[0]
    den = lr * lr + li * li
    nr = ar - 1.0
    kr = (nr * lr + ai * li) / den
    ki = (ai * lr - nr * li) / den
    br, bi = b_re.astype(F32), b_im.astype(F32)
    bbr = kr[..., None] * br - ki[..., None] * bi
    bbi = kr[..., None] * bi + ki[..., None] * br
    cr, ci = c_re.astype(F32), c_im.astype(F32)

    pr, pi = a_pow(jnp.arange(t + 1))
    wr = pr[:t, :, :, None] * bbr[None] - pi[:t, :, :, None] * bbi[None]
    wi = pr[:t, :, :, None] * bbi[None] + pi[:t, :, :, None] * bbr[None]
    kk = (jnp.einsum('gop,kgpi->kgoi', cr, wr, precision=hp)
          - jnp.einsum('gop,kgpi->kgoi', ci, wi, precision=hp))

    nb, gl, hh, pp = SSM_BLOCKS, SSM_LANE_GROUPS, SSM_GROUP, SSM_STATE
    grp = jnp.arange(gl)[:, None, None]
    row = jnp.arange(t * hh)[None, :, None]
    col = jnp.arange(t * LANES)[None, None, :]
    place = ((row // hh == col // LANES) & (row % hh == col % hh) & ((col % LANES) // hh == grp)).astype(BF16)
    prow = jnp.arange(pp)[None, :, None]
    pcol = jnp.arange(gl * pp)[None, None, :]
    spread = ((pcol % pp == prow) & (pcol // pp == grp)).astype(BF16)

    def rows_sgi(m):
        n = m.shape[-1]
        return m.reshape(nb, gl, t, hh, n).transpose(0, 2, 1, 3, 4).reshape(nb, t * LANES, n)

    lrow = jnp.arange(t * hh)[None, :, None]
    lcol = jnp.arange(t * hh)[None, None, :]
    s_idx = jnp.arange(t)[:, None, None]
    shift = ((lcol // hh == lrow // hh + s_idx) & (lcol % hh == lrow % hh)).astype(BF16)
    kk_i = kk.reshape(t, nb, gl, hh, hh).transpose(1, 2, 4, 0, 3).reshape(nb, gl, hh, t * hh)
    toe = jnp.einsum('cgil,sln->cgsin', kk_i.astype(BF16), shift,
                     preferred_element_type=BF16).reshape(nb, gl, t * hh, t * hh)
    m_intra = rows_sgi(jnp.einsum('cgab,gbn->cgan', toe, place, preferred_element_type=BF16))

    def state_cols(w):
        w = w[::-1].reshape(t, nb, gl, pp, hh).transpose(1, 2, 0, 4, 3).reshape(nb, gl, t * hh, pp)
        return rows_sgi(jnp.einsum('cgap,gpn->cgan', w.astype(BF16), spread, preferred_element_type=BF16))

    m_state_r, m_state_i = state_cols(wr), state_cols(wi)

    def in_rows(q):
        q = q.reshape(t, nb, gl, hh, pp).transpose(1, 2, 4, 0, 3).reshape(nb, gl, pp, t * hh)
        return jnp.einsum('cgpb,gbn->cgpn', q.astype(BF16), place,
                          preferred_element_type=BF16).reshape(nb, gl * pp, t * LANES)

    m_in_r = in_rows(pr[1:, :, None, :] * cr[None] - pi[1:, :, None, :] * ci[None])
    m_in_i = in_rows(-(pr[1:, :, None, :] * ci[None] + pi[1:, :, None, :] * cr[None]))

    n_lvl = int(math.log2(SEQ_LEN // t))
    sr2, si2 = a_pow(t * (2 ** jnp.arange(n_lvl)))
    ap_r = sr2.reshape(n_lvl, nb, gl * pp).transpose(1, 0, 2)
    ap_i = si2.reshape(n_lvl, nb, gl * pp).transpose(1, 0, 2)
    return m_intra, m_state_r, m_state_i, m_in_r, m_in_i, ap_r, ap_i


def _ssm_kernel(u_ref, mintra_ref, msr_ref, msi_ref, minr_ref, mini_ref, apr_ref, api_ref, d_ref, y_ref,
                x_scr, yi_scr, cr_scr, ci_scr, zr_scr, zi_scr, *, n_chunks, n_levels):
    t = SSM_CHUNK
    sc = SSM_STATE_COLS
    pair = 2 * LANES
    for bb in range(SSM_BATCH):
        for t0 in range(t):
            x_scr[bb * n_chunks:(bb + 1) * n_chunks, t0 * LANES:(t0 + 1) * LANES] = (
                u_ref[bb, pl.ds(t0, n_chunks, stride=t), :].astype(BF16))
    halves = [slice(0, sc // 2), slice(sc // 2, sc)]
    bb_rows = [slice(bb * n_chunks, (bb + 1) * n_chunks) for bb in range(SSM_BATCH)]
    s_re = [[_dot(x_scr[r, :], msr_ref[0, :, hc]) for hc in halves] for r in bb_rows]
    s_im = [[_dot(x_scr[r, :], msi_ref[0, :, hc]) for hc in halves] for r in bb_rows]
    for tp in range(t // 2):
        cols = slice(tp * pair, (tp + 1) * pair)
        for r in bb_rows:
            yi_scr[r, cols] = _dot(x_scr[r, 0:(tp + 1) * pair], mintra_ref[0, 0:(tp + 1) * pair, cols])

    zero = jnp.zeros((n_chunks, sc), F32)
    for p in range(2):
        zr_scr[p, 0:n_chunks, :] = zero
        zi_scr[p, 0:n_chunks, :] = zero
    for bb in range(SSM_BATCH):
        rows = slice(bb * n_chunks, (bb + 1) * n_chunks)
        for hc, pr_, pi_ in zip(halves, s_re[bb], s_im[bb]):
            zr_scr[0, n_chunks:, hc] = pr_
            zi_scr[0, n_chunks:, hc] = pi_
        for k in range(n_levels):
            src, dst = k % 2, 1 - (k % 2)
            sh = n_chunks - (1 << k)
            zr = zr_scr[src, n_chunks:, :]
            zi = zi_scr[src, n_chunks:, :]
            pr = zr_scr[src, sh:sh + n_chunks, :]
            pi = zi_scr[src, sh:sh + n_chunks, :]
            ar = apr_ref[0, k:k + 1, :]
            ai = api_ref[0, k:k + 1, :]
            zr_scr[dst, n_chunks:, :] = zr + ar * pr - ai * pi
            zi_scr[dst, n_chunks:, :] = zi + ar * pi + ai * pr
        fin = n_levels % 2
        cr_scr[rows, :] = zr_scr[fin, n_chunks - 1:2 * n_chunks - 1, :].astype(BF16)
        ci_scr[rows, :] = zi_scr[fin, n_chunks - 1:2 * n_chunks - 1, :].astype(BF16)

    for tp in range(t // 2):
        cols = slice(tp * pair, (tp + 1) * pair)
        for bb, r in enumerate(bb_rows):
            yc = (yi_scr[r, cols] + _dot(cr_scr[r, :], minr_ref[0, :, cols])
                  + _dot(ci_scr[r, :], mini_ref[0, :, cols]))
            for j in range(2):
                y_ref[bb, pl.ds(2 * tp + j, n_chunks, stride=t), :] = yc[:, j * LANES:(j + 1) * LANES]
    for bb in range(SSM_BATCH):
        y_ref[bb] = y_ref[bb] + d_ref[0] * u_ref[bb]


def _ssm(u, mats, layer, d_skip):
    m_intra, m_state_r, m_state_i, m_in_r, m_in_i, ap_r, ap_i = mats
    bsz, seq, _ = u.shape
    n_chunks = seq // SSM_CHUNK
    n_levels = ap_r.shape[2]
    nb = SSM_BATCH
    once = pl.Buffered(1)
    blk = pl.BlockSpec((nb, seq, LANES), lambda c, b: (b, 0, c))
    mat = lambda m: pl.BlockSpec((None, 1) + m.shape[2:], lambda c, b: (layer, c, 0, 0), pipeline_mode=once)
    lvl = pl.BlockSpec((None, 1, n_levels, SSM_STATE_COLS), lambda c, b: (layer, c, 0, 0))
    return pl.pallas_call(
        functools.partial(_ssm_kernel, n_chunks=n_chunks, n_levels=n_levels),
        grid=(SSM_BLOCKS, bsz // nb),
        in_specs=[blk, mat(m_intra), mat(m_state_r), mat(m_state_i), mat(m_in_r), mat(m_in_i), lvl, lvl,
                  pl.BlockSpec((1, 1, LANES), lambda c, b: (c, 0, 0))],
        out_specs=blk,
        out_shape=jax.ShapeDtypeStruct(u.shape, F32),
        scratch_shapes=[pltpu.VMEM((nb * n_chunks, SSM_CHUNK * LANES), BF16),
                        pltpu.VMEM((nb * n_chunks, SSM_CHUNK * LANES), F32),
                        pltpu.VMEM((nb * n_chunks, SSM_STATE_COLS), BF16),
                        pltpu.VMEM((nb * n_chunks, SSM_STATE_COLS), BF16),
                        pltpu.VMEM((2, 2 * n_chunks, SSM_STATE_COLS), F32),
                        pltpu.VMEM((2, 2 * n_chunks, SSM_STATE_COLS), F32)],
        compiler_params=_params(2),
        name="ssm",
    )(u, m_intra, m_state_r, m_state_i, m_in_r, m_in_i, ap_r, ap_i, d_skip.reshape(SSM_BLOCKS, 1, LANES))


def _finish(x_ref, gate_ref, out, nxt, xo_ref, h_outs, stage_scr, stage2_scr):
    xn = x_ref[0] + gate_ref[0] * out
    xo_ref[0] = xn
    if nxt is None:
        return
    nw_ref, sc_ref, sh_ref = nxt
    hn = _adaln(xn, nw_ref[...], sc_ref[0], sh_ref[0])
    h_outs[0][0] = hn.astype(BF16)
    if len(h_outs) == 1:
        return
    slabs = D_MODEL // LANES
    d1, d2 = ATTN_PATTERNS[1][1], ATTN_PATTERNS[2][1]
    assert d2 == d1 * d1 and len(h_outs) == 3
    n1, n2 = ROW_TILE // d1, ROW_TILE // d2
    for l in range(slabs):
        stage_scr[l] = hn[:, l * LANES:(l + 1) * LANES]
    for r in range(d1):
        parts = [stage_scr[l, pl.ds(r, n1, stride=d1), :] for l in range(slabs)]
        h_outs[1][0, r] = jnp.concatenate(parts, axis=-1).astype(BF16)
        for l in range(slabs):
            stage2_scr[r * slabs + l] = parts[l]
    for r in range(d1):
        for a in range(d1):
            parts = [stage2_scr[r * slabs + l, pl.ds(a, n2, stride=d1), :] for l in range(slabs)]
            h_outs[2][0, a * d1 + r] = jnp.concatenate(parts, axis=-1).astype(BF16)


def _even_out_kernel(*refs, n_h):
    x_ref, ya_ref, ys_ref, bz_ref, gw_ref, gb_ref, wa_ref, ws_ref, gate_ref = refs[:9]
    nxt, xo_ref, h_outs, stages = _tail_refs(refs[9:], n_h)
    y = ys_ref[0]
    y = 0.5 * y * (1.0 + lax.erf(y * (2.0 ** -0.5)))
    y = y * jax.nn.sigmoid(_dot(y.astype(BF16), gw_ref[...]) + gb_ref[...])
    y = y * jax.nn.silu(bz_ref[0])
    out = _dot(ya_ref[0], wa_ref[...]) + _dot(y.astype(BF16), ws_ref[...])
    _finish(x_ref, gate_ref, out, nxt, xo_ref, h_outs, *stages)


def _odd_out_kernel(*refs, n_h):
    x_ref, h_ref, o_ref, wz_ref, wo_ref, gate_ref = refs[:6]
    nxt, xo_ref, h_outs, stages = _tail_refs(refs[6:], n_h)
    z = _dot(h_ref[0], wz_ref[...].astype(BF16))
    g = o_ref[0] * jax.nn.silu(z)
    out = _dot(g.astype(BF16), wo_ref[...])
    _finish(x_ref, gate_ref, out, nxt, xo_ref, h_outs, *stages)


def _tail_refs(refs, n_h):
    if n_h == 0:
        return None, refs[0], (), (None, None)
    nxt, xo_ref, h_outs = refs[:3], refs[3], refs[4:4 + n_h]
    return nxt, xo_ref, h_outs, (tuple(refs[4 + n_h:6 + n_h]) if n_h > 1 else (None, None))


def _tail_call(body, name, x, rows, consts, gate, nxt, permuted):
    bsz, seq, _ = x.shape
    n_h = 0 if nxt is None else (N_PATTERNS if permuted else 1)
    row = lambda a: pl.BlockSpec((1, ROW_TILE, a.shape[-1]), lambda b, i: (b, i, 0))
    const = lambda a: pl.BlockSpec(a.shape, lambda b, i: (0,) * a.ndim)
    vec = pl.BlockSpec((1, 1, D_MODEL), lambda b, i: (b, 0, 0))
    consts = [a if isinstance(a, tuple) else (a, const(a)) for a in consts]
    args = [x, *rows, *[a for a, _ in consts], gate]
    in_specs = [row(x)] + [row(a) for a in rows] + [spec for _, spec in consts] + [vec]
    out_specs = [row(x)]
    out_shape = [jax.ShapeDtypeStruct(x.shape, F32)]
    scratch = []
    if n_h:
        nw, sc, sh = nxt
        args += [nw, sc, sh]
        in_specs += [const(nw), vec, vec]
        out_specs.append(row(x))
        out_shape.append(jax.ShapeDtypeStruct(x.shape, BF16))
    if n_h > 1:
        for _, dil in ATTN_PATTERNS[1:]:
            out_specs.append(pl.BlockSpec((1, dil, ROW_TILE // dil, D_MODEL), lambda b, i: (b, 0, i, 0)))
            out_shape.append(jax.ShapeDtypeStruct((bsz, dil, seq // dil, D_MODEL), BF16))
        d1 = ATTN_PATTERNS[1][1]
        scratch.append(pltpu.VMEM((D_MODEL // LANES, ROW_TILE, LANES), F32))
        scratch.append(pltpu.VMEM((d1 * D_MODEL // LANES, ROW_TILE // d1, LANES), F32))
    res = pl.pallas_call(
        functools.partial(body, n_h=n_h),
        grid=(bsz, seq // ROW_TILE),
        in_specs=in_specs, out_specs=out_specs, out_shape=out_shape, scratch_shapes=scratch,
        compiler_params=_params(2),
        name=name,
    )(*args)
    return res[0], [h.reshape(x.shape) for h in res[1:]]


def _attn_kernel(h0_ref, h1_ref, h2_ref, *refs, seq):
    w_refs = refs[:N_PATTERNS]
    (t0_ref, t1_ref, t2_ref, qw_ref, kw_ref, o_ref,
     q_scr, k_scr, v_scr, on_scr, lse_scr, bias_scr) = refs[N_PATTERNS:]
    wk = WINDOW_KEYS
    n_chunks = seq // PROJ_ROWS
    chunk_blocks = PROJ_ROWS // wk
    h_refs = (h0_ref, h1_ref, h2_ref)
    tabs = (t0_ref, t1_ref, t2_ref)

    qi = lax.broadcasted_iota(jnp.int32, (wk, 2 * wk), 0)
    kj = lax.broadcasted_iota(jnp.int32, (wk, 2 * wk), 1)
    band = (kj >= qi) & (kj <= qi + wk)
    bias_scr[0] = jnp.where(band, 0.0, NEG_INF)
    bias_scr[1] = jnp.where(band & (kj >= wk), 0.0, NEG_INF)

    def chunk_rows(c):
        start = c * PROJ_ROWS
        if not isinstance(c, int):
            start = pl.multiple_of(start, PROJ_ROWS)
        return pl.ds(start, PROJ_ROWS)

    def proj_dots(c):
        rows = chunk_rows(c)
        return [_dot(h_refs[g][0, rows, :], w_refs[g][...]) for g in range(N_PATTERNS)]

    def proj_store(c, prs):
        rows = chunk_rows(c)
        for g, pr in enumerate(prs):
            tab = tabs[g][0, rows, :]
            swapped = pltpu.roll(tab, HEAD_DIM // 2, 1)
            lower = lax.broadcasted_iota(jnp.int32, tab.shape, 1) < HEAD_DIM // 2
            cos = jnp.where(lower, tab, swapped)
            sin = jnp.where(lower, -swapped, tab)

            def norm_rope(t, w):
                t = t * lax.rsqrt(jnp.mean(t * t, axis=-1, keepdims=True) + EPS) * w
                return t * cos + pltpu.roll(t, HEAD_DIM // 2, 1) * sin

            for hl in range(HEAD_PAIR):
                off = hl * 3 * HEAD_DIM
                slot = g * HEAD_PAIR + hl
                q_scr[slot, rows, :] = (norm_rope(pr[:, off:off + HEAD_DIM], qw_ref[...]) * ATTN_SCALE).astype(BF16)
                k_scr[slot, rows, :] = norm_rope(pr[:, off + HEAD_DIM:off + 2 * HEAD_DIM], kw_ref[...]).astype(BF16)
                v_scr[slot, rows, :] = pr[:, off + 2 * HEAD_DIM:off + 3 * HEAD_DIM].astype(BF16)

    def block_ids(c):
        return [(g, hl, chunk_blocks * c + u, u) for g in range(N_PATTERNS) for hl in range(HEAD_PAIR)
                for u in range(chunk_blocks)]

    def is_first(g, j, u):
        n_blk = seq // ATTN_PATTERNS[g][1] // wk
        if n_blk == 1:
            return True
        if u % 2 == 1:
            return False
        if isinstance(j, int):
            return j % n_blk == 0
        return (j % n_blk) == 0

    def block_start(j):
        start = j * wk
        return start if isinstance(start, int) else pl.multiple_of(start, wk)

    def scores(g, hl, j, u):
        first = is_first(g, j, u)
        slot = g * HEAD_PAIR + hl
        q = q_scr[slot, pl.ds(block_start(j), wk), :]
        if first is True:
            kk = k_scr[slot, pl.ds(block_start(j), wk), :]
            bias = bias_scr[1, :, wk:]
        else:
            kk = k_scr[slot, pl.ds(block_start(j - 1), 2 * wk), :]
            bias = bias_scr[0] if first is False else bias_scr[jnp.where(first, 1, 0)]
        return lax.dot_general(q, kk, (((1,), (1,)), ((), ())), preferred_element_type=F32) + bias

    def softmax(s):
        m = jnp.max(s, axis=-1, keepdims=True)
        return jnp.exp(s - m).astype(BF16), m

    def weighted(g, hl, j, u, p):
        slot = g * HEAD_PAIR + hl
        if is_first(g, j, u) is True:
            vv = v_scr[slot, pl.ds(block_start(j), wk), :]
        else:
            vv = v_scr[slot, pl.ds(block_start(j - 1), 2 * wk), :]
        ov = _dot(p, jnp.concatenate([vv, jnp.ones_like(vv)], axis=1))
        return ov[:, :HEAD_DIM], ov[:, HEAD_DIM:]

    def attn_store(g, hl, j, o, m, den):
        dil = ATTN_PATTERNS[g][1]
        n_blk = seq // dil // wk
        slot = g * HEAD_PAIR + hl
        if dil == 1:
            nat = pl.ds(block_start(j), wk)
        else:
            nat = pl.ds((j % n_blk) * (wk * dil) + j // n_blk, wk, stride=dil)
        on_scr[slot, nat, :] = o * (1.0 / den)
        lse_scr[slot, nat, :] = m + jnp.log(den)

    def step(c_attn, c_proj):
        ids = block_ids(c_attn) if c_attn is not None else []
        ss = [scores(g, hl, j, u) for g, hl, j, u in ids]
        prs = proj_dots(c_proj) if c_proj is not None else None
        sm = [softmax(s) for s in ss]
        os_ = [weighted(g, hl, j, u, p) for (g, hl, j, u), (p, _) in zip(ids, sm)]
        if prs is not None:
            proj_store(c_proj, prs)
        for (g, hl, j, u), (o, den), (_, m) in zip(ids, os_, sm):
            attn_store(g, hl, j, o, m, den)

    step(None, 0)
    step(0, 1)

    def body(c, carry):
        step(c - 1, c)
        return carry

    lax.fori_loop(2, n_chunks, body, 0)
    step(n_chunks - 1, None)

    def merge_body(c, carry):
        rows = pl.ds(pl.multiple_of(c * PROJ_ROWS, PROJ_ROWS), PROJ_ROWS)
        for hl in range(HEAD_PAIR):
            ls = [lse_scr[g * HEAD_PAIR + hl, rows, :] for g in range(N_PATTERNS)]
            mx = jnp.maximum(jnp.maximum(ls[0], ls[1]), ls[2])
            num = jnp.zeros((PROJ_ROWS, HEAD_DIM), F32)
            den = jnp.zeros((PROJ_ROWS, HEAD_DIM), F32)
            for g in range(N_PATTERNS):
                wgt = jnp.exp(ls[g] - mx)
                num = num + wgt * on_scr[g * HEAD_PAIR + hl, rows, :]
                den = den + wgt
            o_ref[0, rows, hl * HEAD_DIM:(hl + 1) * HEAD_DIM] = (num / den).astype(BF16)
        return carry

    lax.fori_loop(0, seq // PROJ_ROWS, merge_body, 0)


def _attention(hs, w_qkv, tables, q_norm_w, k_norm_w):
    bsz, seq, _ = hs[0].shape
    pair_cols = HEAD_PAIR * 3 * HEAD_DIM
    wspec = lambda g: pl.BlockSpec((D_MODEL, pair_cols), lambda b, hp: (0, g * (HEADS // HEAD_PAIR) + hp))
    slots = N_PATTERNS * HEAD_PAIR
    stat = pltpu.VMEM((slots, seq, HEAD_DIM), F32)
    qkv = pltpu.VMEM((slots, seq, HEAD_DIM), BF16)
    once = pl.Buffered(1)
    hspec = pl.BlockSpec((1, seq, D_MODEL), lambda b, hp: (b, 0, 0), pipeline_mode=once)
    tspec = pl.BlockSpec((1, seq, HEAD_DIM), lambda b, hp: (b, 0, 0), pipeline_mode=once)
    vec = pl.BlockSpec((1, HEAD_DIM), lambda b, hp: (0, 0))
    return pl.pallas_call(
        functools.partial(_attn_kernel, seq=seq),
        grid=(bsz, HEADS // HEAD_PAIR),
        in_specs=[hspec, hspec, hspec, *[wspec(g) for g in range(N_PATTERNS)],
                  tspec, tspec, tspec, vec, vec],
        out_specs=pl.BlockSpec((1, seq, HEAD_PAIR * HEAD_DIM), lambda b, hp: (b, 0, hp)),
        out_shape=jax.ShapeDtypeStruct((bsz, seq, HEADS * HEAD_DIM), BF16),
        scratch_shapes=[qkv, qkv, qkv, stat, stat,
                        pltpu.VMEM((2, WINDOW_KEYS, 2 * WINDOW_KEYS), F32)],
        compiler_params=_params(2, ATTN_VMEM_LIMIT),
        name="attention",
    )(*hs, *[w_qkv] * N_PATTERNS, *tables, q_norm_w.reshape(1, HEAD_DIM), k_norm_w.reshape(1, HEAD_DIM))


def _qkv_cast_kernel(*refs):
    o_ref = refs[-1]
    for slot, ref in enumerate(refs[:-1]):
        o_ref[:, slot * HEAD_DIM:(slot + 1) * HEAD_DIM] = ref[...].astype(BF16)


def _pair_major_qkv(w_all, layer):
    n_heads = N_PATTERNS * HEADS
    part = lambda hl, p: pl.BlockSpec((None, D_MODEL, HEAD_DIM),
                                      lambda j: (layer, 0, p * n_heads + HEAD_PAIR * j + hl))
    specs = [part(hl, p) for hl in range(HEAD_PAIR) for p in range(3)]
    return pl.pallas_call(
        _qkv_cast_kernel,
        grid=(n_heads // HEAD_PAIR,),
        in_specs=specs,
        out_specs=pl.BlockSpec((D_MODEL, HEAD_PAIR * 3 * HEAD_DIM), lambda j: (0, j)),
        out_shape=jax.ShapeDtypeStruct((D_MODEL, 3 * ATTN_QKV), BF16),
        compiler_params=_params(1),
        name="qkv_weights",
    )(*[w_all] * len(specs))


def kernel(x, c, positions, mod_w, mod_b, norm_w, even_w_in, conv_dw_w, conv_dw_b, conv_ln_w, conv_ln_b, ssm_lam_re, ssm_lam_im, ssm_log_dt, ssm_b_re, ssm_b_im, ssm_c_re, ssm_c_im, ssm_d, ssm_glu_w, ssm_glu_b, even_w_out, attn_w_in, attn_q_norm_w, attn_k_norm_w, attn_w_out):
    bsz, seq, _ = x.shape
    assert seq == SEQ_LEN and x.shape[-1] == D_MODEL and bsz % SSM_BATCH == 0
    mod = _modulation(c, mod_w, mod_b)
    shift = [mod[l, :, None, :D_MODEL] for l in range(DEPTH)]
    scale = [mod[l, :, None, D_MODEL:2 * D_MODEL] for l in range(DEPTH)]
    gate = [mod[l, :, None, 2 * D_MODEL:] for l in range(DEPTH)]
    nw = [norm_w[l].reshape(1, D_MODEL) for l in range(DEPTH)]
    assert DEPTH % 2 == 0
    tables = _rope_tables(positions)
    ssm_mats = jax.vmap(_ssm_matrices)(ssm_lam_re, ssm_lam_im, ssm_log_dt, ssm_b_re, ssm_b_im, ssm_c_re, ssm_c_im)

    hs = None
    for layer in range(DEPTH):
        i = layer // 2
        nxt = (nw[layer + 1], scale[layer + 1], shift[layer + 1]) if layer + 1 < DEPTH else None
        if layer % 2 == 0:
            src, norm = (x, (nw[0], scale[0], shift[0])) if layer == 0 else (hs[0], None)
            ya, u, bz = _even_in(src, even_w_in[i].astype(BF16), conv_dw_w[i], conv_dw_b[i],
                                 conv_ln_w[i], conv_ln_b[i], norm)
            ys = _ssm(u, ssm_mats, i, ssm_d[i])
            w_out = even_w_out[i].astype(BF16)
            x, hs = _tail_call(_even_out_kernel, "even_out", x, [ya, ys, bz],
                               [ssm_glu_w[i].astype(BF16), ssm_glu_b[i].reshape(1, SSM_WIDTH),
                                w_out[:CONV_WIDTH], w_out[CONV_WIDTH:]], gate[layer], nxt, permuted=True)
        else:
            o = _attention(hs, _pair_major_qkv(attn_w_in, i), tables, attn_q_norm_w[i], attn_k_norm_w[i])
            width = HEADS * HEAD_DIM
            gate_cols = pl.BlockSpec((None, D_MODEL, width), lambda b, t, i=i: (i, 0, 3 * ATTN_QKV // width))
            x, hs = _tail_call(_odd_out_kernel, "odd_out", x, [hs[0], o],
                               [(attn_w_in, gate_cols), attn_w_out[i].astype(BF16)],
                               gate[layer], nxt, permuted=False)
    return x
```
